```python
import jax
import jax.numpy as jnp
from jax import lax
import numpy as np

D_MODEL = 2048
BATCH = 16
SEQ = 2048
DEPTH = 2

GRID_W = 64
CTX_LEN = 256
N_MIXERS = 2
NA_HEADS = 16
NA_HEAD_DIM = D_MODEL // NA_HEADS
WIN_ROWS = 8
WIN_COLS = 16
D_RNN = D_MODEL
RG_BLOCKS = 8
RG_BLOCK_W = D_RNN // RG_BLOCKS
RG_C = 8.0
CONV_W = 4
CONV_PAD = (2, 1)
N_EXPERTS = 32
TOP_K = 4
D_EXPERT = D_MODEL
SWIGLU_ALPHA = 1.702
SWIGLU_LIMIT = 7.0
MOE_BLOCK = 256
RMS_EPS = 1e-6

kernel_name = "hybrid_na_rglru_moe_dit"


def rmsnorm(x, g):
    xf = x.astype(jnp.float32)
    y = xf * lax.rsqrt(jnp.mean(xf * xf, axis=-1, keepdims=True) + RMS_EPS)
    return (y * g.astype(jnp.float32)).astype(x.dtype)


def modulate(h, shift, scale):
    return h * (1 + scale[:, None, :]) + shift[:, None, :]


def na_mixer(hl, hc, w_qkv, w_o, rpb, ctx_out):
    b, n, d = hl.shape
    n_ctx = hc.shape[1]
    rows = n // GRID_W
    kr = min(WIN_ROWS, rows)
    scale = NA_HEAD_DIM ** -0.5
    q_l, k_l, v_l = jnp.moveaxis((hl @ w_qkv).reshape(b, rows, GRID_W, 3, NA_HEADS, NA_HEAD_DIM), 3, 0)
    q_c, k_c, v_c = jnp.moveaxis((hc @ w_qkv).reshape(b, n_ctx, 3, NA_HEADS, NA_HEAD_DIM), 2, 0)
    col = np.arange(GRID_W)
    col_start = np.clip(col - WIN_COLS // 2, 0, GRID_W - WIN_COLS)
    col_mask = (col[None, :] >= col_start[:, None]) & (col[None, :] < col_start[:, None] + WIN_COLS)
    band_mask = np.broadcast_to(col_mask[:, None, :], (GRID_W, kr, GRID_W)).reshape(GRID_W, kr * GRID_W)
    dc_idx = np.clip(col[None, :] - col[:, None], 1 - WIN_COLS, WIN_COLS - 1) + WIN_COLS - 1
    rpb_cols = rpb[:, :, dc_idx]

    def row_block(r):
        rs = jnp.clip(r - kr // 2, 0, rows - kr)
        q = lax.dynamic_index_in_dim(q_l, r, axis=1, keepdims=False)
        kb = lax.dynamic_slice_in_dim(k_l, rs, kr, axis=1).reshape(b, kr * GRID_W, NA_HEADS, NA_HEAD_DIM)
        vb = lax.dynamic_slice_in_dim(v_l, rs, kr, axis=1).reshape(b, kr * GRID_W, NA_HEADS, NA_HEAD_DIM)
        dr_idx = rs + jnp.arange(kr) - r + WIN_ROWS - 1
        bias = jnp.take(rpb_cols, dr_idx, axis=1).transpose(0, 2, 1, 3).reshape(NA_HEADS, GRID_W, kr * GRID_W)
        s_lat = jnp.einsum('bqhd,bkhd->bhqk', q, kb).astype(jnp.float32) * scale + bias.astype(jnp.float32)
        s_lat = jnp.where(band_mask, s_lat, -jnp.inf)
        s_ctx = jnp.einsum('bqhd,bkhd->bhqk', q, k_c).astype(jnp.float32) * scale
        p = jax.nn.softmax(jnp.concatenate([s_lat, s_ctx], axis=-1), axis=-1).astype(vb.dtype)
        return (jnp.einsum('bhqk,bkhd->bqhd', p[..., :kr * GRID_W], vb)
                + jnp.einsum('bhqk,bkhd->bqhd', p[..., kr * GRID_W:], v_c))

    o_l = lax.map(row_block, jnp.arange(rows))
    y_l = jnp.moveaxis(o_l, 0, 1).reshape(b, n, d) @ w_o
    if not ctx_out:
        return y_l, None
    s_c = jnp.einsum('bqhd,bkhd->bhqk', q_c, k_c).astype(jnp.float32) * scale
    p_c = jax.nn.softmax(s_c, axis=-1).astype(v_c.dtype)
    y_c = jnp.einsum('bhqk,bkhd->bqhd', p_c, v_c).reshape(b, n_ctx, d) @ w_o
    return y_l, y_c


def depthwise_conv(u, w, bias):
    out = lax.conv_general_dilated(u, w[:, None, :], window_strides=(1,), padding=(CONV_PAD,),
                                   dimension_numbers=('NWC', 'WIO', 'NWC'), feature_group_count=u.shape[-1])
    return out + bias


def _linear_combine(e1, e2):
    a1, b1 = e1
    a2, b2 = e2
    return a1 * a2, a2 * b1 + b2


def rg_lru_scan(u, w_a, b_a, w_i, b_i, lam, h0, reverse):
    b, n, dr = u.shape
    ub = u.reshape(b, n, RG_BLOCKS, RG_BLOCK_W)
    r = jax.nn.sigmoid(jnp.einsum('blgc,gcd->blgd', ub, w_a) + b_a).reshape(b, n, dr)
    i = jax.nn.sigmoid(jnp.einsum('blgc,gcd->blgd', ub, w_i) + b_i).reshape(b, n, dr)
    log_a = -RG_C * r.astype(jnp.float32) * jax.nn.softplus(-lam.astype(jnp.float32))
    a = jnp.exp(log_a)
    xin = jnp.sqrt(-jnp.expm1(2.0 * log_a)) * (i * u).astype(jnp.float32)
    first = n - 1 if reverse else 0
    xin = xin.at[:, first].add(a[:, first] * h0)
    _, h = lax.associative_scan(_linear_combine, (a, xin), reverse=reverse, axis=1)
    return h


def rglru_mixer(hl, hc, w_y, b_y, w_x, b_x, conv_w, conv_b, w_a, b_a, w_i, b_i, lam, w_out, b_out, ctx_out):
    b = hl.shape[0]
    u_l = depthwise_conv(hl @ w_x + b_x, conv_w, conv_b)
    u_c = depthwise_conv(hc @ w_x + b_x, conv_w, conv_b)
    states_l, states_c = [], []
    for dirn, reverse in enumerate((False, True)):
        h_c = rg_lru_scan(u_c, w_a[dirn], b_a[dirn], w_i[dirn], b_i[dirn], lam[dirn],
                          jnp.zeros((b, D_RNN), jnp.float32), reverse)
        h_end = h_c[:, 0] if reverse else h_c[:, -1]
        states_l.append(rg_lru_scan(u_l, w_a[dirn], b_a[dirn], w_i[dirn], b_i[dirn], lam[dirn], h_end, reverse))
        states_c.append(h_c)
    h_l = (states_l[0] + states_l[1]).astype(hl.dtype)
    y_l = (h_l * jax.nn.gelu(hl @ w_y + b_y)) @ w_out + b_out
    if not ctx_out:
        return y_l, None
    h_cs = (states_c[0] + states_c[1]).astype(hc.dtype)
    y_c = (h_cs * jax.nn.gelu(hc @ w_y + b_y)) @ w_out + b_out
    return y_l, y_c


def moe(h, router_w, router_b, w_gu, b_gu, w_dn, b_dn):
    bsz, n, d = h.shape
    xt = h.reshape(-1, d)
    t = xt.shape[0]
    logits = (xt @ router_w + router_b).astype(jnp.float32)
    top_v, top_e = lax.top_k(logits, TOP_K)
    gates = jax.nn.softmax(top_v, axis=-1).astype(h.dtype)
    tk = t * TOP_K
    flat_e = top_e.reshape(-1).astype(jnp.int32)
    flat_tok = jnp.arange(tk, dtype=jnp.int32) // TOP_K
    order = jnp.argsort(flat_e)
    se, stok, sg = flat_e[order], flat_tok[order], gates.reshape(-1)[order]
    counts = jnp.bincount(flat_e, length=N_EXPERTS)
    starts = jnp.cumsum(counts) - counts
    padded = (counts + MOE_BLOCK - 1) // MOE_BLOCK * MOE_BLOCK
    pends = jnp.cumsum(padded)
    pstarts = pends - padded
    dest = pstarts[se] + (jnp.arange(tk, dtype=jnp.int32) - starts[se])
    n_blk = -(-tk // MOE_BLOCK) + N_EXPERTS
    p_rows = n_blk * MOE_BLOCK
    tok_buf = jnp.full((p_rows,), t, jnp.int32).at[dest].set(stok)
    g_buf = jnp.zeros((p_rows,), h.dtype).at[dest].set(sg)
    blk_e = jnp.minimum(jnp.searchsorted(pends, jnp.arange(n_blk) * MOE_BLOCK, side='right'), N_EXPERTS - 1)
    x_pad = jnp.concatenate([xt, jnp.zeros((1, d), xt.dtype)], axis=0)

    def expert_block(args):
        tok, g, e = args
        xb = jnp.take(x_pad, tok, axis=0)
        gu = xb @ w_gu[e] + b_gu[e]
        glu = jnp.minimum(gu[:, 0::2], SWIGLU_LIMIT)
        lin = jnp.clip(gu[:, 1::2], -SWIGLU_LIMIT, SWIGLU_LIMIT)
        act = glu * jax.nn.sigmoid(SWIGLU_ALPHA * glu) * (lin + 1)
        return (act @ w_dn[e] + b_dn[e]) * g[:, None]

    yb = lax.map(expert_block, (tok_buf.reshape(n_blk, MOE_BLOCK), g_buf.reshape(n_blk, MOE_BLOCK), blk_e))
    out = jax.ops.segment_sum(yb.reshape(p_rows, d), tok_buf, num_segments=t + 1)[:t]
    return out.reshape(bsz, n, d)


def setup_inputs(seed: int = 0) -> dict:
    key = jax.random.key(seed)
    keys = iter(jax.random.split(key, 40))

    def nrm(shape, scale):
        return jax.random.normal(next(keys), shape, jnp.float32) * scale

    d = D_MODEL
    n_na = (DEPTH + 1) // 2
    n_rg = DEPTH // 2
    u = jax.random.uniform(next(keys), (n_rg, 2, D_RNN), jnp.float32, 0.9, 0.999)
    s = u ** (1.0 / RG_C)
    rg_lam = jnp.log(s) - jnp.log1p(-s)
    return {
        "x": nrm((BATCH, SEQ, d), 1.0),
        "c": nrm((BATCH, d), 1.0),
        "ctx": nrm((BATCH, CTX_LEN, d), 1.0),
        "c_ctx": nrm((d,), 1.0),
        "mod_w": nrm((DEPTH, d, 6 * d), 0.5 * d ** -0.5),
        "mod_b": nrm((DEPTH, 6 * d), 0.02),
        "norm1_g": 1.0 + nrm((DEPTH, d), 0.05),
        "norm2_g": 1.0 + nrm((DEPTH, d), 0.05),
        "final_g": 1.0 + nrm((d,), 0.05),
        "na_w_qkv": nrm((n_na, d, 3 * d), d ** -0.5),
        "na_w_o": nrm((n_na, d, d), d ** -0.5),
        "na_rpb": nrm((n_na, NA_HEADS, 2 * WIN_ROWS - 1, 2 * WIN_COLS - 1), 0.5),
        "rg_w_y": nrm((n_rg, d, D_RNN), d ** -0.5),
        "rg_b_y": nrm((n_rg, D_RNN), 0.02),
        "rg_w_x": nrm((n_rg, d, D_RNN), d ** -0.5),
        "rg_b_x": nrm((n_rg, D_RNN), 0.02),
        "rg_conv_w": nrm((n_rg, CONV_W, D_RNN), CONV_W ** -0.5),
        "rg_conv_b": nrm((n_rg, D_RNN), 0.02),
        "rg_w_a": nrm((n_rg, 2, RG_BLOCKS, RG_BLOCK_W, RG_BLOCK_W), RG_BLOCK_W ** -0.5),
        "rg_b_a": nrm((n_rg, 2, RG_BLOCKS, RG_BLOCK_W), 0.02),
        "rg_w_i": nrm((n_rg, 2, RG_BLOCKS, RG_BLOCK_W, RG_BLOCK_W), RG_BLOCK_W ** -0.5),
        "rg_b_i": nrm((n_rg, 2, RG_BLOCKS, RG_BLOCK_W), 0.02),
        "rg_lam": rg_lam,
        "rg_w_out": nrm((n_rg, D_RNN, d), D_RNN ** -0.5),
        "rg_b_out": nrm((n_rg, d), 0.02),
        "moe_router_w": nrm((DEPTH, d, N_EXPERTS), d ** -0.5),
        "moe_router_b": nrm((DEPTH, N_EXPERTS), 0.01),
        "moe_w_gu": nrm((DEPTH, N_EXPERTS, d, 2 * D_EXPERT), d ** -0.5),
        "moe_b_gu": nrm((DEPTH, N_EXPERTS, 2 * D_EXPERT), 0.02),
        "moe_w_dn": nrm((DEPTH, N_EXPERTS, D_EXPERT, d), D_EXPERT ** -0.5),
        "moe_b_dn": nrm((DEPTH, N_EXPERTS, d), 0.02),
    }


def reference(x, c, ctx, c_ctx, mod_w, mod_b, norm1_g, norm2_g, final_g, na_w_qkv, na_w_o, na_rpb,
              rg_w_y, rg_b_y, rg_w_x, rg_b_x, rg_conv_w, rg_conv_b, rg_w_a, rg_b_a, rg_w_i, rg_b_i, rg_lam,
              rg_w_out, rg_b_out, moe_router_w, moe_router_b, moe_w_gu, moe_b_gu, moe_w_dn, moe_b_dn):
    n_ctx = ctx.shape[1]
    xl, xc = x, ctx
    for i in range(DEPTH):
        last = i == DEPTH - 1
        ml = jnp.split(jax.nn.silu(c) @ mod_w[i] + mod_b[i], 6, axis=-1)
        mc = jnp.split(jax.nn.silu(c_ctx)[None, :] @ mod_w[i] + mod_b[i], 6, axis=-1)
        hl = modulate(rmsnorm(xl, norm1_g[i]), ml[0], ml[1])
        hc = modulate(rmsnorm(xc, norm1_g[i]), mc[0], mc[1])
        j = i // N_MIXERS
        if i % N_MIXERS == 0:
            yl, yc = na_mixer(hl, hc, na_w_qkv[j], na_w_o[j], na_rpb[j], not last)
        else:
            yl, yc = rglru_mixer(hl, hc, rg_w_y[j], rg_b_y[j], rg_w_x[j], rg_b_x[j], rg_conv_w[j], rg_conv_b[j],
                                 rg_w_a[j], rg_b_a[j], rg_w_i[j], rg_b_i[j], rg_lam[j], rg_w_out[j], rg_b_out[j],
                                 not last)
        xl = xl + ml[2][:, None, :] * yl
        hl2 = modulate(rmsnorm(xl, norm2_g[i]), ml[3], ml[4])
        if last:
            xl = xl + ml[5][:, None, :] * moe(hl2, moe_router_w[i], moe_router_b[i], moe_w_gu[i], moe_b_gu[i],
                                              moe_w_dn[i], moe_b_dn[i])
        else:
            xc = xc + mc[2][:, None, :] * yc
            hc2 = modulate(rmsnorm(xc, norm2_g[i]), mc[3], mc[4])
            y2 = moe(jnp.concatenate([hc2, hl2], axis=1), moe_router_w[i], moe_router_b[i], moe_w_gu[i],
                     moe_b_gu[i], moe_w_dn[i], moe_b_dn[i])
            xc = xc + mc[5][:, None, :] * y2[:, :n_ctx]
            xl = xl + ml[5][:, None, :] * y2[:, n_ctx:]
    return rmsnorm(xl, final_g)
```

```python
import functools

import jax
import jax.numpy as jnp
import numpy as np
from jax import lax
from jax.experimental import pallas as pl
from jax.experimental.pallas import tpu as pltpu

F32 = jnp.float32
BF16 = jnp.bfloat16

LANES = 128
GRID_W = 64
TOP_K = 4
RG_C = 8.0
CONV_LEFT = 2
SWIGLU_ALPHA = 1.702
SWIGLU_LIMIT = 7.0
RMS_EPS = 1e-6
MOD_ROWS = 6
TAB_ROWS = 16
VMEM_LIMIT = 56 * 1024 * 1024


def _cparams(sem):
    return pltpu.CompilerParams(dimension_semantics=sem, vmem_limit_bytes=VMEM_LIMIT)


def _pick(n, pref, mult=8):
    for t in range(min(pref, n), 0, -1):
        if n % t == 0 and t % mult == 0:
            return t
    return n


def _dot(a, b):
    return jnp.dot(a, b, preferred_element_type=F32)


def _dot_nt(a, b):
    return lax.dot_general(a, b, (((1,), (1,)), ((), ())), preferred_element_type=F32)


def _split_bf16(x):
    hi = x.astype(BF16)
    lo = (x - hi.astype(F32)).astype(BF16)
    return hi, lo


def _dot3(a, w):
    a_hi, a_lo = _split_bf16(a)
    w_hi, w_lo = _split_bf16(w)
    return _dot(a_hi, w_hi) + _dot(a_lo, w_hi) + _dot(a_hi, w_lo)


def _norm_mod(x, g, shift, scale):
    y = x * lax.rsqrt(jnp.mean(x * x, axis=-1, keepdims=True) + RMS_EPS)
    return (y * g) * (1.0 + scale) + shift


def _tab_row(tab_ref, is_ctx, k):
    base = jnp.where(is_ctx, 0, MOD_ROWS)
    return tab_ref[0, pl.ds(base + k, 1), :]


def _mod_kernel(a_ref, w_ref, b_ref, o_ref):
    a = a_ref[...]
    a = a * jax.nn.sigmoid(a)
    o_ref[0] = _dot3(a, w_ref[0]) + b_ref[0]


def _modulation(cc, mod_w, mod_b):
    depth, d, n = mod_w.shape
    r = cc.shape[0]
    tn = _pick(n, 1024, LANES)
    return pl.pallas_call(
        _mod_kernel,
        grid=(depth, n // tn),
        in_specs=[
            pl.BlockSpec((r, d), lambda i, j: (0, 0)),
            pl.BlockSpec((1, d, tn), lambda i, j: (i, 0, j)),
            pl.BlockSpec((1, 1, tn), lambda i, j: (i, 0, j)),
        ],
        out_specs=pl.BlockSpec((1, r, tn), lambda i, j: (i, 0, j)),
        out_shape=jax.ShapeDtypeStruct((depth, r, n), F32),
        compiler_params=_cparams(("parallel", "parallel")),
        name="modulation",
    )(cc, mod_w, mod_b.reshape(depth, 1, n))


def _prenorm_kernel(x_ref, tab_ref, g_ref, o_ref, *, ctx_tiles):
    is_ctx = pl.program_id(1) < ctx_tiles
    h = _norm_mod(x_ref[0], g_ref[...], _tab_row(tab_ref, is_ctx, 0), _tab_row(tab_ref, is_ctx, 1))
    o_ref[0] = h.astype(BF16)


def _prenorm(x, tab, g, n_ctx):
    b, l, d = x.shape
    tm = n_ctx
    return pl.pallas_call(
        functools.partial(_prenorm_kernel, ctx_tiles=n_ctx // tm),
        grid=(b, l // tm),
        in_specs=[
            pl.BlockSpec((1, tm, d), lambda i, j: (i, j, 0)),
            pl.BlockSpec((1, TAB_ROWS, d), lambda i, j: (i, 0, 0)),
            pl.BlockSpec((1, d), lambda i, j: (0, 0)),
        ],
        out_specs=pl.BlockSpec((1, tm, d), lambda i, j: (i, j, 0)),
        out_shape=jax.ShapeDtypeStruct((b, l, d), BF16),
        compiler_params=_cparams(("parallel", "parallel")),
        name="prenorm",
    )(x, tab, g.reshape(1, d))


def _gelu_tanh(x):
    return 0.5 * x * (1.0 + jnp.tanh(np.sqrt(2.0 / np.pi) * (x + 0.044715 * (x * x * x))))


def _matmul_kernel(a_ref, w_ref, b_ref, o_ref, *, act):
    y = _dot(a_ref[...], w_ref[...]) + b_ref[...]
    if act == "gelu":
        y = _gelu_tanh(y)
    o_ref[...] = y.astype(o_ref.dtype)


def _matmul(a, w, bias, out_dtype, act=None):
    m, k = a.shape
    n = w.shape[1]
    tm = _pick(m, 1024)
    tn = _pick(n, 512, LANES)
    return pl.pallas_call(
        functools.partial(_matmul_kernel, act=act),
        grid=(m // tm, n // tn),
        in_specs=[
            pl.BlockSpec((tm, k), lambda i, j: (i, 0)),
            pl.BlockSpec((k, tn), lambda i, j: (0, j)),
            pl.BlockSpec((1, tn), lambda i, j: (0, j)),
        ],
        out_specs=pl.BlockSpec((tm, tn), lambda i, j: (i, j)),
        out_shape=jax.ShapeDtypeStruct((m, n), out_dtype),
        compiler_params=_cparams(("parallel", "parallel")),
        name="matmul_" + (act or "linear"),
    )(a, w, bias.reshape(1, n))


def _softmax_parts(parts):
    m = parts[0].max(axis=-1, keepdims=True)
    for s in parts[1:]:
        m = jnp.maximum(m, s.max(axis=-1, keepdims=True))
    ps = [jnp.exp(s - m) for s in parts]
    den = ps[0].sum(axis=-1, keepdims=True)
    for p in ps[1:]:
        den = den + p.sum(axis=-1, keepdims=True)
    return ps, den


def _attn_kernel(q_ref, k_ref, v_ref, bias_ref, o_ref, *, n_ctx, rows, kr, scale):
    kc = k_ref[0, 0:n_ctx, :]
    vc = v_ref[0, 0:n_ctx, :]
    (p,), den = _softmax_parts([_dot_nt(q_ref[0, 0:n_ctx, :], kc) * scale])
    o_ref[0, 0:n_ctx, :] = (_dot(p.astype(BF16), vc) / den).astype(BF16)

    def row_block(r, carry):
        rs = jnp.clip(r - kr // 2, 0, rows - kr)
        q0 = pl.multiple_of(n_ctx + r * GRID_W, GRID_W)
        k0 = pl.multiple_of(n_ctx + rs * GRID_W, GRID_W)
        q = q_ref[0, pl.ds(q0, GRID_W), :]
        kl = k_ref[0, pl.ds(k0, kr * GRID_W), :]
        vl = v_ref[0, pl.ds(k0, kr * GRID_W), :]
        s_lat = _dot_nt(q, kl) * scale + bias_ref[0, r - rs]
        s_ctx = _dot_nt(q, kc) * scale
        (p_lat, p_ctx), den = _softmax_parts([s_lat, s_ctx])
        o = _dot(p_lat.astype(BF16), vl) + _dot(p_ctx.astype(BF16), vc)
        o_ref[0, pl.ds(q0, GRID_W), :] = (o / den).astype(BF16)
        return carry

    lax.fori_loop(0, rows, row_block, 0)


def _na_bias_table(rpb, rows, kr):
    h, n_dr, n_dc = rpb.shape
    win_rows, win_cols = (n_dr + 1) // 2, (n_dc + 1) // 2
    col = np.arange(GRID_W)
    col_start = np.clip(col - win_cols // 2, 0, GRID_W - win_cols)
    col_mask = (col[None, :] >= col_start[:, None]) & (col[None, :] < col_start[:, None] + win_cols)
    dc_idx = np.clip(col[None, :] - col[:, None], 1 - win_cols, win_cols - 1) + win_cols - 1
    dr_idx = np.arange(kr)[None, :] - np.arange(kr)[:, None] + win_rows - 1
    t = rpb[:, dr_idx][:, :, :, dc_idx]
    t = jnp.where(col_mask[None, None, None], t.astype(F32), -jnp.inf)
    return t.transpose(0, 1, 3, 2, 4).reshape(h, kr, GRID_W, kr * GRID_W)


def _attention(qkv, bias, n_ctx, heads):
    b, l, d3 = qkv.shape
    d = d3 // 3
    dh = d // heads
    rows = (l - n_ctx) // GRID_W
    kr = bias.shape[1]
    kern = functools.partial(_attn_kernel, n_ctx=n_ctx, rows=rows, kr=kr, scale=dh ** -0.5)
    return pl.pallas_call(
        kern,
        grid=(heads, b),
        in_specs=[
            pl.BlockSpec((1, l, dh), lambda h, i: (i, 0, h)),
            pl.BlockSpec((1, l, dh), lambda h, i: (i, 0, heads + h)),
            pl.BlockSpec((1, l, dh), lambda h, i: (i, 0, 2 * heads + h)),
            pl.BlockSpec((1, kr, GRID_W, kr * GRID_W), lambda h, i: (h, 0, 0, 0)),
        ],
        out_specs=pl.BlockSpec((1, l, dh), lambda h, i: (i, 0, h)),
        out_shape=jax.ShapeDtypeStruct((b, l, d), BF16),
        compiler_params=_cparams(("parallel", "parallel")),
        name="na_attention",
    )(qkv, qkv, qkv, bias)


def _proj_kernel(a_ref, w_ref, b_ref, x_ref, tab_ref, g_ref, rw_ref, rb_ref,
                 x1_ref, hrow_ref, lg_ref, *, ctx_tiles, tile_off):
    is_ctx = pl.program_id(1) + tile_off < ctx_tiles
    y = _dot(a_ref[0], w_ref[...]) + b_ref[...]
    x1 = x_ref[0] + _tab_row(tab_ref, is_ctx, 2) * y
    x1_ref[0] = x1
    h2 = _norm_mod(x1, g_ref[...], _tab_row(tab_ref, is_ctx, 3), _tab_row(tab_ref, is_ctx, 4))
    lg_ref[...] = _dot3(h2, rw_ref[...]) + rb_ref[...]
    tm, d = h2.shape
    s_rows = d // LANES
    for s in range(s_rows):
        hrow_ref[pl.ds(s, tm, stride=s_rows), :] = h2[:, s * LANES:(s + 1) * LANES]


def _proj(a, w, bias, x, tab, g2, rw, rb, n_ctx, latent_only):
    b, l, d = x.shape
    tm = n_ctx
    off = n_ctx // tm if latent_only else 0
    nt = l // tm - off
    s_rows = d // LANES
    t = b * nt * tm
    kern = functools.partial(_proj_kernel, ctx_tiles=n_ctx // tm, tile_off=off)
    return pl.pallas_call(
        kern,
        grid=(b, nt),
        in_specs=[
            pl.BlockSpec((1, tm, d), lambda i, j: (i, j + off, 0)),
            pl.BlockSpec((d, d), lambda i, j: (0, 0)),
            pl.BlockSpec((1, d), lambda i, j: (0, 0)),
            pl.BlockSpec((1, tm, d), lambda i, j: (i, j + off, 0)),
            pl.BlockSpec((1, TAB_ROWS, d), lambda i, j: (i, 0, 0)),
            pl.BlockSpec((1, d), lambda i, j: (0, 0)),
            pl.BlockSpec((d, LANES), lambda i, j: (0, 0)),
            pl.BlockSpec((1, LANES), lambda i, j: (0, 0)),
        ],
        out_specs=[
            pl.BlockSpec((1, tm, d), lambda i, j: (i, j, 0)),
            pl.BlockSpec((tm * s_rows, LANES), lambda i, j: (i * nt + j, 0)),
            pl.BlockSpec((tm, LANES), lambda i, j: (i * nt + j, 0)),
        ],
        out_shape=[
            jax.ShapeDtypeStruct((b, nt * tm, d), F32),
            jax.ShapeDtypeStruct((t * s_rows, LANES), F32),
            jax.ShapeDtypeStruct((t, LANES), F32),
        ],
        compiler_params=_cparams(("parallel", "parallel")),
        name="mixer_proj",
    )(a, w, bias.reshape(1, d), x, tab, g2.reshape(1, d), rw, rb)


def _route_kernel(lg_ref, e_ref, g_ref, r_ref, cnt_ref, carry_ref, *, n_exp):
    @pl.when(pl.program_id(0) == 0)
    def _():
        carry_ref[...] = jnp.zeros_like(carry_ref)

    lg = lg_ref[...]
    tm = lg.shape[0]
    lane = lax.broadcasted_iota(jnp.int32, lg.shape, 1).astype(F32)
    cur = jnp.where(lane < n_exp, lg, -jnp.inf)
    multi = jnp.zeros(lg.shape, F32)
    vals, idxs = [], []
    for _ in range(TOP_K):
        m = cur.max(axis=-1, keepdims=True)
        idx = jnp.where(cur == m, lane, float(LANES)).min(axis=-1, keepdims=True)
        sel = lane == idx
        multi = jnp.where(sel, 1.0, multi)
        cur = jnp.where(sel, -jnp.inf, cur)
        vals.append(m)
        idxs.append(idx)
    exps = [jnp.exp(v - vals[0]) for v in vals]
    den = exps[0]
    for e in exps[1:]:
        den = den + e
    tri = (lax.broadcasted_iota(jnp.int32, (tm, tm), 0) > lax.broadcasted_iota(jnp.int32, (tm, tm), 1))
    pref = _dot(jnp.where(tri, 1.0, 0.0).astype(BF16), multi.astype(BF16))
    tot = carry_ref[...] + pref
    e_out = jnp.zeros(lg.shape, F32)
    g_out = jnp.zeros(lg.shape, F32)
    r_out = jnp.zeros(lg.shape, F32)
    for k in range(TOP_K):
        rank_k = jnp.where(lane == idxs[k], tot, 0.0).sum(axis=-1, keepdims=True)
        e_out = jnp.where(lane == k, idxs[k], e_out)
        g_out = jnp.where(lane == k, exps[k] / den, g_out)
        r_out = jnp.where(lane == k, rank_k, r_out)
    e_ref[...] = e_out.astype(jnp.int32)
    g_ref[...] = g_out
    r_ref[...] = r_out.astype(jnp.int32)
    carry_ref[...] = carry_ref[...] + multi.sum(axis=0, keepdims=True)
    cnt_ref[...] = carry_ref[...]


def _route(logits, n_exp):
    t = logits.shape[0]
    tm = _pick(t, 256)
    spec = pl.BlockSpec((tm, LANES), lambda i: (i, 0))
    return pl.pallas_call(
        functools.partial(_route_kernel, n_exp=n_exp),
        grid=(t // tm,),
        in_specs=[spec],
        out_specs=[spec, spec, spec, pl.BlockSpec((1, LANES), lambda i: (0, 0))],
        out_shape=[
            jax.ShapeDtypeStruct((t, LANES), jnp.int32),
            jax.ShapeDtypeStruct((t, LANES), F32),
            jax.ShapeDtypeStruct((t, LANES), jnp.int32),
            jax.ShapeDtypeStruct((1, LANES), F32),
        ],
        scratch_shapes=[pltpu.VMEM((1, LANES), F32)],
        compiler_params=_cparams(("arbitrary",)),
        name="router",
    )(logits)


def _dispatch_kernel(dest_ref, h_ref, xs_ref, sem, *, td, s_rows):
    def issue(t, carry):
        src = h_ref.at[pl.ds(pl.multiple_of(t * s_rows, s_rows), s_rows), :]
        for k in range(TOP_K):
            d = dest_ref[0, 0, t * TOP_K + k]
            dst = xs_ref.at[pl.ds(pl.multiple_of(d * s_rows, s_rows), s_rows), :]
            pltpu.make_async_copy(src, dst, sem).start()
        return carry

    lax.fori_loop(0, td, issue, 0)
    for _ in range(TOP_K):
        pltpu.make_async_copy(h_ref, xs_ref.at[pl.ds(0, td * s_rows), :], sem).wait()


def _dispatch(hrow, dest, s_rows):
    t = dest.shape[0] // TOP_K
    td = _pick(t, 256)
    nt = t // td
    return pl.pallas_call(
        functools.partial(_dispatch_kernel, td=td, s_rows=s_rows),
        grid=(nt,),
        in_specs=[
            pl.BlockSpec((1, 1, td * TOP_K), lambda i: (i, 0, 0), memory_space=pltpu.SMEM),
            pl.BlockSpec((td * s_rows, LANES), lambda i: (i, 0)),
        ],
        out_specs=pl.BlockSpec(memory_space=pl.ANY),
        out_shape=jax.ShapeDtypeStruct((t * TOP_K * s_rows, LANES), F32),
        scratch_shapes=[pltpu.SemaphoreType.DMA(())],
        compiler_params=pltpu.CompilerParams(dimension_semantics=("arbitrary",), vmem_limit_bytes=VMEM_LIMIT,
                                             has_side_effects=True),
        name="moe_dispatch",
    )(dest.reshape(nt, 1, td * TOP_K), hrow)


def _gmm_kernel(it_tile, it_e, it_lo, it_hi, it_first, it_valid,
                xs_ref, wg_ref, wl_ref, bg_ref, bl_ref, wd_ref, bd_ref, ys_ref, xb_ref, acc_ref,
                *, tg, s_rows, n_chunks):
    m = pl.program_id(0)
    c = pl.program_id(1)

    @pl.when(it_valid[m] == 1)
    def _():
        @pl.when(c == 0)
        def _():
            for s in range(s_rows):
                xb_ref[:, s * LANES:(s + 1) * LANES] = xs_ref[pl.ds(s, tg, stride=s_rows), :].astype(BF16)

        x = xb_ref[...]
        glu = jnp.minimum(_dot(x, wg_ref[0]) + bg_ref[0], SWIGLU_LIMIT)
        lin = jnp.clip(_dot(x, wl_ref[0]) + bl_ref[0], -SWIGLU_LIMIT, SWIGLU_LIMIT)
        act = glu * jax.nn.sigmoid(SWIGLU_ALPHA * glu) * (lin + 1.0)
        part = _dot(act.astype(BF16), wd_ref[0])

        @pl.when(c == 0)
        def _():
            acc_ref[...] = part

        @pl.when(c > 0)
        def _():
            acc_ref[...] += part

        @pl.when(c == n_chunks - 1)
        def _():
            y = acc_ref[...] + bd_ref[0]
            row = lax.broadcasted_iota(jnp.int32, (tg, 1), 0)
            mine = (row >= it_lo[m]) & (row < it_hi[m])

            @pl.when(it_first[m] == 1)
            def _():
                for s in range(s_rows):
                    ys_ref[pl.ds(s, tg, stride=s_rows), :] = jnp.where(mine, y[:, s * LANES:(s + 1) * LANES], 0.0)

            @pl.when(it_first[m] == 0)
            def _():
                for s in range(s_rows):
                    old = ys_ref[pl.ds(s, tg, stride=s_rows), :]
                    ys_ref[pl.ds(s, tg, stride=s_rows), :] = jnp.where(mine, y[:, s * LANES:(s + 1) * LANES], old)


def _gmm_items(counts, n_tiles, tg, max_items):
    n_exp = counts.shape[0]
    ends = jnp.cumsum(counts)
    starts = ends - counts
    tile0 = jnp.arange(n_tiles, dtype=jnp.int32) * tg
    e_lo = jnp.minimum(jnp.searchsorted(ends, tile0, side="right"), n_exp - 1).astype(jnp.int32)
    e_hi = jnp.minimum(jnp.searchsorted(ends, tile0 + tg - 1, side="right"), n_exp - 1).astype(jnp.int32)
    n_items = e_hi - e_lo + 1
    item_end = jnp.cumsum(n_items)
    item_start = item_end - n_items
    total = item_end[-1]
    m = jnp.arange(max_items, dtype=jnp.int32)
    valid = m < total
    tile = jnp.minimum(jnp.searchsorted(item_end, m, side="right"), n_tiles - 1).astype(jnp.int32)
    e = jnp.where(valid, e_lo[tile] + (m - item_start[tile]), e_hi[n_tiles - 1]).astype(jnp.int32)
    lo = jnp.clip(starts[e] - tile * tg, 0, tg).astype(jnp.int32)
    hi = jnp.clip(ends[e] - tile * tg, 0, tg).astype(jnp.int32)
    first = (m == item_start[tile]).astype(jnp.int32)
    return tile, e, lo, hi, first, valid.astype(jnp.int32)


def _gmm(xs, counts, wg, wl, bg, bl, wd, bd, s_rows):
    n_exp, d, f = wg.shape
    p = xs.shape[0] // s_rows
    tg = _pick(p, 512)
    fk = _pick(f, 512, LANES)
    n_tiles, n_chunks = p // tg, f // fk
    max_items = n_tiles + n_exp - 1
    items = _gmm_items(counts, n_tiles, tg, max_items)

    def chunk(c, valid, m):
        return jnp.where(valid[m] == 1, c, n_chunks - 1)

    grid_spec = pltpu.PrefetchScalarGridSpec(
        num_scalar_prefetch=6,
        grid=(max_items, n_chunks),
        in_specs=[
            pl.BlockSpec((tg * s_rows, LANES), lambda m, c, t, e, lo, hi, fi, va: (t[m], 0)),
            pl.BlockSpec((1, d, fk), lambda m, c, t, e, lo, hi, fi, va: (e[m], 0, chunk(c, va, m))),
            pl.BlockSpec((1, d, fk), lambda m, c, t, e, lo, hi, fi, va: (e[m], 0, chunk(c, va, m))),
            pl.BlockSpec((1, 1, fk), lambda m, c, t, e, lo, hi, fi, va: (e[m], 0, chunk(c, va, m))),
            pl.BlockSpec((1, 1, fk), lambda m, c, t, e, lo, hi, fi, va: (e[m], 0, chunk(c, va, m))),
            pl.BlockSpec((1, fk, d), lambda m, c, t, e, lo, hi, fi, va: (e[m], chunk(c, va, m), 0)),
            pl.BlockSpec((1, 1, d), lambda m, c, t, e, lo, hi, fi, va: (e[m], 0, 0)),
        ],
        out_specs=pl.BlockSpec((tg * s_rows, LANES), lambda m, c, t, e, lo, hi, fi, va: (t[m], 0)),
        scratch_shapes=[pltpu.VMEM((tg, d), BF16), pltpu.VMEM((tg, d), F32)],
    )
    return pl.pallas_call(
        functools.partial(_gmm_kernel, tg=tg, s_rows=s_rows, n_chunks=n_chunks),
        grid_spec=grid_spec,
        out_shape=jax.ShapeDtypeStruct(xs.shape, F32),
        compiler_params=_cparams(("arbitrary", "arbitrary")),
        name="moe_experts",
    )(*items, xs, wg, wl, bg.reshape(n_exp, 1, f), bl.reshape(n_exp, 1, f), wd, bd.reshape(n_exp, 1, d))


def _combine_kernel(dest_ref, dest_next_ref, ys_ref, gate_ref, x_ref, tab_ref, tabn_ref, g_ref,
                    x2_ref, h_ref, buf_ref, moe_ref, sem, *, tc, s_rows, ctx_tiles, n_steps, mode):
    step = pl.program_id(0) * pl.num_programs(1) + pl.program_id(1)
    slot = step % 2

    def gather(dref, slot_idx):
        def issue(t, carry):
            for k in range(TOP_K):
                d = dref[0, 0, t * TOP_K + k]
                src = ys_ref.at[pl.ds(pl.multiple_of(d * s_rows, s_rows), s_rows), :]
                dst = buf_ref.at[slot_idx, pl.ds(pl.multiple_of((k * tc + t) * s_rows, s_rows), s_rows), :]
                pltpu.make_async_copy(src, dst, sem.at[slot_idx]).start()
            return carry

        lax.fori_loop(0, tc, issue, 0)

    @pl.when(step == 0)
    def _():
        gather(dest_ref, 0)

    @pl.when(step + 1 < n_steps)
    def _():
        gather(dest_next_ref, 1 - slot)

    pltpu.make_async_copy(buf_ref.at[slot], buf_ref.at[slot], sem.at[slot]).wait()

    gates = gate_ref[...]
    for s in range(s_rows):
        acc = None
        for k in range(TOP_K):
            rows = buf_ref[slot, pl.ds(k * tc * s_rows + s, tc, stride=s_rows), :]
            term = rows * gates[:, k:k + 1]
            acc = term if acc is None else acc + term
        moe_ref[:, s * LANES:(s + 1) * LANES] = acc

    is_ctx = pl.program_id(1) < ctx_tiles
    x2 = x_ref[0] + _tab_row(tab_ref, is_ctx, 5) * moe_ref[...]
    x2_ref[0] = x2
    if mode == "next":
        h = _norm_mod(x2, g_ref[...], _tab_row(tabn_ref, is_ctx, 0), _tab_row(tabn_ref, is_ctx, 1))
        h_ref[0] = h.astype(h_ref.dtype)
    else:
        y = x2 * lax.rsqrt(jnp.mean(x2 * x2, axis=-1, keepdims=True) + RMS_EPS)
        h_ref[0] = (y * g_ref[...]).astype(h_ref.dtype)


def _combine(ys, dest, gates, x1, tab, tab_next, g_next, n_ctx_rows, s_rows, mode):
    b, l, d = x1.shape
    tc = _pick(min(l, 128) if n_ctx_rows == 0 else n_ctx_rows, 128)
    nt = l // tc
    n_steps = b * nt
    dest3 = dest.reshape(n_steps, 1, tc * TOP_K)
    kern = functools.partial(_combine_kernel, tc=tc, s_rows=s_rows, ctx_tiles=n_ctx_rows // tc,
                             n_steps=n_steps, mode=mode)
    out_dtype = BF16 if mode == "next" else F32
    return pl.pallas_call(
        kern,
        grid=(b, nt),
        in_specs=[
            pl.BlockSpec((1, 1, tc * TOP_K), lambda i, j: (i * nt + j, 0, 0), memory_space=pltpu.SMEM),
            pl.BlockSpec((1, 1, tc * TOP_K), lambda i, j: (jnp.minimum(i * nt + j + 1, n_steps - 1), 0, 0),
                         memory_space=pltpu.SMEM),
            pl.BlockSpec(memory_space=pl.ANY),
            pl.BlockSpec((tc, LANES), lambda i, j: (i * nt + j, 0)),
            pl.BlockSpec((1, tc, d), lambda i, j: (i, j, 0)),
            pl.BlockSpec((1, TAB_ROWS, d), lambda i, j: (i, 0, 0)),
            pl.BlockSpec((1, TAB_ROWS, d), lambda i, j: (i, 0, 0)),
            pl.BlockSpec((1, d), lambda i, j: (0, 0)),
        ],
        out_specs=[
            pl.BlockSpec((1, tc, d), lambda i, j: (i, j, 0)),
            pl.BlockSpec((1, tc, d), lambda i, j: (i, j, 0)),
        ],
        out_shape=[
            jax.ShapeDtypeStruct((b, l, d), F32),
            jax.ShapeDtypeStruct((b, l, d), out_dtype),
        ],
        scratch_shapes=[
            pltpu.VMEM((2, tc * TOP_K * s_rows, LANES), F32),
            pltpu.VMEM((tc, d), F32),
            pltpu.SemaphoreType.DMA((2,)),
        ],
        compiler_params=_cparams(("arbitrary", "arbitrary")),
        name="moe_combine_" + mode,
    )(dest3, dest3, ys, gates, x1, tab, tab_next, g_next.reshape(1, d))


def _moe(hrow, logits, x1, tab, tab_next, g_next, w, n_ctx_rows, mode):
    n_exp = w["wg"].shape[0]
    s_rows = x1.shape[-1] // LANES
    e_pad, g_pad, r_pad, cnt = _route(logits, n_exp)
    top_e, rank = e_pad[:, :TOP_K], r_pad[:, :TOP_K]
    counts = cnt[0, :n_exp].astype(jnp.int32)
    starts = jnp.cumsum(counts) - counts
    dest = (starts[top_e] + rank).reshape(-1)
    xs = _dispatch(hrow, dest, s_rows)
    ys = _gmm(xs, counts, w["wg"], w["wl"], w["bg"], w["bl"], w["wd"], w["bd"], s_rows)
    return _combine(ys, dest, g_pad, x1, tab, tab_next, g_next, n_ctx_rows, s_rows, mode)


def _conv_kernel(x_ref, w_ref, b_ref, o_ref, *, n_ctx):
    l = x_ref.shape[1]
    w = w_ref[...]
    for s0, n in ((0, n_ctx), (n_ctx, l - n_ctx)):
        x = x_ref[0, s0:s0 + n, :]
        row = lax.broadcasted_iota(jnp.int32, (n, 1), 0)
        acc = x * w[CONV_LEFT:CONV_LEFT + 1]
        for j in range(w.shape[0]):
            off = j - CONV_LEFT
            if off == 0:
                continue
            shifted = pltpu.roll(x, (-off) % n, 0)
            ok = (row + off >= 0) & (row + off < n)
            acc = acc + jnp.where(ok, shifted, 0.0) * w[j:j + 1]
        o_ref[0, s0:s0 + n, :] = acc + b_ref[...]


def _conv(xb, conv_w, conv_b, n_ctx):
    b, l, d = xb.shape
    dt = _pick(d, 256, LANES)
    return pl.pallas_call(
        functools.partial(_conv_kernel, n_ctx=n_ctx),
        grid=(b, d // dt),
        in_specs=[
            pl.BlockSpec((1, l, dt), lambda i, j: (i, 0, j)),
            pl.BlockSpec((conv_w.shape[0], dt), lambda i, j: (0, j)),
            pl.BlockSpec((1, dt), lambda i, j: (0, j)),
        ],
        out_specs=pl.BlockSpec((1, l, dt), lambda i, j: (i, 0, j)),
        out_shape=jax.ShapeDtypeStruct((b, l, d), F32),
        compiler_params=_cparams(("parallel", "parallel")),
        name="rg_conv",
    )(xb, conv_w, conv_b.reshape(1, d))


SCAN_PAD = 8
SCAN_UNROLL = 8


def _scan_kernel(*refs, nb, tc, nh, reverse):
    n_in = 8 if reverse else 6
    u_ref, wa_ref, ba_ref, wi_ref, bi_ref, lam_ref = refs[:6]
    o_ref = refs[n_in]
    scratch = refs[n_in + 1:]
    a_s, x_s, h_s = scratch[0:nh], scratch[nh:2 * nh], scratch[2 * nh:3 * nh]
    carry = scratch[3 * nh]
    ts = tc + SCAN_PAD

    @pl.when(pl.program_id(1) == 0)
    def _():
        carry[...] = jnp.zeros_like(carry)

    neg = -lam_ref[0]
    softplus = jnp.maximum(neg, 0.0) + jnp.log1p(jnp.exp(-jnp.abs(neg)))
    for bi in range(nb):
        u = u_ref[bi]
        ub = u.astype(BF16)
        r = jax.nn.sigmoid(_dot(ub, wa_ref[0, 0]) + ba_ref[0, 0])
        i = jax.nn.sigmoid(_dot(ub, wi_ref[0, 0]) + bi_ref[0, 0])
        log_a = (-RG_C) * r * softplus
        a = jnp.exp(log_a)
        xin = jnp.sqrt(1.0 - a * a) * (i * u)
        for p in range(nh):
            a_s[p][pl.ds(bi * ts, tc), :] = a[:, p * LANES:(p + 1) * LANES]
            x_s[p][pl.ds(bi * ts, tc), :] = xin[:, p * LANES:(p + 1) * LANES]

    def block(j, hs):
        hs = list(hs)
        for q in range(SCAN_UNROLL):
            t = j * SCAN_UNROLL + q
            if reverse:
                t = tc - 1 - t
            for p in range(nh):
                hs[p] = a_s[p][pl.ds(t, nb, stride=ts), :] * hs[p] + x_s[p][pl.ds(t, nb, stride=ts), :]
                h_s[p][pl.ds(t, nb, stride=ts), :] = hs[p]
        return tuple(hs)

    h0 = tuple(carry[:, p * LANES:(p + 1) * LANES] for p in range(nh))
    hs = lax.fori_loop(0, tc // SCAN_UNROLL, block, h0)
    for p in range(nh):
        carry[:, p * LANES:(p + 1) * LANES] = hs[p]
    for bi in range(nb):
        for p in range(nh):
            h = h_s[p][pl.ds(bi * ts, tc), :]
            cols = slice(p * LANES, (p + 1) * LANES)
            if reverse:
                hf_ref, gy_ref = refs[6], refs[7]
                o_ref[bi, :, cols] = ((hf_ref[bi, :, cols] + h) * gy_ref[bi, :, cols]).astype(o_ref.dtype)
            else:
                o_ref[bi, :, cols] = h


def _scan(u, w_a, b_a, w_i, b_i, lam, dirn, n_ctx, h_fwd=None, gy=None):
    b, l, d = u.shape
    n_blk, w = w_a.shape[1], w_a.shape[2]
    tc = _pick(n_ctx, 128)
    nt, nc = l // tc, n_ctx // tc
    reverse = dirn == 1

    def chunk(j):
        if not reverse:
            return j
        return jnp.where(j < nc, nc - 1 - j, nt - 1 - (j - nc))

    blk = pl.BlockSpec((b, tc, w), lambda g, j: (0, chunk(j), g))
    in_specs = [
        blk,
        pl.BlockSpec((1, 1, w, w), lambda g, j: (dirn, g, 0, 0)),
        pl.BlockSpec((1, 1, 1, w), lambda g, j: (dirn, g, 0, 0)),
        pl.BlockSpec((1, 1, w, w), lambda g, j: (dirn, g, 0, 0)),
        pl.BlockSpec((1, 1, 1, w), lambda g, j: (dirn, g, 0, 0)),
        pl.BlockSpec((1, 1, w), lambda g, j: (dirn, 0, g)),
    ]
    args = [u, w_a, b_a.reshape(2, n_blk, 1, w), w_i, b_i.reshape(2, n_blk, 1, w), lam.reshape(2, 1, d)]
    if reverse:
        in_specs += [blk, blk]
        args += [h_fwd, gy]
    ts = tc + SCAN_PAD
    return pl.pallas_call(
        functools.partial(_scan_kernel, nb=b, tc=tc, nh=w // LANES, reverse=reverse),
        grid=(n_blk, nt),
        in_specs=in_specs,
        out_specs=blk,
        out_shape=jax.ShapeDtypeStruct((b, l, d), BF16 if reverse else F32),
        scratch_shapes=[pltpu.VMEM((b * ts, LANES), F32)] * (3 * (w // LANES)) + [pltpu.VMEM((b, w), F32)],
        compiler_params=_cparams(("parallel", "arbitrary")),
        name="rg_scan_" + ("rev" if reverse else "fwd"),
    )(*args)


def _mod_tables(mod_out, b, d):
    tabs = []
    for i in range(mod_out.shape[0]):
        ml = mod_out[i, :b].reshape(b, MOD_ROWS, d)
        mc = jnp.broadcast_to(mod_out[i, b].reshape(1, MOD_ROWS, d), (b, MOD_ROWS, d))
        pad = jnp.zeros((b, TAB_ROWS - 2 * MOD_ROWS, d), F32)
        tabs.append(jnp.concatenate([mc, ml, pad], axis=1))
    return tabs


def _moe_weights(router_w, router_b, w_gu, b_gu, w_dn, b_dn):
    d, n_exp = router_w.shape
    return {
        "rw": jnp.pad(router_w, ((0, 0), (0, LANES - n_exp))),
        "rb": jnp.pad(router_b, (0, LANES - n_exp)).reshape(1, LANES),
        "wg": w_gu[..., 0::2].astype(BF16),
        "wl": w_gu[..., 1::2].astype(BF16),
        "bg": b_gu[..., 0::2],
        "bl": b_gu[..., 1::2],
        "wd": w_dn.astype(BF16),
        "bd": b_dn,
    }


def kernel(x, c, ctx, c_ctx, mod_w, mod_b, norm1_g, norm2_g, final_g, na_w_qkv, na_w_o, na_rpb, rg_w_y, rg_b_y,
           rg_w_x, rg_b_x, rg_conv_w, rg_conv_b, rg_w_a, rg_b_a, rg_w_i, rg_b_i, rg_lam, rg_w_out, rg_b_out,
           moe_router_w, moe_router_b, moe_w_gu, moe_b_gu, moe_w_dn, moe_b_dn):
    b, n_lat, d = x.shape
    n_ctx = ctx.shape[1]
    l = n_ctx + n_lat
    heads = na_rpb.shape[1]
    rows = n_lat // GRID_W
    kr = min((na_rpb.shape[2] + 1) // 2, rows)
    assert mod_w.shape[0] == 2 and n_lat % n_ctx == 0 and d % LANES == 0

    mod_rows = -(-(b + 1) // 8) * 8
    cc = jnp.concatenate([c, c_ctx[None, :], jnp.zeros((mod_rows - b - 1, d), F32)], axis=0)
    tab0, tab1 = _mod_tables(_modulation(cc, mod_w, mod_b), b, d)

    xs0 = jnp.concatenate([ctx, x], axis=1)

    h = _prenorm(xs0, tab0, norm1_g[0], n_ctx)
    qkv = _matmul(h.reshape(b * l, d), na_w_qkv[0].astype(BF16), jnp.zeros((3 * d,), F32), BF16)
    bias = _na_bias_table(na_rpb[0], rows, kr)
    o = _attention(qkv.reshape(b, l, 3 * d), bias, n_ctx, heads)
    w0 = _moe_weights(moe_router_w[0], moe_router_b[0], moe_w_gu[0], moe_b_gu[0], moe_w_dn[0], moe_b_dn[0])
    x1, hrow, logits = _proj(o, na_w_o[0].astype(BF16), jnp.zeros((d,), F32), xs0, tab0, norm2_g[0],
                             w0["rw"], w0["rb"], n_ctx, latent_only=False)
    x2, h = _moe(hrow, logits, x1, tab0, tab1, norm1_g[1], w0, n_ctx, "next")

    hf = h.reshape(b * l, d)
    xb = _matmul(hf, rg_w_x[0].astype(BF16), rg_b_x[0], F32).reshape(b, l, d)
    gy = _matmul(hf, rg_w_y[0].astype(BF16), rg_b_y[0], F32, act="gelu").reshape(b, l, d)
    u = _conv(xb, rg_conv_w[0], rg_conv_b[0], n_ctx)
    w_a, w_i = rg_w_a[0].astype(BF16), rg_w_i[0].astype(BF16)
    h_fwd = _scan(u, w_a, rg_b_a[0], w_i, rg_b_i[0], rg_lam[0], 0, n_ctx)
    hg = _scan(u, w_a, rg_b_a[0], w_i, rg_b_i[0], rg_lam[0], 1, n_ctx, h_fwd, gy)
    w1 = _moe_weights(moe_router_w[1], moe_router_b[1], moe_w_gu[1], moe_b_gu[1], moe_w_dn[1], moe_b_dn[1])
    x1, hrow, logits = _proj(hg, rg_w_out[0].astype(BF16), rg_b_out[0], x2, tab1, norm2_g[1],
                             w1["rw"], w1["rb"], n_ctx, latent_only=True)
    _, out = _moe(hrow, logits, x1, tab1, tab1, final_g, w1, 0, "final")
    return out
```

```python
import functools

import jax
import jax.numpy as jnp
import numpy as np
from jax import lax
from jax.experimental import pallas as pl
from jax.experimental.pallas import tpu as pltpu

F32 = jnp.float32
BF16 = jnp.bfloat16

LANES = 128
GRID_W = 64
TOP_K = 4
RG_C = 8.0
CONV_LEFT = 2
SWIGLU_ALPHA = 1.702
SWIGLU_LIMIT = 7.0
RMS_EPS = 1e-6
MOD_ROWS = 6
TAB_ROWS = 16
VMEM_LIMIT = 56 * 1024 * 1024


def _cparams(sem):
    return pltpu.CompilerParams(dimension_semantics=sem, vmem_limit_bytes=VMEM_LIMIT)


def _pick(n, pref, mult=8):
    for t in range(min(pref, n), 0, -1):
        if n % t == 0 and t % mult == 0:
            return t
    return n


def _dot(a, b):
    return jnp.dot(a, b, preferred_element_type=F32)


def _dot_nt(a, b):
    return lax.dot_general(a, b, (((1,), (1,)), ((), ())), preferred_element_type=F32)


def _split_bf16(x):
    hi = x.astype(BF16)
    lo = (x - hi.astype(F32)).astype(BF16)
    return hi, lo


def _dot3(a, w):
    a_hi, a_lo = _split_bf16(a)
    w_hi, w_lo = _split_bf16(w)
    return _dot(a_hi, w_hi) + _dot(a_lo, w_hi) + _dot(a_hi, w_lo)


def _norm_mod(x, g, shift, scale):
    y = x * lax.rsqrt(jnp.mean(x * x, axis=-1, keepdims=True) + RMS_EPS)
    return (y * g) * (1.0 + scale) + shift


def _tab_row(tab_ref, is_ctx, k):
    base = jnp.where(is_ctx, 0, MOD_ROWS)
    return tab_ref[0, pl.ds(base + k, 1), :]


def _mod_kernel(a_ref, w_ref, b_ref, o_ref):
    a = a_ref[...]
    a = a * jax.nn.sigmoid(a)
    o_ref[0] = _dot3(a, w_ref[0]) + b_ref[0]


def _modulation(cc, mod_w, mod_b):
    depth, d, n = mod_w.shape
    r = cc.shape[0]
    tn = _pick(n, 1024, LANES)
    return pl.pallas_call(
        _mod_kernel,
        grid=(depth, n // tn),
        in_specs=[
            pl.BlockSpec((r, d), lambda i, j: (0, 0)),
            pl.BlockSpec((1, d, tn), lambda i, j: (i, 0, j)),
            pl.BlockSpec((1, 1, tn), lambda i, j: (i, 0, j)),
        ],
        out_specs=pl.BlockSpec((1, r, tn), lambda i, j: (i, 0, j)),
        out_shape=jax.ShapeDtypeStruct((depth, r, n), F32),
        compiler_params=_cparams(("parallel", "parallel")),
        name="modulation",
    )(cc, mod_w, mod_b.reshape(depth, 1, n))


def _prenorm_kernel(x_ref, tab_ref, g_ref, o_ref, *, ctx_tiles):
    is_ctx = pl.program_id(1) < ctx_tiles
    h = _norm_mod(x_ref[0], g_ref[...], _tab_row(tab_ref, is_ctx, 0), _tab_row(tab_ref, is_ctx, 1))
    o_ref[0] = h.astype(BF16)


def _prenorm(x, tab, g, n_ctx):
    b, l, d = x.shape
    tm = n_ctx
    return pl.pallas_call(
        functools.partial(_prenorm_kernel, ctx_tiles=n_ctx // tm),
        grid=(b, l // tm),
        in_specs=[
            pl.BlockSpec((1, tm, d), lambda i, j: (i, j, 0)),
            pl.BlockSpec((1, TAB_ROWS, d), lambda i, j: (i, 0, 0)),
            pl.BlockSpec((1, d), lambda i, j: (0, 0)),
        ],
        out_specs=pl.BlockSpec((1, tm, d), lambda i, j: (i, j, 0)),
        out_shape=jax.ShapeDtypeStruct((b, l, d), BF16),
        compiler_params=_cparams(("parallel", "parallel")),
        name="prenorm",
    )(x, tab, g.reshape(1, d))


def _gelu_tanh(x):
    return 0.5 * x * (1.0 + jnp.tanh(np.sqrt(2.0 / np.pi) * (x + 0.044715 * (x * x * x))))


def _matmul_kernel(a_ref, w_ref, b_ref, o_ref, *, act):
    y = _dot(a_ref[...], w_ref[...]) + b_ref[...]
    if act == "gelu":
        y = _gelu_tanh(y)
    o_ref[...] = y.astype(o_ref.dtype)


def _matmul(a, w, bias, out_dtype, act=None):
    m, k = a.shape
    n = w.shape[1]
    tm = _pick(m, 1024)
    tn = _pick(n, 512, LANES)
    return pl.pallas_call(
        functools.partial(_matmul_kernel, act=act),
        grid=(m // tm, n // tn),
        in_specs=[
            pl.BlockSpec((tm, k), lambda i, j: (i, 0)),
            pl.BlockSpec((k, tn), lambda i, j: (0, j)),
            pl.BlockSpec((1, tn), lambda i, j: (0, j)),
        ],
        out_specs=pl.BlockSpec((tm, tn), lambda i, j: (i, j)),
        out_shape=jax.ShapeDtypeStruct((m, n), out_dtype),
        compiler_params=_cparams(("parallel", "parallel")),
        name="matmul_" + (act or "linear"),
    )(a, w, bias.reshape(1, n))


def _softmax_parts(parts):
    m = parts[0].max(axis=-1, keepdims=True)
    for s in parts[1:]:
        m = jnp.maximum(m, s.max(axis=-1, keepdims=True))
    ps = [jnp.exp(s - m) for s in parts]
    den = ps[0].sum(axis=-1, keepdims=True)
    for p in ps[1:]:
        den = den + p.sum(axis=-1, keepdims=True)
    return ps, den


ATTN_ROWS_PER_ITER = 8


def _attn_kernel(q_ref, k_ref, v_ref, bias_ref, o_ref, *, n_ctx, rows, kr, scale):
    kc = k_ref[0, 0:n_ctx, :]
    vc = v_ref[0, 0:n_ctx, :]
    (p,), den = _softmax_parts([_dot_nt(q_ref[0, 0:n_ctx, :], kc) * scale])
    o_ref[0, 0:n_ctx, :] = (_dot(p.astype(BF16), vc) / den).astype(BF16)

    group = next(g for g in (ATTN_ROWS_PER_ITER, 2, 1) if rows % g == 0)

    def row_group(i, carry):
        rr = [i * group + j for j in range(group)]
        rs = [jnp.clip(r - kr // 2, 0, rows - kr) for r in rr]
        q0 = [pl.multiple_of(n_ctx + r * GRID_W, GRID_W) for r in rr]
        k0 = [pl.multiple_of(n_ctx + s * GRID_W, GRID_W) for s in rs]
        scores = []
        for j in range(group):
            q = q_ref[0, pl.ds(q0[j], GRID_W), :]
            s_lat = _dot_nt(q, k_ref[0, pl.ds(k0[j], kr * GRID_W), :]) * scale + bias_ref[0, rr[j] - rs[j]]
            scores.append([s_lat, _dot_nt(q, kc) * scale])
        probs = [_softmax_parts(s) for s in scores]
        for j in range(group):
            (p_lat, p_ctx), den = probs[j]
            o = _dot(p_lat.astype(BF16), v_ref[0, pl.ds(k0[j], kr * GRID_W), :]) + _dot(p_ctx.astype(BF16), vc)
            o_ref[0, pl.ds(q0[j], GRID_W), :] = (o / den).astype(BF16)
        return carry

    lax.fori_loop(0, rows // group, row_group, 0)


def _na_bias_table(rpb, rows, kr):
    h, n_dr, n_dc = rpb.shape
    win_rows, win_cols = (n_dr + 1) // 2, (n_dc + 1) // 2
    col = np.arange(GRID_W)
    col_start = np.clip(col - win_cols // 2, 0, GRID_W - win_cols)
    col_mask = (col[None, :] >= col_start[:, None]) & (col[None, :] < col_start[:, None] + win_cols)
    dc_idx = np.clip(col[None, :] - col[:, None], 1 - win_cols, win_cols - 1) + win_cols - 1
    dr_idx = np.arange(kr)[None, :] - np.arange(kr)[:, None] + win_rows - 1
    t = rpb[:, dr_idx][:, :, :, dc_idx]
    t = jnp.where(col_mask[None, None, None], t.astype(F32), -jnp.inf)
    return t.transpose(0, 1, 3, 2, 4).reshape(h, kr, GRID_W, kr * GRID_W)


def _attention(qkv, bias, n_ctx, heads):
    b, l, d3 = qkv.shape
    d = d3 // 3
    dh = d // heads
    rows = (l - n_ctx) // GRID_W
    kr = bias.shape[1]
    kern = functools.partial(_attn_kernel, n_ctx=n_ctx, rows=rows, kr=kr, scale=dh ** -0.5)
    return pl.pallas_call(
        kern,
        grid=(heads, b),
        in_specs=[
            pl.BlockSpec((1, l, dh), lambda h, i: (i, 0, h)),
            pl.BlockSpec((1, l, dh), lambda h, i: (i, 0, heads + h)),
            pl.BlockSpec((1, l, dh), lambda h, i: (i, 0, 2 * heads + h)),
            pl.BlockSpec((1, kr, GRID_W, kr * GRID_W), lambda h, i: (h, 0, 0, 0)),
        ],
        out_specs=pl.BlockSpec((1, l, dh), lambda h, i: (i, 0, h)),
        out_shape=jax.ShapeDtypeStruct((b, l, d), BF16),
        compiler_params=_cparams(("parallel", "parallel")),
        name="na_attention",
    )(qkv, qkv, qkv, bias)


def _proj_kernel(a_ref, w_ref, b_ref, x_ref, tab_ref, g_ref, rw_ref, rb_ref,
                 x1_ref, hrow_ref, lg_ref, *, ctx_tiles, tile_off):
    is_ctx = pl.program_id(1) + tile_off < ctx_tiles
    y = _dot(a_ref[0], w_ref[...]) + b_ref[...]
    x1 = x_ref[0] + _tab_row(tab_ref, is_ctx, 2) * y
    x1_ref[0] = x1
    h2 = _norm_mod(x1, g_ref[...], _tab_row(tab_ref, is_ctx, 3), _tab_row(tab_ref, is_ctx, 4))
    lg_ref[...] = _dot3(h2, rw_ref[...]) + rb_ref[...]
    tm, d = h2.shape
    s_rows = d // LANES
    for s in range(s_rows):
        hrow_ref[pl.ds(s, tm, stride=s_rows), :] = h2[:, s * LANES:(s + 1) * LANES]


def _proj(a, w, bias, x, tab, g2, rw, rb, n_ctx, latent_only):
    b, l, d = x.shape
    tm = n_ctx
    off = n_ctx // tm if latent_only else 0
    nt = l // tm - off
    s_rows = d // LANES
    t = b * nt * tm
    kern = functools.partial(_proj_kernel, ctx_tiles=n_ctx // tm, tile_off=off)
    return pl.pallas_call(
        kern,
        grid=(b, nt),
        in_specs=[
            pl.BlockSpec((1, tm, d), lambda i, j: (i, j + off, 0)),
            pl.BlockSpec((d, d), lambda i, j: (0, 0)),
            pl.BlockSpec((1, d), lambda i, j: (0, 0)),
            pl.BlockSpec((1, tm, d), lambda i, j: (i, j + off, 0)),
            pl.BlockSpec((1, TAB_ROWS, d), lambda i, j: (i, 0, 0)),
            pl.BlockSpec((1, d), lambda i, j: (0, 0)),
            pl.BlockSpec((d, LANES), lambda i, j: (0, 0)),
            pl.BlockSpec((1, LANES), lambda i, j: (0, 0)),
        ],
        out_specs=[
            pl.BlockSpec((1, tm, d), lambda i, j: (i, j, 0)),
            pl.BlockSpec((tm * s_rows, LANES), lambda i, j: (i * nt + j, 0)),
            pl.BlockSpec((tm, LANES), lambda i, j: (i * nt + j, 0)),
        ],
        out_shape=[
            jax.ShapeDtypeStruct((b, nt * tm, d), F32),
            jax.ShapeDtypeStruct((t * s_rows, LANES), F32),
            jax.ShapeDtypeStruct((t, LANES), F32),
        ],
        compiler_params=_cparams(("parallel", "parallel")),
        name="mixer_proj",
    )(a, w, bias.reshape(1, d), x, tab, g2.reshape(1, d), rw, rb)


def _route_kernel(lg_ref, e_ref, g_ref, r_ref, cnt_ref, carry_ref, *, n_exp):
    @pl.when(pl.program_id(0) == 0)
    def _():
        carry_ref[...] = jnp.zeros_like(carry_ref)

    lg = lg_ref[...]
    tm = lg.shape[0]
    lane = lax.broadcasted_iota(jnp.int32, lg.shape, 1).astype(F32)
    cur = jnp.where(lane < n_exp, lg, -jnp.inf)
    multi = jnp.zeros(lg.shape, F32)
    vals, idxs = [], []
    for _ in range(TOP_K):
        m = cur.max(axis=-1, keepdims=True)
        idx = jnp.where(cur == m, lane, float(LANES)).min(axis=-1, keepdims=True)
        sel = lane == idx
        multi = jnp.where(sel, 1.0, multi)
        cur = jnp.where(sel, -jnp.inf, cur)
        vals.append(m)
        idxs.append(idx)
    exps = [jnp.exp(v - vals[0]) for v in vals]
    den = exps[0]
    for e in exps[1:]:
        den = den + e
    tri = (lax.broadcasted_iota(jnp.int32, (tm, tm), 0) > lax.broadcasted_iota(jnp.int32, (tm, tm), 1))
    pref = _dot(jnp.where(tri, 1.0, 0.0).astype(BF16), multi.astype(BF16))
    tot = carry_ref[...] + pref
    e_out = jnp.zeros(lg.shape, F32)
    g_out = jnp.zeros(lg.shape, F32)
    r_out = jnp.zeros(lg.shape, F32)
    for k in range(TOP_K):
        rank_k = jnp.where(lane == idxs[k], tot, 0.0).sum(axis=-1, keepdims=True)
        e_out = jnp.where(lane == k, idxs[k], e_out)
        g_out = jnp.where(lane == k, exps[k] / den, g_out)
        r_out = jnp.where(lane == k, rank_k, r_out)
    e_ref[...] = e_out.astype(jnp.int32)
    g_ref[...] = g_out
    r_ref[...] = r_out.astype(jnp.int32)
    carry_ref[...] = carry_ref[...] + multi.sum(axis=0, keepdims=True)
    cnt_ref[...] = carry_ref[...]


def _route(logits, n_exp):
    t = logits.shape[0]
    tm = _pick(t, 256)
    spec = pl.BlockSpec((tm, LANES), lambda i: (i, 0))
    return pl.pallas_call(
        functools.partial(_route_kernel, n_exp=n_exp),
        grid=(t // tm,),
        in_specs=[spec],
        out_specs=[spec, spec, spec, pl.BlockSpec((1, LANES), lambda i: (0, 0))],
        out_shape=[
            jax.ShapeDtypeStruct((t, LANES), jnp.int32),
            jax.ShapeDtypeStruct((t, LANES), F32),
            jax.ShapeDtypeStruct((t, LANES), jnp.int32),
            jax.ShapeDtypeStruct((1, LANES), F32),
        ],
        scratch_shapes=[pltpu.VMEM((1, LANES), F32)],
        compiler_params=_cparams(("arbitrary",)),
        name="router",
    )(logits)


def _dispatch_kernel(dest_ref, h_ref, xs_ref, sem, *, td, s_rows):
    def issue(t, carry):
        src = h_ref.at[pl.ds(pl.multiple_of(t * s_rows, s_rows), s_rows), :]
        for k in range(TOP_K):
            d = dest_ref[0, 0, t * TOP_K + k]
            dst = xs_ref.at[pl.ds(pl.multiple_of(d * s_rows, s_rows), s_rows), :]
            pltpu.make_async_copy(src, dst, sem).start()
        return carry

    lax.fori_loop(0, td, issue, 0)
    for _ in range(TOP_K):
        pltpu.make_async_copy(h_ref, xs_ref.at[pl.ds(0, td * s_rows), :], sem).wait()


def _dispatch(hrow, dest, s_rows):
    t = dest.shape[0] // TOP_K
    td = _pick(t, 256)
    nt = t // td
    return pl.pallas_call(
        functools.partial(_dispatch_kernel, td=td, s_rows=s_rows),
        grid=(nt,),
        in_specs=[
            pl.BlockSpec((1, 1, td * TOP_K), lambda i: (i, 0, 0), memory_space=pltpu.SMEM),
            pl.BlockSpec((td * s_rows, LANES), lambda i: (i, 0)),
        ],
        out_specs=pl.BlockSpec(memory_space=pl.ANY),
        out_shape=jax.ShapeDtypeStruct((t * TOP_K * s_rows, LANES), F32),
        scratch_shapes=[pltpu.SemaphoreType.DMA(())],
        compiler_params=pltpu.CompilerParams(dimension_semantics=("arbitrary",), vmem_limit_bytes=VMEM_LIMIT,
                                             has_side_effects=True),
        name="moe_dispatch",
    )(dest.reshape(nt, 1, td * TOP_K), hrow)


def _gmm_kernel(it_tile, it_e, it_lo, it_hi, it_first, it_valid,
                xs_ref, wg_ref, wl_ref, bg_ref, bl_ref, wd_ref, bd_ref, ys_ref, xb_ref, acc_ref,
                *, tg, s_rows, n_chunks):
    m = pl.program_id(0)
    c = pl.program_id(1)

    @pl.when(it_valid[m] == 1)
    def _():
        @pl.when(c == 0)
        def _():
            for s in range(s_rows):
                xb_ref[:, s * LANES:(s + 1) * LANES] = xs_ref[pl.ds(s, tg, stride=s_rows), :].astype(BF16)

        x = xb_ref[...]
        glu = jnp.minimum(_dot(x, wg_ref[0]) + bg_ref[0], SWIGLU_LIMIT)
        lin = jnp.clip(_dot(x, wl_ref[0]) + bl_ref[0], -SWIGLU_LIMIT, SWIGLU_LIMIT)
        act = glu * jax.nn.sigmoid(SWIGLU_ALPHA * glu) * (lin + 1.0)
        part = _dot(act.astype(BF16), wd_ref[0])

        @pl.when(c == 0)
        def _():
            acc_ref[...] = part

        @pl.when(c > 0)
        def _():
            acc_ref[...] += part

        @pl.when(c == n_chunks - 1)
        def _():
            y = acc_ref[...] + bd_ref[0]
            row = lax.broadcasted_iota(jnp.int32, (tg, 1), 0)
            mine = (row >= it_lo[m]) & (row < it_hi[m])

            @pl.when(it_first[m] == 1)
            def _():
                for s in range(s_rows):
                    ys_ref[pl.ds(s, tg, stride=s_rows), :] = jnp.where(mine, y[:, s * LANES:(s + 1) * LANES], 0.0)

            @pl.when(it_first[m] == 0)
            def _():
                for s in range(s_rows):
                    old = ys_ref[pl.ds(s, tg, stride=s_rows), :]
                    ys_ref[pl.ds(s, tg, stride=s_rows), :] = jnp.where(mine, y[:, s * LANES:(s + 1) * LANES], old)


def _gmm_items(counts, n_tiles, tg, max_items):
    n_exp = counts.shape[0]
    ends = jnp.cumsum(counts)
    starts = ends - counts
    def count_le(sorted_vals, q):
        return jnp.sum(sorted_vals[None, :] <= q[:, None], axis=1, dtype=jnp.int32)

    tile0 = jnp.arange(n_tiles, dtype=jnp.int32) * tg
    e_lo = jnp.minimum(count_le(ends, tile0), n_exp - 1)
    e_hi = jnp.minimum(count_le(ends, tile0 + tg - 1), n_exp - 1)
    n_items = e_hi - e_lo + 1
    item_end = jnp.cumsum(n_items)
    item_start = item_end - n_items
    total = item_end[-1]
    m = jnp.arange(max_items, dtype=jnp.int32)
    valid = m < total
    tile = jnp.minimum(count_le(item_end, m), n_tiles - 1)
    e = jnp.where(valid, e_lo[tile] + (m - item_start[tile]), e_hi[n_tiles - 1]).astype(jnp.int32)
    lo = jnp.clip(starts[e] - tile * tg, 0, tg).astype(jnp.int32)
    hi = jnp.clip(ends[e] - tile * tg, 0, tg).astype(jnp.int32)
    first = (m == item_start[tile]).astype(jnp.int32)
    return tile, e, lo, hi, first, valid.astype(jnp.int32)


def _gmm(xs, counts, wg, wl, bg, bl, wd, bd, s_rows):
    n_exp, d, f = wg.shape
    p = xs.shape[0] // s_rows
    tg = _pick(p, 512)
    fk = _pick(f, 512, LANES)
    n_tiles, n_chunks = p // tg, f // fk
    max_items = n_tiles + n_exp - 1
    items = _gmm_items(counts, n_tiles, tg, max_items)

    def chunk(c, valid, m):
        return jnp.where(valid[m] == 1, c, n_chunks - 1)

    grid_spec = pltpu.PrefetchScalarGridSpec(
        num_scalar_prefetch=6,
        grid=(max_items, n_chunks),
        in_specs=[
            pl.BlockSpec((tg * s_rows, LANES), lambda m, c, t, e, lo, hi, fi, va: (t[m], 0)),
            pl.BlockSpec((1, d, fk), lambda m, c, t, e, lo, hi, fi, va: (e[m], 0, chunk(c, va, m))),
            pl.BlockSpec((1, d, fk), lambda m, c, t, e, lo, hi, fi, va: (e[m], 0, chunk(c, va, m))),
            pl.BlockSpec((1, 1, fk), lambda m, c, t, e, lo, hi, fi, va: (e[m], 0, chunk(c, va, m))),
            pl.BlockSpec((1, 1, fk), lambda m, c, t, e, lo, hi, fi, va: (e[m], 0, chunk(c, va, m))),
            pl.BlockSpec((1, fk, d), lambda m, c, t, e, lo, hi, fi, va: (e[m], chunk(c, va, m), 0)),
            pl.BlockSpec((1, 1, d), lambda m, c, t, e, lo, hi, fi, va: (e[m], 0, 0)),
        ],
        out_specs=pl.BlockSpec((tg * s_rows, LANES), lambda m, c, t, e, lo, hi, fi, va: (t[m], 0)),
        scratch_shapes=[pltpu.VMEM((tg, d), BF16), pltpu.VMEM((tg, d), F32)],
    )
    return pl.pallas_call(
        functools.partial(_gmm_kernel, tg=tg, s_rows=s_rows, n_chunks=n_chunks),
        grid_spec=grid_spec,
        out_shape=jax.ShapeDtypeStruct(xs.shape, F32),
        compiler_params=_cparams(("arbitrary", "arbitrary")),
        name="moe_experts",
    )(*items, xs, wg, wl, bg.reshape(n_exp, 1, f), bl.reshape(n_exp, 1, f), wd, bd.reshape(n_exp, 1, d))


def _combine_kernel(dest_ref, dest_next_ref, ys_ref, gate_ref, x_ref, tab_ref, tabn_ref, g_ref,
                    x2_ref, h_ref, buf_ref, moe_ref, sem, *, tc, s_rows, ctx_tiles, n_steps, mode):
    step = pl.program_id(0) * pl.num_programs(1) + pl.program_id(1)
    slot = step % 2

    def gather(dref, slot_idx):
        def issue(t, carry):
            for k in range(TOP_K):
                d = dref[0, 0, t * TOP_K + k]
                src = ys_ref.at[pl.ds(pl.multiple_of(d * s_rows, s_rows), s_rows), :]
                dst = buf_ref.at[slot_idx, pl.ds(pl.multiple_of((k * tc + t) * s_rows, s_rows), s_rows), :]
                pltpu.make_async_copy(src, dst, sem.at[slot_idx]).start()
            return carry

        lax.fori_loop(0, tc, issue, 0)

    @pl.when(step == 0)
    def _():
        gather(dest_ref, 0)

    @pl.when(step + 1 < n_steps)
    def _():
        gather(dest_next_ref, 1 - slot)

    pltpu.make_async_copy(buf_ref.at[slot], buf_ref.at[slot], sem.at[slot]).wait()

    gates = gate_ref[...]
    for s in range(s_rows):
        acc = None
        for k in range(TOP_K):
            rows = buf_ref[slot, pl.ds(k * tc * s_rows + s, tc, stride=s_rows), :]
            term = rows * gates[:, k:k + 1]
            acc = term if acc is None else acc + term
        moe_ref[:, s * LANES:(s + 1) * LANES] = acc

    is_ctx = pl.program_id(1) < ctx_tiles
    x2 = x_ref[0] + _tab_row(tab_ref, is_ctx, 5) * moe_ref[...]
    x2_ref[0] = x2
    if mode == "next":
        h = _norm_mod(x2, g_ref[...], _tab_row(tabn_ref, is_ctx, 0), _tab_row(tabn_ref, is_ctx, 1))
        h_ref[0] = h.astype(h_ref.dtype)
    else:
        y = x2 * lax.rsqrt(jnp.mean(x2 * x2, axis=-1, keepdims=True) + RMS_EPS)
        h_ref[0] = (y * g_ref[...]).astype(h_ref.dtype)


def _combine(ys, dest, gates, x1, tab, tab_next, g_next, n_ctx_rows, s_rows, mode):
    b, l, d = x1.shape
    tc = _pick(min(l, 128) if n_ctx_rows == 0 else n_ctx_rows, 128)
    nt = l // tc
    n_steps = b * nt
    dest3 = dest.reshape(n_steps, 1, tc * TOP_K)
    kern = functools.partial(_combine_kernel, tc=tc, s_rows=s_rows, ctx_tiles=n_ctx_rows // tc,
                             n_steps=n_steps, mode=mode)
    out_dtype = BF16 if mode == "next" else F32
    return pl.pallas_call(
        kern,
        grid=(b, nt),
        in_specs=[
            pl.BlockSpec((1, 1, tc * TOP_K), lambda i, j: (i * nt + j, 0, 0), memory_space=pltpu.SMEM),
            pl.BlockSpec((1, 1, tc * TOP_K), lambda i, j: (jnp.minimum(i * nt + j + 1, n_steps - 1), 0, 0),
                         memory_space=pltpu.SMEM),
            pl.BlockSpec(memory_space=pl.ANY),
            pl.BlockSpec((tc, LANES), lambda i, j: (i * nt + j, 0)),
            pl.BlockSpec((1, tc, d), lambda i, j: (i, j, 0)),
            pl.BlockSpec((1, TAB_ROWS, d), lambda i, j: (i, 0, 0)),
            pl.BlockSpec((1, TAB_ROWS, d), lambda i, j: (i, 0, 0)),
            pl.BlockSpec((1, d), lambda i, j: (0, 0)),
        ],
        out_specs=[
            pl.BlockSpec((1, tc, d), lambda i, j: (i, j, 0)),
            pl.BlockSpec((1, tc, d), lambda i, j: (i, j, 0)),
        ],
        out_shape=[
            jax.ShapeDtypeStruct((b, l, d), F32),
            jax.ShapeDtypeStruct((b, l, d), out_dtype),
        ],
        scratch_shapes=[
            pltpu.VMEM((2, tc * TOP_K * s_rows, LANES), F32),
            pltpu.VMEM((tc, d), F32),
            pltpu.SemaphoreType.DMA((2,)),
        ],
        compiler_params=_cparams(("arbitrary", "arbitrary")),
        name="moe_combine_" + mode,
    )(dest3, dest3, ys, gates, x1, tab, tab_next, g_next.reshape(1, d))


def _moe(hrow, logits, x1, tab, tab_next, g_next, w, n_ctx_rows, mode):
    n_exp = w["wg"].shape[0]
    s_rows = x1.shape[-1] // LANES
    e_pad, g_pad, r_pad, cnt = _route(logits, n_exp)
    top_e, rank = e_pad[:, :TOP_K], r_pad[:, :TOP_K]
    counts = cnt[0, :n_exp].astype(jnp.int32)
    starts = jnp.cumsum(counts) - counts
    dest = (starts[top_e] + rank).reshape(-1)
    xs = _dispatch(hrow, dest, s_rows)
    ys = _gmm(xs, counts, w["wg"], w["wl"], w["bg"], w["bl"], w["wd"], w["bd"], s_rows)
    return _combine(ys, dest, g_pad, x1, tab, tab_next, g_next, n_ctx_rows, s_rows, mode)


def _conv_kernel(x_ref, w_ref, b_ref, o_ref, *, n_ctx):
    l = x_ref.shape[1]
    w = w_ref[...]
    for s0, n in ((0, n_ctx), (n_ctx, l - n_ctx)):
        x = x_ref[0, s0:s0 + n, :]
        row = lax.broadcasted_iota(jnp.int32, (n, 1), 0)
        acc = x * w[CONV_LEFT:CONV_LEFT + 1]
        for j in range(w.shape[0]):
            off = j - CONV_LEFT
            if off == 0:
                continue
            shifted = pltpu.roll(x, (-off) % n, 0)
            ok = (row + off >= 0) & (row + off < n)
            acc = acc + jnp.where(ok, shifted, 0.0) * w[j:j + 1]
        o_ref[0, s0:s0 + n, :] = acc + b_ref[...]


def _conv(xb, conv_w, conv_b, n_ctx):
    b, l, d = xb.shape
    dt = _pick(d, 256, LANES)
    return pl.pallas_call(
        functools.partial(_conv_kernel, n_ctx=n_ctx),
        grid=(b, d // dt),
        in_specs=[
            pl.BlockSpec((1, l, dt), lambda i, j: (i, 0, j)),
            pl.BlockSpec((conv_w.shape[0], dt), lambda i, j: (0, j)),
            pl.BlockSpec((1, dt), lambda i, j: (0, j)),
        ],
        out_specs=pl.BlockSpec((1, l, dt), lambda i, j: (i, 0, j)),
        out_shape=jax.ShapeDtypeStruct((b, l, d), F32),
        compiler_params=_cparams(("parallel", "parallel")),
        name="rg_conv",
    )(xb, conv_w, conv_b.reshape(1, d))


SCAN_PAD = 8
SCAN_UNROLL = 8


def _scan_kernel(*refs, nb, tc, nh, reverse):
    n_in = 8 if reverse else 6
    u_ref, wa_ref, ba_ref, wi_ref, bi_ref, lam_ref = refs[:6]
    o_ref = refs[n_in]
    scratch = refs[n_in + 1:]
    a_s, x_s, h_s = scratch[0:nh], scratch[nh:2 * nh], scratch[2 * nh:3 * nh]
    carry = scratch[3 * nh]
    ts = tc + SCAN_PAD

    @pl.when(pl.program_id(1) == 0)
    def _():
        carry[...] = jnp.zeros_like(carry)

    neg = -lam_ref[0]
    softplus = jnp.maximum(neg, 0.0) + jnp.log1p(jnp.exp(-jnp.abs(neg)))
    for bi in range(nb):
        u = u_ref[bi]
        ub = u.astype(BF16)
        r = jax.nn.sigmoid(_dot(ub, wa_ref[0, 0]) + ba_ref[0, 0])
        i = jax.nn.sigmoid(_dot(ub, wi_ref[0, 0]) + bi_ref[0, 0])
        log_a = (-RG_C) * r * softplus
        a = jnp.exp(log_a)
        xin = jnp.sqrt(1.0 - a * a) * (i * u)
        for p in range(nh):
            a_s[p][pl.ds(bi * ts, tc), :] = a[:, p * LANES:(p + 1) * LANES]
            x_s[p][pl.ds(bi * ts, tc), :] = xin[:, p * LANES:(p + 1) * LANES]

    def block(j, hs):
        hs = list(hs)
        for q in range(SCAN_UNROLL):
            t = j * SCAN_UNROLL + q
            if reverse:
                t = tc - 1 - t
            for p in range(nh):
                hs[p] = a_s[p][pl.ds(t, nb, stride=ts), :] * hs[p] + x_s[p][pl.ds(t, nb, stride=ts), :]
                h_s[p][pl.ds(t, nb, stride=ts), :] = hs[p]
        return tuple(hs)

    h0 = tuple(carry[:, p * LANES:(p + 1) * LANES] for p in range(nh))
    hs = lax.fori_loop(0, tc // SCAN_UNROLL, block, h0)
    for p in range(nh):
        carry[:, p * LANES:(p + 1) * LANES] = hs[p]
    for bi in range(nb):
        for p in range(nh):
            h = h_s[p][pl.ds(bi * ts, tc), :]
            cols = slice(p * LANES, (p + 1) * LANES)
            if reverse:
                hf_ref, gy_ref = refs[6], refs[7]
                o_ref[bi, :, cols] = ((hf_ref[bi, :, cols] + h) * gy_ref[bi, :, cols]).astype(o_ref.dtype)
            else:
                o_ref[bi, :, cols] = h


def _scan(u, w_a, b_a, w_i, b_i, lam, dirn, n_ctx, h_fwd=None, gy=None):
    b, l, d = u.shape
    n_blk, w = w_a.shape[1], w_a.shape[2]
    tc = _pick(n_ctx, 128)
    nt, nc = l // tc, n_ctx // tc
    reverse = dirn == 1

    def chunk(j):
        if not reverse:
            return j
        return jnp.where(j < nc, nc - 1 - j, nt - 1 - (j - nc))

    blk = pl.BlockSpec((b, tc, w), lambda g, j: (0, chunk(j), g))
    in_specs = [
        blk,
        pl.BlockSpec((1, 1, w, w), lambda g, j: (dirn, g, 0, 0)),
        pl.BlockSpec((1, 1, 1, w), lambda g, j: (dirn, g, 0, 0)),
        pl.BlockSpec((1, 1, w, w), lambda g, j: (dirn, g, 0, 0)),
        pl.BlockSpec((1, 1, 1, w), lambda g, j: (dirn, g, 0, 0)),
        pl.BlockSpec((1, 1, w), lambda g, j: (dirn, 0, g)),
    ]
    args = [u, w_a, b_a.reshape(2, n_blk, 1, w), w_i, b_i.reshape(2, n_blk, 1, w), lam.reshape(2, 1, d)]
    if reverse:
        in_specs += [blk, blk]
        args += [h_fwd, gy]
    ts = tc + SCAN_PAD
    return pl.pallas_call(
        functools.partial(_scan_kernel, nb=b, tc=tc, nh=w // LANES, reverse=reverse),
        grid=(n_blk, nt),
        in_specs=in_specs,
        out_specs=blk,
        out_shape=jax.ShapeDtypeStruct((b, l, d), BF16 if reverse else F32),
        scratch_shapes=[pltpu.VMEM((b * ts, LANES), F32)] * (3 * (w // LANES)) + [pltpu.VMEM((b, w), F32)],
        compiler_params=_cparams(("parallel", "arbitrary")),
        name="rg_scan_" + ("rev" if reverse else "fwd"),
    )(*args)


def _mod_tables(mod_out, b, d):
    tabs = []
    for i in range(mod_out.shape[0]):
        ml = mod_out[i, :b].reshape(b, MOD_ROWS, d)
        mc = jnp.broadcast_to(mod_out[i, b].reshape(1, MOD_ROWS, d), (b, MOD_ROWS, d))
        pad = jnp.zeros((b, TAB_ROWS - 2 * MOD_ROWS, d), F32)
        tabs.append(jnp.concatenate([mc, ml, pad], axis=1))
    return tabs


def _split_gu_kernel(w_ref, perm_ref, wg_ref, wl_ref):
    for j in range(w_ref.shape[1] // (2 * LANES)):
        y = _dot(w_ref[:, 2 * LANES * j:2 * LANES * (j + 1)].astype(BF16), perm_ref[...])
        wg_ref[:, LANES * j:LANES * (j + 1)] = y[:, :LANES].astype(BF16)
        wl_ref[:, LANES * j:LANES * (j + 1)] = y[:, LANES:].astype(BF16)


def _split_gu(w_gu):
    lead, n2 = w_gu.shape[:-1], w_gu.shape[-1]
    rows = int(np.prod(lead))
    tr = _pick(rows, 512)
    src = np.concatenate([np.arange(0, 2 * LANES, 2), np.arange(1, 2 * LANES, 2)])
    perm = jnp.asarray(np.arange(2 * LANES)[:, None] == src[None, :], BF16)
    out_spec = pl.BlockSpec((tr, n2 // 2), lambda i: (i, 0))
    wg, wl = pl.pallas_call(
        _split_gu_kernel,
        grid=(rows // tr,),
        in_specs=[pl.BlockSpec((tr, n2), lambda i: (i, 0)), pl.BlockSpec((2 * LANES, 2 * LANES), lambda i: (0, 0))],
        out_specs=[out_spec, out_spec],
        out_shape=[jax.ShapeDtypeStruct((rows, n2 // 2), BF16)] * 2,
        compiler_params=_cparams(("parallel",)),
        name="split_gu_weights",
    )(w_gu.reshape(rows, n2), perm)
    return wg.reshape(*lead, n2 // 2), wl.reshape(*lead, n2 // 2)


def _moe_weights(router_w, router_b, wg, wl, b_gu, w_dn, b_dn):
    d, n_exp = router_w.shape
    return {
        "rw": jnp.pad(router_w, ((0, 0), (0, LANES - n_exp))),
        "rb": jnp.pad(router_b, (0, LANES - n_exp)).reshape(1, LANES),
        "wg": wg,
        "wl": wl,
        "bg": b_gu[..., 0::2],
        "bl": b_gu[..., 1::2],
        "wd": w_dn.astype(BF16),
        "bd": b_dn,
    }


def kernel(x, c, ctx, c_ctx, mod_w, mod_b, norm1_g, norm2_g, final_g, na_w_qkv, na_w_o, na_rpb, rg_w_y, rg_b_y,
           rg_w_x, rg_b_x, rg_conv_w, rg_conv_b, rg_w_a, rg_b_a, rg_w_i, rg_b_i, rg_lam, rg_w_out, rg_b_out,
           moe_router_w, moe_router_b, moe_w_gu, moe_b_gu, moe_w_dn, moe_b_dn):
    b, n_lat, d = x.shape
    n_ctx = ctx.shape[1]
    l = n_ctx + n_lat
    heads = na_rpb.shape[1]
    rows = n_lat // GRID_W
    kr = min((na_rpb.shape[2] + 1) // 2, rows)
    assert mod_w.shape[0] == 2 and n_lat % n_ctx == 0 and d % LANES == 0

    mod_rows = -(-(b + 1) // 8) * 8
    cc = jnp.concatenate([c, c_ctx[None, :], jnp.zeros((mod_rows - b - 1, d), F32)], axis=0)
    tab0, tab1 = _mod_tables(_modulation(cc, mod_w, mod_b), b, d)

    xs0 = jnp.concatenate([ctx, x], axis=1)

    h = _prenorm(xs0, tab0, norm1_g[0], n_ctx)
    qkv = _matmul(h.reshape(b * l, d), na_w_qkv[0].astype(BF16), jnp.zeros((3 * d,), F32), BF16)
    bias = _na_bias_table(na_rpb[0], rows, kr)
    o = _attention(qkv.reshape(b, l, 3 * d), bias, n_ctx, heads)
    wg_all, wl_all = _split_gu(moe_w_gu)
    w0 = _moe_weights(moe_router_w[0], moe_router_b[0], wg_all[0], wl_all[0], moe_b_gu[0], moe_w_dn[0], moe_b_dn[0])
    x1, hrow, logits = _proj(o, na_w_o[0].astype(BF16), jnp.zeros((d,), F32), xs0, tab0, norm2_g[0],
                             w0["rw"], w0["rb"], n_ctx, latent_only=False)
    x2, h = _moe(hrow, logits, x1, tab0, tab1, norm1_g[1], w0, n_ctx, "next")

    hf = h.reshape(b * l, d)
    xb = _matmul(hf, rg_w_x[0].astype(BF16), rg_b_x[0], F32).reshape(b, l, d)
    gy = _matmul(hf, rg_w_y[0].astype(BF16), rg_b_y[0], F32, act="gelu").reshape(b, l, d)
    u = _conv(xb, rg_conv_w[0], rg_conv_b[0], n_ctx)
    w_a, w_i = rg_w_a[0].astype(BF16), rg_w_i[0].astype(BF16)
    h_fwd = _scan(u, w_a, rg_b_a[0], w_i, rg_b_i[0], rg_lam[0], 0, n_ctx)
    hg = _scan(u, w_a, rg_b_a[0], w_i, rg_b_i[0], rg_lam[0], 1, n_ctx, h_fwd, gy)
    w1 = _moe_weights(moe_router_w[1], moe_router_b[1], wg_all[1], wl_all[1], moe_b_gu[1], moe_w_dn[1], moe_b_dn[1])
    x1, hrow, logits = _proj(hg, rg_w_out[0].astype(BF16), rg_b_out[0], x2, tab1, norm2_g[1],
                             w1["rw"], w1["rb"], n_ctx, latent_only=True)
    _, out = _moe(hrow, logits, x1, tab1, tab1, final_g, w1, 0, "final")
    return out
```

```python
import functools

import jax
import jax.numpy as jnp
import numpy as np
from jax import lax
from jax.experimental import pallas as pl
from jax.experimental.pallas import tpu as pltpu

F32 = jnp.float32
BF16 = jnp.bfloat16

LANES = 128
GRID_W = 64
TOP_K = 4
RG_C = 8.0
CONV_LEFT = 2
SWIGLU_ALPHA = 1.702
SWIGLU_LIMIT = 7.0
RMS_EPS = 1e-6
MOD_ROWS = 6
TAB_ROWS = 16
VMEM_LIMIT = 56 * 1024 * 1024


def _cparams(sem):
    return pltpu.CompilerParams(dimension_semantics=sem, vmem_limit_bytes=VMEM_LIMIT)


def _pick(n, pref, mult=8):
    for t in range(min(pref, n), 0, -1):
        if n % t == 0 and t % mult == 0:
            return t
    return n


def _dot(a, b):
    return jnp.dot(a, b, preferred_element_type=F32)


def _dot_nt(a, b):
    return lax.dot_general(a, b, (((1,), (1,)), ((), ())), preferred_element_type=F32)


def _split_bf16(x):
    hi = x.astype(BF16)
    lo = (x - hi.astype(F32)).astype(BF16)
    return hi, lo


def _dot3(a, w):
    a_hi, a_lo = _split_bf16(a)
    w_hi, w_lo = _split_bf16(w)
    return _dot(a_hi, w_hi) + _dot(a_lo, w_hi) + _dot(a_hi, w_lo)


def _norm_mod(x, g, shift, scale):
    y = x * lax.rsqrt(jnp.mean(x * x, axis=-1, keepdims=True) + RMS_EPS)
    return (y * g) * (1.0 + scale) + shift


def _to_token_rows(v):
    t, d = v.shape
    s_rows = d // LANES
    chunks = jnp.stack([v[:, s * LANES:(s + 1) * LANES] for s in range(s_rows)], axis=0)
    return pltpu.einshape("stl->tsl", chunks).reshape(t * s_rows, LANES)


def _from_token_rows(r, s_rows):
    return pltpu.einshape("tsl->stl", r.reshape(r.shape[0] // s_rows, s_rows, LANES))


def _tab_row(tab_ref, is_ctx, k):
    base = jnp.where(is_ctx, 0, MOD_ROWS)
    return tab_ref[0, pl.ds(base + k, 1), :]


def _mod_kernel(a_ref, w_ref, b_ref, o_ref):
    a = a_ref[...]
    a = a * jax.nn.sigmoid(a)
    o_ref[0] = _dot3(a, w_ref[0]) + b_ref[0]


def _modulation(cc, mod_w, mod_b):
    depth, d, n = mod_w.shape
    r = cc.shape[0]
    tn = _pick(n, 1024, LANES)
    return pl.pallas_call(
        _mod_kernel,
        grid=(depth, n // tn),
        in_specs=[
            pl.BlockSpec((r, d), lambda i, j: (0, 0)),
            pl.BlockSpec((1, d, tn), lambda i, j: (i, 0, j)),
            pl.BlockSpec((1, 1, tn), lambda i, j: (i, 0, j)),
        ],
        out_specs=pl.BlockSpec((1, r, tn), lambda i, j: (i, 0, j)),
        out_shape=jax.ShapeDtypeStruct((depth, r, n), F32),
        compiler_params=_cparams(("parallel", "parallel")),
        name="modulation",
    )(cc, mod_w, mod_b.reshape(depth, 1, n))


def _prenorm_kernel(x_ref, tab_ref, g_ref, o_ref, *, ctx_tiles):
    is_ctx = pl.program_id(1) < ctx_tiles
    h = _norm_mod(x_ref[0], g_ref[...], _tab_row(tab_ref, is_ctx, 0), _tab_row(tab_ref, is_ctx, 1))
    o_ref[0] = h.astype(BF16)


def _prenorm(x, tab, g, n_ctx):
    b, l, d = x.shape
    tm = n_ctx
    return pl.pallas_call(
        functools.partial(_prenorm_kernel, ctx_tiles=n_ctx // tm),
        grid=(b, l // tm),
        in_specs=[
            pl.BlockSpec((1, tm, d), lambda i, j: (i, j, 0)),
            pl.BlockSpec((1, TAB_ROWS, d), lambda i, j: (i, 0, 0)),
            pl.BlockSpec((1, d), lambda i, j: (0, 0)),
        ],
        out_specs=pl.BlockSpec((1, tm, d), lambda i, j: (i, j, 0)),
        out_shape=jax.ShapeDtypeStruct((b, l, d), BF16),
        compiler_params=_cparams(("parallel", "parallel")),
        name="prenorm",
    )(x, tab, g.reshape(1, d))


def _gelu_tanh(x):
    return 0.5 * x * (1.0 + jnp.tanh(np.sqrt(2.0 / np.pi) * (x + 0.044715 * (x * x * x))))


def _matmul_kernel(a_ref, w_ref, b_ref, o_ref, *, act):
    y = _dot(a_ref[...], w_ref[...]) + b_ref[...]
    if act == "gelu":
        y = _gelu_tanh(y)
    o_ref[...] = y.astype(o_ref.dtype)


def _matmul(a, w, bias, out_dtype, act=None):
    m, k = a.shape
    n = w.shape[1]
    tm = _pick(m, 1024)
    tn = _pick(n, 512, LANES)
    return pl.pallas_call(
        functools.partial(_matmul_kernel, act=act),
        grid=(m // tm, n // tn),
        in_specs=[
            pl.BlockSpec((tm, k), lambda i, j: (i, 0)),
            pl.BlockSpec((k, tn), lambda i, j: (0, j)),
            pl.BlockSpec((1, tn), lambda i, j: (0, j)),
        ],
        out_specs=pl.BlockSpec((tm, tn), lambda i, j: (i, j)),
        out_shape=jax.ShapeDtypeStruct((m, n), out_dtype),
        compiler_params=_cparams(("parallel", "parallel")),
        name="matmul_" + (act or "linear"),
    )(a, w, bias.reshape(1, n))


def _softmax_parts(parts):
    m = parts[0].max(axis=-1, keepdims=True)
    for s in parts[1:]:
        m = jnp.maximum(m, s.max(axis=-1, keepdims=True))
    ps = [jnp.exp(s - m) for s in parts]
    den = ps[0].sum(axis=-1, keepdims=True)
    for p in ps[1:]:
        den = den + p.sum(axis=-1, keepdims=True)
    return ps, den


ATTN_ROWS_PER_ITER = 8


def _attn_kernel(q_ref, k_ref, v_ref, bias_ref, o_ref, *, n_ctx, rows, kr, scale):
    kc = k_ref[0, 0:n_ctx, :]
    vc = v_ref[0, 0:n_ctx, :]
    (p,), den = _softmax_parts([_dot_nt(q_ref[0, 0:n_ctx, :], kc) * scale])
    o_ref[0, 0:n_ctx, :] = (_dot(p.astype(BF16), vc) / den).astype(BF16)

    group = next(g for g in (ATTN_ROWS_PER_ITER, 2, 1) if rows % g == 0)

    def row_group(i, carry):
        rr = [i * group + j for j in range(group)]
        rs = [jnp.clip(r - kr // 2, 0, rows - kr) for r in rr]
        q0 = [pl.multiple_of(n_ctx + r * GRID_W, GRID_W) for r in rr]
        k0 = [pl.multiple_of(n_ctx + s * GRID_W, GRID_W) for s in rs]
        scores = []
        for j in range(group):
            q = q_ref[0, pl.ds(q0[j], GRID_W), :]
            s_lat = _dot_nt(q, k_ref[0, pl.ds(k0[j], kr * GRID_W), :]) * scale + bias_ref[0, rr[j] - rs[j]]
            scores.append([s_lat, _dot_nt(q, kc) * scale])
        probs = [_softmax_parts(s) for s in scores]
        for j in range(group):
            (p_lat, p_ctx), den = probs[j]
            o = _dot(p_lat.astype(BF16), v_ref[0, pl.ds(k0[j], kr * GRID_W), :]) + _dot(p_ctx.astype(BF16), vc)
            o_ref[0, pl.ds(q0[j], GRID_W), :] = (o / den).astype(BF16)
        return carry

    lax.fori_loop(0, rows // group, row_group, 0)


def _na_bias_table(rpb, rows, kr):
    h, n_dr, n_dc = rpb.shape
    win_rows, win_cols = (n_dr + 1) // 2, (n_dc + 1) // 2
    col = np.arange(GRID_W)
    col_start = np.clip(col - win_cols // 2, 0, GRID_W - win_cols)
    col_mask = (col[None, :] >= col_start[:, None]) & (col[None, :] < col_start[:, None] + win_cols)
    dc_idx = np.clip(col[None, :] - col[:, None], 1 - win_cols, win_cols - 1) + win_cols - 1
    dr_idx = np.arange(kr)[None, :] - np.arange(kr)[:, None] + win_rows - 1
    t = rpb[:, dr_idx][:, :, :, dc_idx]
    t = jnp.where(col_mask[None, None, None], t.astype(F32), -jnp.inf)
    return t.transpose(0, 1, 3, 2, 4).reshape(h, kr, GRID_W, kr * GRID_W)


def _attention(qkv, bias, n_ctx, heads):
    b, l, d3 = qkv.shape
    d = d3 // 3
    dh = d // heads
    rows = (l - n_ctx) // GRID_W
    kr = bias.shape[1]
    kern = functools.partial(_attn_kernel, n_ctx=n_ctx, rows=rows, kr=kr, scale=dh ** -0.5)
    return pl.pallas_call(
        kern,
        grid=(heads, b),
        in_specs=[
            pl.BlockSpec((1, l, dh), lambda h, i: (i, 0, h)),
            pl.BlockSpec((1, l, dh), lambda h, i: (i, 0, heads + h)),
            pl.BlockSpec((1, l, dh), lambda h, i: (i, 0, 2 * heads + h)),
            pl.BlockSpec((1, kr, GRID_W, kr * GRID_W), lambda h, i: (h, 0, 0, 0)),
        ],
        out_specs=pl.BlockSpec((1, l, dh), lambda h, i: (i, 0, h)),
        out_shape=jax.ShapeDtypeStruct((b, l, d), BF16),
        compiler_params=_cparams(("parallel", "parallel")),
        name="na_attention",
    )(qkv, qkv, qkv, bias)


def _proj_kernel(a_ref, w_ref, b_ref, x_ref, tab_ref, g_ref, rw_ref, rb_ref,
                 x1_ref, hrow_ref, lg_ref, *, ctx_tiles, tile_off):
    is_ctx = pl.program_id(1) + tile_off < ctx_tiles
    y = _dot(a_ref[0], w_ref[...]) + b_ref[...]
    x1 = x_ref[0] + _tab_row(tab_ref, is_ctx, 2) * y
    x1_ref[0] = x1
    h2 = _norm_mod(x1, g_ref[...], _tab_row(tab_ref, is_ctx, 3), _tab_row(tab_ref, is_ctx, 4))
    lg_ref[...] = _dot3(h2, rw_ref[...]) + rb_ref[...]
    hrow_ref[...] = _to_token_rows(h2)


def _proj(a, w, bias, x, tab, g2, rw, rb, n_ctx, latent_only):
    b, l, d = x.shape
    tm = n_ctx
    off = n_ctx // tm if latent_only else 0
    nt = l // tm - off
    s_rows = d // LANES
    t = b * nt * tm
    kern = functools.partial(_proj_kernel, ctx_tiles=n_ctx // tm, tile_off=off)
    return pl.pallas_call(
        kern,
        grid=(b, nt),
        in_specs=[
            pl.BlockSpec((1, tm, d), lambda i, j: (i, j + off, 0)),
            pl.BlockSpec((d, d), lambda i, j: (0, 0)),
            pl.BlockSpec((1, d), lambda i, j: (0, 0)),
            pl.BlockSpec((1, tm, d), lambda i, j: (i, j + off, 0)),
            pl.BlockSpec((1, TAB_ROWS, d), lambda i, j: (i, 0, 0)),
            pl.BlockSpec((1, d), lambda i, j: (0, 0)),
            pl.BlockSpec((d, LANES), lambda i, j: (0, 0)),
            pl.BlockSpec((1, LANES), lambda i, j: (0, 0)),
        ],
        out_specs=[
            pl.BlockSpec((1, tm, d), lambda i, j: (i, j, 0)),
            pl.BlockSpec((tm * s_rows, LANES), lambda i, j: (i * nt + j, 0)),
            pl.BlockSpec((tm, LANES), lambda i, j: (i * nt + j, 0)),
        ],
        out_shape=[
            jax.ShapeDtypeStruct((b, nt * tm, d), F32),
            jax.ShapeDtypeStruct((t * s_rows, LANES), F32),
            jax.ShapeDtypeStruct((t, LANES), F32),
        ],
        compiler_params=_cparams(("parallel", "parallel")),
        name="mixer_proj",
    )(a, w, bias.reshape(1, d), x, tab, g2.reshape(1, d), rw, rb)


def _route_kernel(lg_ref, e_ref, g_ref, r_ref, cnt_ref, carry_ref, *, n_exp):
    @pl.when(pl.program_id(0) == 0)
    def _():
        carry_ref[...] = jnp.zeros_like(carry_ref)

    lg = lg_ref[...]
    tm = lg.shape[0]
    lane = lax.broadcasted_iota(jnp.int32, lg.shape, 1).astype(F32)
    cur = jnp.where(lane < n_exp, lg, -jnp.inf)
    multi = jnp.zeros(lg.shape, F32)
    vals, idxs = [], []
    for _ in range(TOP_K):
        m = cur.max(axis=-1, keepdims=True)
        idx = jnp.where(cur == m, lane, float(LANES)).min(axis=-1, keepdims=True)
        sel = lane == idx
        multi = jnp.where(sel, 1.0, multi)
        cur = jnp.where(sel, -jnp.inf, cur)
        vals.append(m)
        idxs.append(idx)
    exps = [jnp.exp(v - vals[0]) for v in vals]
    den = exps[0]
    for e in exps[1:]:
        den = den + e
    tri = (lax.broadcasted_iota(jnp.int32, (tm, tm), 0) > lax.broadcasted_iota(jnp.int32, (tm, tm), 1))
    pref = _dot(jnp.where(tri, 1.0, 0.0).astype(BF16), multi.astype(BF16))
    tot = carry_ref[...] + pref
    e_out = jnp.zeros(lg.shape, F32)
    g_out = jnp.zeros(lg.shape, F32)
    r_out = jnp.zeros(lg.shape, F32)
    for k in range(TOP_K):
        rank_k = jnp.where(lane == idxs[k], tot, 0.0).sum(axis=-1, keepdims=True)
        e_out = jnp.where(lane == k, idxs[k], e_out)
        g_out = jnp.where(lane == k, exps[k] / den, g_out)
        r_out = jnp.where(lane == k, rank_k, r_out)
    e_ref[...] = e_out.astype(jnp.int32)
    g_ref[...] = g_out
    r_ref[...] = r_out.astype(jnp.int32)
    carry_ref[...] = carry_ref[...] + multi.sum(axis=0, keepdims=True)
    cnt_ref[...] = carry_ref[...]


def _route(logits, n_exp):
    t = logits.shape[0]
    tm = _pick(t, 256)
    spec = pl.BlockSpec((tm, LANES), lambda i: (i, 0))
    return pl.pallas_call(
        functools.partial(_route_kernel, n_exp=n_exp),
        grid=(t // tm,),
        in_specs=[spec],
        out_specs=[spec, spec, spec, pl.BlockSpec((1, LANES), lambda i: (0, 0))],
        out_shape=[
            jax.ShapeDtypeStruct((t, LANES), jnp.int32),
            jax.ShapeDtypeStruct((t, LANES), F32),
            jax.ShapeDtypeStruct((t, LANES), jnp.int32),
            jax.ShapeDtypeStruct((1, LANES), F32),
        ],
        scratch_shapes=[pltpu.VMEM((1, LANES), F32)],
        compiler_params=_cparams(("arbitrary",)),
        name="router",
    )(logits)


def _dispatch_kernel(dest_ref, h_ref, xs_ref, sem, *, td, s_rows):
    def issue(t, carry):
        src = h_ref.at[pl.ds(pl.multiple_of(t * s_rows, s_rows), s_rows), :]
        for k in range(TOP_K):
            d = dest_ref[0, 0, t * TOP_K + k]
            dst = xs_ref.at[pl.ds(pl.multiple_of(d * s_rows, s_rows), s_rows), :]
            pltpu.make_async_copy(src, dst, sem).start()
        return carry

    lax.fori_loop(0, td, issue, 0)
    for _ in range(TOP_K):
        pltpu.make_async_copy(h_ref, xs_ref.at[pl.ds(0, td * s_rows), :], sem).wait()


def _dispatch(hrow, dest, s_rows):
    t = dest.shape[0] // TOP_K
    td = _pick(t, 256)
    nt = t // td
    return pl.pallas_call(
        functools.partial(_dispatch_kernel, td=td, s_rows=s_rows),
        grid=(nt,),
        in_specs=[
            pl.BlockSpec((1, 1, td * TOP_K), lambda i: (i, 0, 0), memory_space=pltpu.SMEM),
            pl.BlockSpec((td * s_rows, LANES), lambda i: (i, 0)),
        ],
        out_specs=pl.BlockSpec(memory_space=pl.ANY),
        out_shape=jax.ShapeDtypeStruct((t * TOP_K * s_rows, LANES), F32),
        scratch_shapes=[pltpu.SemaphoreType.DMA(())],
        compiler_params=pltpu.CompilerParams(dimension_semantics=("arbitrary",), vmem_limit_bytes=VMEM_LIMIT,
                                             has_side_effects=True),
        name="moe_dispatch",
    )(dest.reshape(nt, 1, td * TOP_K), hrow)


def _gmm_kernel(it_tile, it_e, it_lo, it_hi, it_first, it_valid,
                xs_ref, wg_ref, wl_ref, bg_ref, bl_ref, wd_ref, bd_ref, ys_ref, xb_ref, acc_ref,
                *, tg, s_rows, n_chunks):
    m = pl.program_id(0)
    c = pl.program_id(1)
    tsub = _pick(tg, 128)

    @pl.when(it_valid[m] == 1)
    def _():
        @pl.when(c == 0)
        def _():
            for t0 in range(0, tg, tsub):
                xt = _from_token_rows(xs_ref[pl.ds(t0 * s_rows, tsub * s_rows), :], s_rows)
                for s in range(s_rows):
                    xb_ref[pl.ds(t0, tsub), s * LANES:(s + 1) * LANES] = xt[s].astype(BF16)

        x = xb_ref[...]
        glu = jnp.minimum(_dot(x, wg_ref[0]) + bg_ref[0], SWIGLU_LIMIT)
        lin = jnp.clip(_dot(x, wl_ref[0]) + bl_ref[0], -SWIGLU_LIMIT, SWIGLU_LIMIT)
        act = glu * jax.nn.sigmoid(SWIGLU_ALPHA * glu) * (lin + 1.0)
        part = _dot(act.astype(BF16), wd_ref[0])

        @pl.when(c == 0)
        def _():
            acc_ref[...] = part

        @pl.when(c > 0)
        def _():
            acc_ref[...] += part

        @pl.when(c == n_chunks - 1)
        def _():
            for t0 in range(0, tg, tsub):
                rows = pl.ds(t0 * s_rows, tsub * s_rows)
                y = _to_token_rows(acc_ref[pl.ds(t0, tsub), :] + bd_ref[0])
                row = t0 * s_rows + lax.broadcasted_iota(jnp.int32, (tsub * s_rows, 1), 0)
                mine = (row >= it_lo[m] * s_rows) & (row < it_hi[m] * s_rows)

                @pl.when(it_first[m] == 1)
                def _():
                    ys_ref[rows, :] = jnp.where(mine, y, 0.0)

                @pl.when(it_first[m] == 0)
                def _():
                    ys_ref[rows, :] = jnp.where(mine, y, ys_ref[rows, :])


def _gmm_items(counts, n_tiles, tg, max_items):
    n_exp = counts.shape[0]
    ends = jnp.cumsum(counts)
    starts = ends - counts
    def count_le(sorted_vals, q):
        return jnp.sum(sorted_vals[None, :] <= q[:, None], axis=1, dtype=jnp.int32)

    tile0 = jnp.arange(n_tiles, dtype=jnp.int32) * tg
    e_lo = jnp.minimum(count_le(ends, tile0), n_exp - 1)
    e_hi = jnp.minimum(count_le(ends, tile0 + tg - 1), n_exp - 1)
    n_items = e_hi - e_lo + 1
    item_end = jnp.cumsum(n_items)
    item_start = item_end - n_items
    total = item_end[-1]
    m = jnp.arange(max_items, dtype=jnp.int32)
    valid = m < total
    tile = jnp.minimum(count_le(item_end, m), n_tiles - 1)
    e = jnp.where(valid, e_lo[tile] + (m - item_start[tile]), e_hi[n_tiles - 1]).astype(jnp.int32)
    lo = jnp.clip(starts[e] - tile * tg, 0, tg).astype(jnp.int32)
    hi = jnp.clip(ends[e] - tile * tg, 0, tg).astype(jnp.int32)
    first = (m == item_start[tile]).astype(jnp.int32)
    return tile, e, lo, hi, first, valid.astype(jnp.int32)


GMM_ROW_TILE = 512
GMM_F_CHUNK = 1024


def _gmm(xs, counts, w, layer, s_rows):
    n_exp = counts.shape[0]
    d, f = w["wg"].shape[1:]
    p = xs.shape[0] // s_rows
    tg = _pick(p, GMM_ROW_TILE)
    fk = _pick(f, GMM_F_CHUNK, LANES)
    n_tiles, n_chunks = p // tg, f // fk
    max_items = n_tiles + n_exp - 1
    tile, e, lo, hi, first, valid = _gmm_items(counts, n_tiles, tg, max_items)
    items = (tile, e + layer * n_exp, lo, hi, first, valid)

    def wspec(block, pos):
        def index(m, c, t, e, lo, hi, fi, va):
            idx = [e[m], 0, 0]
            if pos is not None:
                idx[pos] = jnp.where(va[m] == 1, c, n_chunks - 1)
            return tuple(idx)
        return pl.BlockSpec(block, index)

    rows_spec = pl.BlockSpec((tg * s_rows, LANES), lambda m, c, t, e, lo, hi, fi, va: (t[m], 0))
    grid_spec = pltpu.PrefetchScalarGridSpec(
        num_scalar_prefetch=6,
        grid=(max_items, n_chunks),
        in_specs=[
            rows_spec,
            wspec((1, d, fk), 2),
            wspec((1, d, fk), 2),
            wspec((1, 1, fk), 2),
            wspec((1, 1, fk), 2),
            wspec((1, fk, d), 1),
            wspec((1, 1, d), None),
        ],
        out_specs=rows_spec,
        scratch_shapes=[pltpu.VMEM((tg, d), BF16), pltpu.VMEM((tg, d), F32)],
    )
    return pl.pallas_call(
        functools.partial(_gmm_kernel, tg=tg, s_rows=s_rows, n_chunks=n_chunks),
        grid_spec=grid_spec,
        out_shape=jax.ShapeDtypeStruct(xs.shape, F32),
        compiler_params=_cparams(("arbitrary", "arbitrary")),
        name="moe_experts",
    )(*items, xs, w["wg"], w["wl"], w["bg"], w["bl"], w["wd"], w["bd"])


def _combine_kernel(dest_ref, dest_next_ref, ys_ref, gate_ref, x_ref, tab_ref, tabn_ref, g_ref,
                    x2_ref, h_ref, buf_ref, moe_ref, sem, *, tc, s_rows, ctx_tiles, n_steps, mode):
    step = pl.program_id(0) * pl.num_programs(1) + pl.program_id(1)
    slot = step % 2

    def gather(dref, slot_idx):
        def issue(t, carry):
            for k in range(TOP_K):
                d = dref[0, 0, t * TOP_K + k]
                src = ys_ref.at[pl.ds(pl.multiple_of(d * s_rows, s_rows), s_rows), :]
                dst = buf_ref.at[slot_idx, pl.ds(pl.multiple_of((k * tc + t) * s_rows, s_rows), s_rows), :]
                pltpu.make_async_copy(src, dst, sem.at[slot_idx]).start()
            return carry

        lax.fori_loop(0, tc, issue, 0)

    @pl.when(step == 0)
    def _():
        gather(dest_ref, 0)

    @pl.when(step + 1 < n_steps)
    def _():
        gather(dest_next_ref, 1 - slot)

    pltpu.make_async_copy(buf_ref.at[slot], buf_ref.at[slot], sem.at[slot]).wait()

    gates = gate_ref[...]
    for k in range(TOP_K):
        chunks = _from_token_rows(buf_ref[slot, pl.ds(k * tc * s_rows, tc * s_rows), :], s_rows)
        for s in range(s_rows):
            term = chunks[s] * gates[:, k:k + 1]
            if k == 0:
                moe_ref[:, s * LANES:(s + 1) * LANES] = term
            else:
                moe_ref[:, s * LANES:(s + 1) * LANES] += term

    is_ctx = pl.program_id(1) < ctx_tiles
    x2 = x_ref[0] + _tab_row(tab_ref, is_ctx, 5) * moe_ref[...]
    x2_ref[0] = x2
    if mode == "next":
        h = _norm_mod(x2, g_ref[...], _tab_row(tabn_ref, is_ctx, 0), _tab_row(tabn_ref, is_ctx, 1))
        h_ref[0] = h.astype(h_ref.dtype)
    else:
        y = x2 * lax.rsqrt(jnp.mean(x2 * x2, axis=-1, keepdims=True) + RMS_EPS)
        h_ref[0] = (y * g_ref[...]).astype(h_ref.dtype)


def _combine(ys, dest, gates, x1, tab, tab_next, g_next, n_ctx_rows, s_rows, mode):
    b, l, d = x1.shape
    tc = _pick(min(l, 128) if n_ctx_rows == 0 else n_ctx_rows, 128)
    nt = l // tc
    n_steps = b * nt
    dest3 = dest.reshape(n_steps, 1, tc * TOP_K)
    kern = functools.partial(_combine_kernel, tc=tc, s_rows=s_rows, ctx_tiles=n_ctx_rows // tc,
                             n_steps=n_steps, mode=mode)
    out_dtype = BF16 if mode == "next" else F32
    return pl.pallas_call(
        kern,
        grid=(b, nt),
        in_specs=[
            pl.BlockSpec((1, 1, tc * TOP_K), lambda i, j: (i * nt + j, 0, 0), memory_space=pltpu.SMEM),
            pl.BlockSpec((1, 1, tc * TOP_K), lambda i, j: (jnp.minimum(i * nt + j + 1, n_steps - 1), 0, 0),
                         memory_space=pltpu.SMEM),
            pl.BlockSpec(memory_space=pl.ANY),
            pl.BlockSpec((tc, LANES), lambda i, j: (i * nt + j, 0)),
            pl.BlockSpec((1, tc, d), lambda i, j: (i, j, 0)),
            pl.BlockSpec((1, TAB_ROWS, d), lambda i, j: (i, 0, 0)),
            pl.BlockSpec((1, TAB_ROWS, d), lambda i, j: (i, 0, 0)),
            pl.BlockSpec((1, d), lambda i, j: (0, 0)),
        ],
        out_specs=[
            pl.BlockSpec((1, tc, d), lambda i, j: (i, j, 0)),
            pl.BlockSpec((1, tc, d), lambda i, j: (i, j, 0)),
        ],
        out_shape=[
            jax.ShapeDtypeStruct((b, l, d), F32),
            jax.ShapeDtypeStruct((b, l, d), out_dtype),
        ],
        scratch_shapes=[
            pltpu.VMEM((2, tc * TOP_K * s_rows, LANES), F32),
            pltpu.VMEM((tc, d), F32),
            pltpu.SemaphoreType.DMA((2,)),
        ],
        compiler_params=_cparams(("arbitrary", "arbitrary")),
        name="moe_combine_" + mode,
    )(dest3, dest3, ys, gates, x1, tab, tab_next, g_next.reshape(1, d))


def _moe(hrow, logits, x1, tab, tab_next, g_next, w, layer, n_exp, n_ctx_rows, mode):
    s_rows = x1.shape[-1] // LANES
    e_pad, g_pad, r_pad, cnt = _route(logits, n_exp)
    top_e, rank = e_pad[:, :TOP_K], r_pad[:, :TOP_K]
    counts = cnt[0, :n_exp].astype(jnp.int32)
    starts = jnp.cumsum(counts) - counts
    dest = (starts[top_e] + rank).reshape(-1)
    xs = _dispatch(hrow, dest, s_rows)
    ys = _gmm(xs, counts, w, layer, s_rows)
    return _combine(ys, dest, g_pad, x1, tab, tab_next, g_next, n_ctx_rows, s_rows, mode)


def _conv_kernel(x_ref, w_ref, b_ref, o_ref, *, n_ctx):
    l = x_ref.shape[1]
    w = w_ref[...]
    for s0, n in ((0, n_ctx), (n_ctx, l - n_ctx)):
        x = x_ref[0, s0:s0 + n, :]
        row = lax.broadcasted_iota(jnp.int32, (n, 1), 0)
        acc = x * w[CONV_LEFT:CONV_LEFT + 1]
        for j in range(w.shape[0]):
            off = j - CONV_LEFT
            if off == 0:
                continue
            shifted = pltpu.roll(x, (-off) % n, 0)
            ok = (row + off >= 0) & (row + off < n)
            acc = acc + jnp.where(ok, shifted, 0.0) * w[j:j + 1]
        o_ref[0, s0:s0 + n, :] = acc + b_ref[...]


def _conv(xb, conv_w, conv_b, n_ctx):
    b, l, d = xb.shape
    dt = _pick(d, 256, LANES)
    return pl.pallas_call(
        functools.partial(_conv_kernel, n_ctx=n_ctx),
        grid=(b, d // dt),
        in_specs=[
            pl.BlockSpec((1, l, dt), lambda i, j: (i, 0, j)),
            pl.BlockSpec((conv_w.shape[0], dt), lambda i, j: (0, j)),
            pl.BlockSpec((1, dt), lambda i, j: (0, j)),
        ],
        out_specs=pl.BlockSpec((1, l, dt), lambda i, j: (i, 0, j)),
        out_shape=jax.ShapeDtypeStruct((b, l, d), F32),
        compiler_params=_cparams(("parallel", "parallel")),
        name="rg_conv",
    )(xb, conv_w, conv_b.reshape(1, d))


SCAN_PAD = 8
SCAN_UNROLL = 8


def _scan_kernel(*refs, nb, tc, nh, reverse):
    n_in = 8 if reverse else 6
    u_ref, wa_ref, ba_ref, wi_ref, bi_ref, lam_ref = refs[:6]
    o_ref = refs[n_in]
    scratch = refs[n_in + 1:]
    a_s, x_s, h_s = scratch[0:nh], scratch[nh:2 * nh], scratch[2 * nh:3 * nh]
    carry = scratch[3 * nh]
    ts = tc + SCAN_PAD

    @pl.when(pl.program_id(1) == 0)
    def _():
        carry[...] = jnp.zeros_like(carry)

    neg = -lam_ref[0]
    softplus = jnp.maximum(neg, 0.0) + jnp.log1p(jnp.exp(-jnp.abs(neg)))
    for bi in range(nb):
        u = u_ref[bi]
        ub = u.astype(BF16)
        r = jax.nn.sigmoid(_dot(ub, wa_ref[0, 0]) + ba_ref[0, 0])
        i = jax.nn.sigmoid(_dot(ub, wi_ref[0, 0]) + bi_ref[0, 0])
        log_a = (-RG_C) * r * softplus
        a = jnp.exp(log_a)
        xin = jnp.sqrt(1.0 - a * a) * (i * u)
        for p in range(nh):
            a_s[p][pl.ds(bi * ts, tc), :] = a[:, p * LANES:(p + 1) * LANES]
            x_s[p][pl.ds(bi * ts, tc), :] = xin[:, p * LANES:(p + 1) * LANES]

    def block(j, hs):
        hs = list(hs)
        for q in range(SCAN_UNROLL):
            t = j * SCAN_UNROLL + q
            if reverse:
                t = tc - 1 - t
            for p in range(nh):
                hs[p] = a_s[p][pl.ds(t, nb, stride=ts), :] * hs[p] + x_s[p][pl.ds(t, nb, stride=ts), :]
                h_s[p][pl.ds(t, nb, stride=ts), :] = hs[p]
        return tuple(hs)

    h0 = tuple(carry[:, p * LANES:(p + 1) * LANES] for p in range(nh))
    hs = lax.fori_loop(0, tc // SCAN_UNROLL, block, h0)
    for p in range(nh):
        carry[:, p * LANES:(p + 1) * LANES] = hs[p]
    for bi in range(nb):
        for p in range(nh):
            h = h_s[p][pl.ds(bi * ts, tc), :]
            cols = slice(p * LANES, (p + 1) * LANES)
            if reverse:
                hf_ref, gy_ref = refs[6], refs[7]
                o_ref[bi, :, cols] = ((hf_ref[bi, :, cols] + h) * gy_ref[bi, :, cols]).astype(o_ref.dtype)
            else:
                o_ref[bi, :, cols] = h


def _scan(u, w_a, b_a, w_i, b_i, lam, dirn, n_ctx, h_fwd=None, gy=None):
    b, l, d = u.shape
    n_blk, w = w_a.shape[1], w_a.shape[2]
    tc = _pick(n_ctx, 128)
    nt, nc = l // tc, n_ctx // tc
    reverse = dirn == 1

    def chunk(j):
        if not reverse:
            return j
        return jnp.where(j < nc, nc - 1 - j, nt - 1 - (j - nc))

    blk = pl.BlockSpec((b, tc, w), lambda g, j: (0, chunk(j), g))
    in_specs = [
        blk,
        pl.BlockSpec((1, 1, w, w), lambda g, j: (dirn, g, 0, 0)),
        pl.BlockSpec((1, 1, 1, w), lambda g, j: (dirn, g, 0, 0)),
        pl.BlockSpec((1, 1, w, w), lambda g, j: (dirn, g, 0, 0)),
        pl.BlockSpec((1, 1, 1, w), lambda g, j: (dirn, g, 0, 0)),
        pl.BlockSpec((1, 1, w), lambda g, j: (dirn, 0, g)),
    ]
    args = [u, w_a, b_a.reshape(2, n_blk, 1, w), w_i, b_i.reshape(2, n_blk, 1, w), lam.reshape(2, 1, d)]
    if reverse:
        in_specs += [blk, blk]
        args += [h_fwd, gy]
    ts = tc + SCAN_PAD
    return pl.pallas_call(
        functools.partial(_scan_kernel, nb=b, tc=tc, nh=w // LANES, reverse=reverse),
        grid=(n_blk, nt),
        in_specs=in_specs,
        out_specs=blk,
        out_shape=jax.ShapeDtypeStruct((b, l, d), BF16 if reverse else F32),
        scratch_shapes=[pltpu.VMEM((b * ts, LANES), F32)] * (3 * (w // LANES)) + [pltpu.VMEM((b, w), F32)],
        compiler_params=_cparams(("parallel", "arbitrary")),
        name="rg_scan_" + ("rev" if reverse else "fwd"),
    )(*args)


def _mod_tables(mod_out, b, d):
    tabs = []
    for i in range(mod_out.shape[0]):
        ml = mod_out[i, :b].reshape(b, MOD_ROWS, d)
        mc = jnp.broadcast_to(mod_out[i, b].reshape(1, MOD_ROWS, d), (b, MOD_ROWS, d))
        pad = jnp.zeros((b, TAB_ROWS - 2 * MOD_ROWS, d), F32)
        tabs.append(jnp.concatenate([mc, ml, pad], axis=1))
    return tabs


def _split_gu_kernel(w_ref, perm_ref, wg_ref, wl_ref):
    for j in range(w_ref.shape[1] // (2 * LANES)):
        y = _dot(w_ref[:, 2 * LANES * j:2 * LANES * (j + 1)].astype(BF16), perm_ref[...])
        wg_ref[:, LANES * j:LANES * (j + 1)] = y[:, :LANES].astype(BF16)
        wl_ref[:, LANES * j:LANES * (j + 1)] = y[:, LANES:].astype(BF16)


def _split_gu(w_gu):
    lead, n2 = w_gu.shape[:-1], w_gu.shape[-1]
    rows = int(np.prod(lead))
    tr = _pick(rows, 512)
    src = np.concatenate([np.arange(0, 2 * LANES, 2), np.arange(1, 2 * LANES, 2)])
    perm = jnp.asarray(np.arange(2 * LANES)[:, None] == src[None, :], BF16)
    out_spec = pl.BlockSpec((tr, n2 // 2), lambda i: (i, 0))
    wg, wl = pl.pallas_call(
        _split_gu_kernel,
        grid=(rows // tr,),
        in_specs=[pl.BlockSpec((tr, n2), lambda i: (i, 0)), pl.BlockSpec((2 * LANES, 2 * LANES), lambda i: (0, 0))],
        out_specs=[out_spec, out_spec],
        out_shape=[jax.ShapeDtypeStruct((rows, n2 // 2), BF16)] * 2,
        compiler_params=_cparams(("parallel",)),
        name="split_gu_weights",
    )(w_gu.reshape(rows, n2), perm)
    return wg.reshape(*lead, n2 // 2), wl.reshape(*lead, n2 // 2)


def _moe_weights(router_w, router_b, w_gu, b_gu, w_dn, b_dn):
    depth, d, n_exp = router_w.shape
    f = w_dn.shape[2]
    wg, wl = _split_gu(w_gu)
    return {
        "rw": jnp.pad(router_w, ((0, 0), (0, 0), (0, LANES - n_exp))),
        "rb": jnp.pad(router_b, ((0, 0), (0, LANES - n_exp))).reshape(depth, 1, LANES),
        "wg": wg.reshape(depth * n_exp, d, f),
        "wl": wl.reshape(depth * n_exp, d, f),
        "bg": b_gu[..., 0::2].reshape(depth * n_exp, 1, f),
        "bl": b_gu[..., 1::2].reshape(depth * n_exp, 1, f),
        "wd": w_dn.astype(BF16).reshape(depth * n_exp, f, d),
        "bd": b_dn.reshape(depth * n_exp, 1, d),
    }


def kernel(x, c, ctx, c_ctx, mod_w, mod_b, norm1_g, norm2_g, final_g, na_w_qkv, na_w_o, na_rpb, rg_w_y, rg_b_y,
           rg_w_x, rg_b_x, rg_conv_w, rg_conv_b, rg_w_a, rg_b_a, rg_w_i, rg_b_i, rg_lam, rg_w_out, rg_b_out,
           moe_router_w, moe_router_b, moe_w_gu, moe_b_gu, moe_w_dn, moe_b_dn):
    b, n_lat, d = x.shape
    n_ctx = ctx.shape[1]
    l = n_ctx + n_lat
    heads = na_rpb.shape[1]
    rows = n_lat // GRID_W
    kr = min((na_rpb.shape[2] + 1) // 2, rows)
    assert mod_w.shape[0] == 2 and n_lat % n_ctx == 0 and d % LANES == 0

    mod_rows = -(-(b + 1) // 8) * 8
    cc = jnp.concatenate([c, c_ctx[None, :], jnp.zeros((mod_rows - b - 1, d), F32)], axis=0)
    tab0, tab1 = _mod_tables(_modulation(cc, mod_w, mod_b), b, d)

    xs0 = jnp.concatenate([ctx, x], axis=1)

    h = _prenorm(xs0, tab0, norm1_g[0], n_ctx)
    qkv = _matmul(h.reshape(b * l, d), na_w_qkv[0].astype(BF16), jnp.zeros((3 * d,), F32), BF16)
    bias = _na_bias_table(na_rpb[0], rows, kr)
    o = _attention(qkv.reshape(b, l, 3 * d), bias, n_ctx, heads)
    n_exp = moe_router_w.shape[2]
    w = _moe_weights(moe_router_w, moe_router_b, moe_w_gu, moe_b_gu, moe_w_dn, moe_b_dn)
    x1, hrow, logits = _proj(o, na_w_o[0].astype(BF16), jnp.zeros((d,), F32), xs0, tab0, norm2_g[0],
                             w["rw"][0], w["rb"][0], n_ctx, latent_only=False)
    x2, h = _moe(hrow, logits, x1, tab0, tab1, norm1_g[1], w, 0, n_exp, n_ctx, "next")

    hf = h.reshape(b * l, d)
    xb = _matmul(hf, rg_w_x[0].astype(BF16), rg_b_x[0], F32).reshape(b, l, d)
    gy = _matmul(hf, rg_w_y[0].astype(BF16), rg_b_y[0], F32, act="gelu").reshape(b, l, d)
    u = _conv(xb, rg_conv_w[0], rg_conv_b[0], n_ctx)
    w_a, w_i = rg_w_a[0].astype(BF16), rg_w_i[0].astype(BF16)
    h_fwd = _scan(u, w_a, rg_b_a[0], w_i, rg_b_i[0], rg_lam[0], 0, n_ctx)
    hg = _scan(u, w_a, rg_b_a[0], w_i, rg_b_i[0], rg_lam[0], 1, n_ctx, h_fwd, gy)
    x1, hrow, logits = _proj(hg, rg_w_out[0].astype(BF16), rg_b_out[0], x2, tab1, norm2_g[1],
                             w["rw"][1], w["rb"][1], n_ctx, latent_only=True)
    _, out = _moe(hrow, logits, x1, tab1, tab1, final_g, w, 1, n_exp, 0, "final")
    return out
```

```python
import functools

import jax
import jax.numpy as jnp
import numpy as np
from jax import lax
from jax.experimental import pallas as pl
from jax.experimental.pallas import tpu as pltpu

F32 = jnp.float32
BF16 = jnp.bfloat16

LANES = 128
GRID_W = 64
TOP_K = 4
RG_C = 8.0
CONV_LEFT = 2
SWIGLU_ALPHA = 1.702
SWIGLU_LIMIT = 7.0
RMS_EPS = 1e-6
MOD_ROWS = 6
TAB_ROWS = 16
VMEM_LIMIT = 56 * 1024 * 1024


def _cparams(sem):
    return pltpu.CompilerParams(dimension_semantics=sem, vmem_limit_bytes=VMEM_LIMIT)


def _pick(n, pref, mult=8):
    for t in range(min(pref, n), 0, -1):
        if n % t == 0 and t % mult == 0:
            return t
    return n


def _dot(a, b):
    return jnp.dot(a, b, preferred_element_type=F32)


def _dot_nt(a, b):
    return lax.dot_general(a, b, (((1,), (1,)), ((), ())), preferred_element_type=F32)


def _split_bf16(x):
    hi = x.astype(BF16)
    lo = (x - hi.astype(F32)).astype(BF16)
    return hi, lo


def _dot3(a, w):
    a_hi, a_lo = _split_bf16(a)
    w_hi, w_lo = _split_bf16(w)
    return _dot(a_hi, w_hi) + _dot(a_lo, w_hi) + _dot(a_hi, w_lo)


def _norm_mod(x, g, shift, scale):
    y = x * lax.rsqrt(jnp.mean(x * x, axis=-1, keepdims=True) + RMS_EPS)
    return (y * g) * (1.0 + scale) + shift


def _to_token_rows(v):
    t, d = v.shape
    s_rows = d // LANES
    chunks = jnp.stack([v[:, s * LANES:(s + 1) * LANES] for s in range(s_rows)], axis=0)
    return pltpu.einshape("stl->tsl", chunks).reshape(t * s_rows, LANES)


def _from_token_rows(r, s_rows):
    return pltpu.einshape("tsl->stl", r.reshape(r.shape[0] // s_rows, s_rows, LANES))


def _tab_row(tab_ref, is_ctx, k):
    base = jnp.where(is_ctx, 0, MOD_ROWS)
    return tab_ref[0, pl.ds(base + k, 1), :]


def _mod_kernel(a_ref, w_ref, b_ref, o_ref):
    a = a_ref[...]
    a = a * jax.nn.sigmoid(a)
    o_ref[0] = _dot3(a, w_ref[0]) + b_ref[0]


def _modulation(cc, mod_w, mod_b):
    depth, d, n = mod_w.shape
    r = cc.shape[0]
    tn = _pick(n, 1024, LANES)
    return pl.pallas_call(
        _mod_kernel,
        grid=(depth, n // tn),
        in_specs=[
            pl.BlockSpec((r, d), lambda i, j: (0, 0)),
            pl.BlockSpec((1, d, tn), lambda i, j: (i, 0, j)),
            pl.BlockSpec((1, 1, tn), lambda i, j: (i, 0, j)),
        ],
        out_specs=pl.BlockSpec((1, r, tn), lambda i, j: (i, 0, j)),
        out_shape=jax.ShapeDtypeStruct((depth, r, n), F32),
        compiler_params=_cparams(("parallel", "parallel")),
        name="modulation",
    )(cc, mod_w, mod_b.reshape(depth, 1, n))


def _prenorm_kernel(x_ref, tab_ref, g_ref, o_ref, *, ctx_tiles):
    is_ctx = pl.program_id(1) < ctx_tiles
    h = _norm_mod(x_ref[0], g_ref[...], _tab_row(tab_ref, is_ctx, 0), _tab_row(tab_ref, is_ctx, 1))
    o_ref[0] = h.astype(BF16)


def _prenorm(x, tab, g, n_ctx):
    b, l, d = x.shape
    tm = n_ctx
    return pl.pallas_call(
        functools.partial(_prenorm_kernel, ctx_tiles=n_ctx // tm),
        grid=(b, l // tm),
        in_specs=[
            pl.BlockSpec((1, tm, d), lambda i, j: (i, j, 0)),
            pl.BlockSpec((1, TAB_ROWS, d), lambda i, j: (i, 0, 0)),
            pl.BlockSpec((1, d), lambda i, j: (0, 0)),
        ],
        out_specs=pl.BlockSpec((1, tm, d), lambda i, j: (i, j, 0)),
        out_shape=jax.ShapeDtypeStruct((b, l, d), BF16),
        compiler_params=_cparams(("parallel", "parallel")),
        name="prenorm",
    )(x, tab, g.reshape(1, d))


def _gelu_tanh(x):
    return 0.5 * x * (1.0 + jnp.tanh(np.sqrt(2.0 / np.pi) * (x + 0.044715 * (x * x * x))))


def _matmul_kernel(a_ref, w_ref, b_ref, o_ref, *, act):
    y = _dot(a_ref[...], w_ref[...]) + b_ref[...]
    if act == "gelu":
        y = _gelu_tanh(y)
    o_ref[...] = y.astype(o_ref.dtype)


def _matmul(a, w, bias, out_dtype, act=None):
    m, k = a.shape
    n = w.shape[1]
    tm = _pick(m, 1024)
    tn = _pick(n, 512, LANES)
    return pl.pallas_call(
        functools.partial(_matmul_kernel, act=act),
        grid=(m // tm, n // tn),
        in_specs=[
            pl.BlockSpec((tm, k), lambda i, j: (i, 0)),
            pl.BlockSpec((k, tn), lambda i, j: (0, j)),
            pl.BlockSpec((1, tn), lambda i, j: (0, j)),
        ],
        out_specs=pl.BlockSpec((tm, tn), lambda i, j: (i, j)),
        out_shape=jax.ShapeDtypeStruct((m, n), out_dtype),
        compiler_params=_cparams(("parallel", "parallel")),
        name="matmul_" + (act or "linear"),
    )(a, w, bias.reshape(1, n))


def _softmax_parts(parts):
    m = parts[0].max(axis=-1, keepdims=True)
    for s in parts[1:]:
        m = jnp.maximum(m, s.max(axis=-1, keepdims=True))
    ps = [jnp.exp(s - m) for s in parts]
    den = ps[0].sum(axis=-1, keepdims=True)
    for p in ps[1:]:
        den = den + p.sum(axis=-1, keepdims=True)
    return ps, den


ATTN_ROWS_PER_ITER = 8


def _attn_kernel(q_ref, k_ref, v_ref, bias_ref, o_ref, *, n_ctx, rows, kr, scale):
    kc = k_ref[0, 0:n_ctx, :]
    vc = v_ref[0, 0:n_ctx, :]
    (p,), den = _softmax_parts([_dot_nt(q_ref[0, 0:n_ctx, :], kc) * scale])
    o_ref[0, 0:n_ctx, :] = (_dot(p.astype(BF16), vc) / den).astype(BF16)

    group = next(g for g in (ATTN_ROWS_PER_ITER, 2, 1) if rows % g == 0)

    def row_group(i, carry):
        rr = [i * group + j for j in range(group)]
        rs = [jnp.clip(r - kr // 2, 0, rows - kr) for r in rr]
        q0 = [pl.multiple_of(n_ctx + r * GRID_W, GRID_W) for r in rr]
        k0 = [pl.multiple_of(n_ctx + s * GRID_W, GRID_W) for s in rs]
        scores = []
        for j in range(group):
            q = q_ref[0, pl.ds(q0[j], GRID_W), :]
            s_lat = _dot_nt(q, k_ref[0, pl.ds(k0[j], kr * GRID_W), :]) * scale + bias_ref[0, rr[j] - rs[j]]
            scores.append([s_lat, _dot_nt(q, kc) * scale])
        probs = [_softmax_parts(s) for s in scores]
        for j in range(group):
            (p_lat, p_ctx), den = probs[j]
            o = _dot(p_lat.astype(BF16), v_ref[0, pl.ds(k0[j], kr * GRID_W), :]) + _dot(p_ctx.astype(BF16), vc)
            o_ref[0, pl.ds(q0[j], GRID_W), :] = (o / den).astype(BF16)
        return carry

    lax.fori_loop(0, rows // group, row_group, 0)


def _na_bias_table(rpb, rows, kr):
    h, n_dr, n_dc = rpb.shape
    win_rows, win_cols = (n_dr + 1) // 2, (n_dc + 1) // 2
    col = np.arange(GRID_W)
    col_start = np.clip(col - win_cols // 2, 0, GRID_W - win_cols)
    col_mask = (col[None, :] >= col_start[:, None]) & (col[None, :] < col_start[:, None] + win_cols)
    dc_idx = np.clip(col[None, :] - col[:, None], 1 - win_cols, win_cols - 1) + win_cols - 1
    dr_idx = np.arange(kr)[None, :] - np.arange(kr)[:, None] + win_rows - 1
    t = rpb[:, dr_idx][:, :, :, dc_idx]
    t = jnp.where(col_mask[None, None, None], t.astype(F32), -jnp.inf)
    return t.transpose(0, 1, 3, 2, 4).reshape(h, kr, GRID_W, kr * GRID_W)


def _attention(qkv, bias, n_ctx, heads):
    b, l, d3 = qkv.shape
    d = d3 // 3
    dh = d // heads
    rows = (l - n_ctx) // GRID_W
    kr = bias.shape[1]
    kern = functools.partial(_attn_kernel, n_ctx=n_ctx, rows=rows, kr=kr, scale=dh ** -0.5)
    return pl.pallas_call(
        kern,
        grid=(heads, b),
        in_specs=[
            pl.BlockSpec((1, l, dh), lambda h, i: (i, 0, h)),
            pl.BlockSpec((1, l, dh), lambda h, i: (i, 0, heads + h)),
            pl.BlockSpec((1, l, dh), lambda h, i: (i, 0, 2 * heads + h)),
            pl.BlockSpec((1, kr, GRID_W, kr * GRID_W), lambda h, i: (h, 0, 0, 0)),
        ],
        out_specs=pl.BlockSpec((1, l, dh), lambda h, i: (i, 0, h)),
        out_shape=jax.ShapeDtypeStruct((b, l, d), BF16),
        compiler_params=_cparams(("parallel", "parallel")),
        name="na_attention",
    )(qkv, qkv, qkv, bias)


def _proj_kernel(a_ref, w_ref, b_ref, x_ref, tab_ref, g_ref, rw_ref, rb_ref,
                 x1_ref, hrow_ref, lg_ref, *, ctx_tiles, tile_off):
    is_ctx = pl.program_id(1) + tile_off < ctx_tiles
    y = _dot(a_ref[0], w_ref[...]) + b_ref[...]
    x1 = x_ref[0] + _tab_row(tab_ref, is_ctx, 2) * y
    x1_ref[0] = x1
    h2 = _norm_mod(x1, g_ref[...], _tab_row(tab_ref, is_ctx, 3), _tab_row(tab_ref, is_ctx, 4))
    lg_ref[...] = _dot3(h2, rw_ref[...]) + rb_ref[...]
    hrow_ref[...] = _to_token_rows(h2)


def _proj(a, w, bias, x, tab, g2, rw, rb, n_ctx, latent_only):
    b, l, d = x.shape
    tm = n_ctx
    off = n_ctx // tm if latent_only else 0
    nt = l // tm - off
    s_rows = d // LANES
    t = b * nt * tm
    kern = functools.partial(_proj_kernel, ctx_tiles=n_ctx // tm, tile_off=off)
    return pl.pallas_call(
        kern,
        grid=(b, nt),
        in_specs=[
            pl.BlockSpec((1, tm, d), lambda i, j: (i, j + off, 0)),
            pl.BlockSpec((d, d), lambda i, j: (0, 0)),
            pl.BlockSpec((1, d), lambda i, j: (0, 0)),
            pl.BlockSpec((1, tm, d), lambda i, j: (i, j + off, 0)),
            pl.BlockSpec((1, TAB_ROWS, d), lambda i, j: (i, 0, 0)),
            pl.BlockSpec((1, d), lambda i, j: (0, 0)),
            pl.BlockSpec((d, LANES), lambda i, j: (0, 0)),
            pl.BlockSpec((1, LANES), lambda i, j: (0, 0)),
        ],
        out_specs=[
            pl.BlockSpec((1, tm, d), lambda i, j: (i, j, 0)),
            pl.BlockSpec((tm * s_rows, LANES), lambda i, j: (i * nt + j, 0)),
            pl.BlockSpec((tm, LANES), lambda i, j: (i * nt + j, 0)),
        ],
        out_shape=[
            jax.ShapeDtypeStruct((b, nt * tm, d), F32),
            jax.ShapeDtypeStruct((t * s_rows, LANES), F32),
            jax.ShapeDtypeStruct((t, LANES), F32),
        ],
        compiler_params=_cparams(("parallel", "parallel")),
        name="mixer_proj",
    )(a, w, bias.reshape(1, d), x, tab, g2.reshape(1, d), rw, rb)


def _route_kernel(lg_ref, e_ref, g_ref, r_ref, cnt_ref, carry_ref, *, n_exp):
    @pl.when(pl.program_id(0) == 0)
    def _():
        carry_ref[...] = jnp.zeros_like(carry_ref)

    lg = lg_ref[...]
    tm = lg.shape[0]
    lane = lax.broadcasted_iota(jnp.int32, lg.shape, 1).astype(F32)
    cur = jnp.where(lane < n_exp, lg, -jnp.inf)
    multi = jnp.zeros(lg.shape, F32)
    vals, idxs = [], []
    for _ in range(TOP_K):
        m = cur.max(axis=-1, keepdims=True)
        idx = jnp.where(cur == m, lane, float(LANES)).min(axis=-1, keepdims=True)
        sel = lane == idx
        multi = jnp.where(sel, 1.0, multi)
        cur = jnp.where(sel, -jnp.inf, cur)
        vals.append(m)
        idxs.append(idx)
    exps = [jnp.exp(v - vals[0]) for v in vals]
    den = exps[0]
    for e in exps[1:]:
        den = den + e
    tri = (lax.broadcasted_iota(jnp.int32, (tm, tm), 0) > lax.broadcasted_iota(jnp.int32, (tm, tm), 1))
    pref = _dot(jnp.where(tri, 1.0, 0.0).astype(BF16), multi.astype(BF16))
    tot = carry_ref[...] + pref
    e_out = jnp.zeros(lg.shape, F32)
    g_out = jnp.zeros(lg.shape, F32)
    r_out = jnp.zeros(lg.shape, F32)
    for k in range(TOP_K):
        rank_k = jnp.where(lane == idxs[k], tot, 0.0).sum(axis=-1, keepdims=True)
        e_out = jnp.where(lane == k, idxs[k], e_out)
        g_out = jnp.where(lane == k, exps[k] / den, g_out)
        r_out = jnp.where(lane == k, rank_k, r_out)
    e_ref[...] = e_out.astype(jnp.int32)
    g_ref[...] = g_out
    r_ref[...] = r_out.astype(jnp.int32)
    carry_ref[...] = carry_ref[...] + multi.sum(axis=0, keepdims=True)
    cnt_ref[...] = carry_ref[...]


def _route(logits, n_exp):
    t = logits.shape[0]
    tm = _pick(t, 256)
    spec = pl.BlockSpec((tm, LANES), lambda i: (i, 0))
    return pl.pallas_call(
        functools.partial(_route_kernel, n_exp=n_exp),
        grid=(t // tm,),
        in_specs=[spec],
        out_specs=[spec, spec, spec, pl.BlockSpec((1, LANES), lambda i: (0, 0))],
        out_shape=[
            jax.ShapeDtypeStruct((t, LANES), jnp.int32),
            jax.ShapeDtypeStruct((t, LANES), F32),
            jax.ShapeDtypeStruct((t, LANES), jnp.int32),
            jax.ShapeDtypeStruct((1, LANES), F32),
        ],
        scratch_shapes=[pltpu.VMEM((1, LANES), F32)],
        compiler_params=_cparams(("arbitrary",)),
        name="router",
    )(logits)


def _dispatch_kernel(dest_ref, h_ref, xs_ref, sem, *, td, s_rows):
    def issue(t, carry):
        src = h_ref.at[pl.ds(pl.multiple_of(t * s_rows, s_rows), s_rows), :]
        for k in range(TOP_K):
            d = dest_ref[0, 0, t * TOP_K + k]
            dst = xs_ref.at[pl.ds(pl.multiple_of(d * s_rows, s_rows), s_rows), :]
            pltpu.make_async_copy(src, dst, sem).start()
        return carry

    lax.fori_loop(0, td, issue, 0)
    for _ in range(TOP_K):
        pltpu.make_async_copy(h_ref, xs_ref.at[pl.ds(0, td * s_rows), :], sem).wait()


def _dispatch(hrow, dest, s_rows):
    t = dest.shape[0] // TOP_K
    td = _pick(t, 256)
    nt = t // td
    return pl.pallas_call(
        functools.partial(_dispatch_kernel, td=td, s_rows=s_rows),
        grid=(nt,),
        in_specs=[
            pl.BlockSpec((1, 1, td * TOP_K), lambda i: (i, 0, 0), memory_space=pltpu.SMEM),
            pl.BlockSpec((td * s_rows, LANES), lambda i: (i, 0)),
        ],
        out_specs=pl.BlockSpec(memory_space=pl.ANY),
        out_shape=jax.ShapeDtypeStruct((t * TOP_K * s_rows, LANES), F32),
        scratch_shapes=[pltpu.SemaphoreType.DMA(())],
        compiler_params=pltpu.CompilerParams(dimension_semantics=("arbitrary",), vmem_limit_bytes=VMEM_LIMIT,
                                             has_side_effects=True),
        name="moe_dispatch",
    )(dest.reshape(nt, 1, td * TOP_K), hrow)


def _gmm_kernel(it_tile, it_e, it_lo, it_hi, it_first, it_valid,
                xs_ref, wg_ref, wl_ref, bg_ref, bl_ref, wd_ref, bd_ref, ys_ref, xb_ref, *, tg, s_rows, fk):
    m = pl.program_id(0)
    tsub = _pick(tg, 128)
    f = wd_ref.shape[1]

    @pl.when(it_valid[m] == 1)
    def _():
        for t0 in range(0, tg, tsub):
            xt = _from_token_rows(xs_ref[pl.ds(t0 * s_rows, tsub * s_rows), :], s_rows)
            for s in range(s_rows):
                xb_ref[pl.ds(t0, tsub), s * LANES:(s + 1) * LANES] = xt[s].astype(BF16)

        x = xb_ref[...]
        y = None
        for f0 in range(0, f, fk):
            cols = slice(f0, f0 + fk)
            glu = jnp.minimum(_dot(x, wg_ref[0, :, cols]) + bg_ref[0, :, cols], SWIGLU_LIMIT)
            lin = jnp.clip(_dot(x, wl_ref[0, :, cols]) + bl_ref[0, :, cols], -SWIGLU_LIMIT, SWIGLU_LIMIT)
            act = glu * jax.nn.sigmoid(SWIGLU_ALPHA * glu) * (lin + 1.0)
            part = _dot(act.astype(BF16), wd_ref[0, cols, :])
            y = part if y is None else y + part
        y = y + bd_ref[0]

        for t0 in range(0, tg, tsub):
            rows = pl.ds(t0 * s_rows, tsub * s_rows)
            yr = _to_token_rows(y[t0:t0 + tsub])
            row = t0 * s_rows + lax.broadcasted_iota(jnp.int32, (tsub * s_rows, 1), 0)
            mine = (row >= it_lo[m] * s_rows) & (row < it_hi[m] * s_rows)

            @pl.when(it_first[m] == 1)
            def _():
                ys_ref[rows, :] = jnp.where(mine, yr, 0.0)

            @pl.when(it_first[m] == 0)
            def _():
                ys_ref[rows, :] = jnp.where(mine, yr, ys_ref[rows, :])


def _gmm_items(counts, n_tiles, tg, max_items):
    n_exp = counts.shape[0]
    ends = jnp.cumsum(counts)
    starts = ends - counts
    def count_le(sorted_vals, q):
        return jnp.sum(sorted_vals[None, :] <= q[:, None], axis=1, dtype=jnp.int32)

    tile0 = jnp.arange(n_tiles, dtype=jnp.int32) * tg
    e_lo = jnp.minimum(count_le(ends, tile0), n_exp - 1)
    e_hi = jnp.minimum(count_le(ends, tile0 + tg - 1), n_exp - 1)
    n_items = e_hi - e_lo + 1
    item_end = jnp.cumsum(n_items)
    item_start = item_end - n_items
    total = item_end[-1]
    m = jnp.arange(max_items, dtype=jnp.int32)
    valid = m < total
    tile = jnp.minimum(count_le(item_end, m), n_tiles - 1)
    e = jnp.where(valid, e_lo[tile] + (m - item_start[tile]), e_hi[n_tiles - 1]).astype(jnp.int32)
    lo = jnp.clip(starts[e] - tile * tg, 0, tg).astype(jnp.int32)
    hi = jnp.clip(ends[e] - tile * tg, 0, tg).astype(jnp.int32)
    first = (m == item_start[tile]).astype(jnp.int32)
    return tile, e, lo, hi, first, valid.astype(jnp.int32)


GMM_ROW_TILE = 256
GMM_F_CHUNK = 512


def _gmm(xs, counts, w, layer, s_rows):
    n_exp = counts.shape[0]
    d, f = w["wg"].shape[1:]
    p = xs.shape[0] // s_rows
    tg = _pick(p, GMM_ROW_TILE)
    fk = _pick(f, GMM_F_CHUNK, LANES)
    n_tiles = p // tg
    max_items = n_tiles + n_exp - 1
    tile, e, lo, hi, first, valid = _gmm_items(counts, n_tiles, tg, max_items)
    items = (tile, e + layer * n_exp, lo, hi, first, valid)

    def wspec(block, buffers=1):
        return pl.BlockSpec(block, lambda m, t, e, lo, hi, fi, va: (e[m], 0, 0), pipeline_mode=pl.Buffered(buffers))

    rows_spec = pl.BlockSpec((tg * s_rows, LANES), lambda m, t, e, lo, hi, fi, va: (t[m], 0))
    grid_spec = pltpu.PrefetchScalarGridSpec(
        num_scalar_prefetch=6,
        grid=(max_items,),
        in_specs=[
            rows_spec,
            wspec((1, d, f), 2),
            wspec((1, d, f), 2),
            wspec((1, 1, f), 2),
            wspec((1, 1, f), 2),
            wspec((1, f, d)),
            wspec((1, 1, d), 2),
        ],
        out_specs=rows_spec,
        scratch_shapes=[pltpu.VMEM((tg, d), BF16)],
    )
    return pl.pallas_call(
        functools.partial(_gmm_kernel, tg=tg, s_rows=s_rows, fk=fk),
        grid_spec=grid_spec,
        out_shape=jax.ShapeDtypeStruct(xs.shape, F32),
        compiler_params=_cparams(("arbitrary",)),
        name="moe_experts",
    )(*items, xs, w["wg"], w["wl"], w["bg"], w["bl"], w["wd"], w["bd"])


def _combine_kernel(dest_ref, dest_next_ref, ys_ref, gate_ref, x_ref, tab_ref, tabn_ref, g_ref,
                    x2_ref, h_ref, buf_ref, moe_ref, sem, *, tc, s_rows, ctx_tiles, n_steps, mode):
    step = pl.program_id(0) * pl.num_programs(1) + pl.program_id(1)
    slot = step % 2

    def gather(dref, slot_idx):
        def issue(t, carry):
            for k in range(TOP_K):
                d = dref[0, 0, t * TOP_K + k]
                src = ys_ref.at[pl.ds(pl.multiple_of(d * s_rows, s_rows), s_rows), :]
                dst = buf_ref.at[slot_idx, pl.ds(pl.multiple_of((k * tc + t) * s_rows, s_rows), s_rows), :]
                pltpu.make_async_copy(src, dst, sem.at[slot_idx]).start()
            return carry

        lax.fori_loop(0, tc, issue, 0)

    @pl.when(step == 0)
    def _():
        gather(dest_ref, 0)

    @pl.when(step + 1 < n_steps)
    def _():
        gather(dest_next_ref, 1 - slot)

    pltpu.make_async_copy(buf_ref.at[slot], buf_ref.at[slot], sem.at[slot]).wait()

    gates = gate_ref[...]
    for k in range(TOP_K):
        chunks = _from_token_rows(buf_ref[slot, pl.ds(k * tc * s_rows, tc * s_rows), :], s_rows)
        for s in range(s_rows):
            term = chunks[s] * gates[:, k:k + 1]
            if k == 0:
                moe_ref[:, s * LANES:(s + 1) * LANES] = term
            else:
                moe_ref[:, s * LANES:(s + 1) * LANES] += term

    is_ctx = pl.program_id(1) < ctx_tiles
    x2 = x_ref[0] + _tab_row(tab_ref, is_ctx, 5) * moe_ref[...]
    x2_ref[0] = x2
    if mode == "next":
        h = _norm_mod(x2, g_ref[...], _tab_row(tabn_ref, is_ctx, 0), _tab_row(tabn_ref, is_ctx, 1))
        h_ref[0] = h.astype(h_ref.dtype)
    else:
        y = x2 * lax.rsqrt(jnp.mean(x2 * x2, axis=-1, keepdims=True) + RMS_EPS)
        h_ref[0] = (y * g_ref[...]).astype(h_ref.dtype)


def _combine(ys, dest, gates, x1, tab, tab_next, g_next, n_ctx_rows, s_rows, mode):
    b, l, d = x1.shape
    tc = _pick(min(l, 128) if n_ctx_rows == 0 else n_ctx_rows, 128)
    nt = l // tc
    n_steps = b * nt
    dest3 = dest.reshape(n_steps, 1, tc * TOP_K)
    kern = functools.partial(_combine_kernel, tc=tc, s_rows=s_rows, ctx_tiles=n_ctx_rows // tc,
                             n_steps=n_steps, mode=mode)
    out_dtype = BF16 if mode == "next" else F32
    return pl.pallas_call(
        kern,
        grid=(b, nt),
        in_specs=[
            pl.BlockSpec((1, 1, tc * TOP_K), lambda i, j: (i * nt + j, 0, 0), memory_space=pltpu.SMEM),
            pl.BlockSpec((1, 1, tc * TOP_K), lambda i, j: (jnp.minimum(i * nt + j + 1, n_steps - 1), 0, 0),
                         memory_space=pltpu.SMEM),
            pl.BlockSpec(memory_space=pl.ANY),
            pl.BlockSpec((tc, LANES), lambda i, j: (i * nt + j, 0)),
            pl.BlockSpec((1, tc, d), lambda i, j: (i, j, 0)),
            pl.BlockSpec((1, TAB_ROWS, d), lambda i, j: (i, 0, 0)),
            pl.BlockSpec((1, TAB_ROWS, d), lambda i, j: (i, 0, 0)),
            pl.BlockSpec((1, d), lambda i, j: (0, 0)),
        ],
        out_specs=[
            pl.BlockSpec((1, tc, d), lambda i, j: (i, j, 0)),
            pl.BlockSpec((1, tc, d), lambda i, j: (i, j, 0)),
        ],
        out_shape=[
            jax.ShapeDtypeStruct((b, l, d), F32),
            jax.ShapeDtypeStruct((b, l, d), out_dtype),
        ],
        scratch_shapes=[
            pltpu.VMEM((2, tc * TOP_K * s_rows, LANES), F32),
            pltpu.VMEM((tc, d), F32),
            pltpu.SemaphoreType.DMA((2,)),
        ],
        compiler_params=_cparams(("arbitrary", "arbitrary")),
        name="moe_combine_" + mode,
    )(dest3, dest3, ys, gates, x1, tab, tab_next, g_next.reshape(1, d))


def _moe(hrow, logits, x1, tab, tab_next, g_next, w, layer, n_exp, n_ctx_rows, mode):
    s_rows = x1.shape[-1] // LANES
    e_pad, g_pad, r_pad, cnt = _route(logits, n_exp)
    top_e, rank = e_pad[:, :TOP_K], r_pad[:, :TOP_K]
    counts = cnt[0, :n_exp].astype(jnp.int32)
    starts = jnp.cumsum(counts) - counts
    dest = (starts[top_e] + rank).reshape(-1)
    xs = _dispatch(hrow, dest, s_rows)
    ys = _gmm(xs, counts, w, layer, s_rows)
    return _combine(ys, dest, g_pad, x1, tab, tab_next, g_next, n_ctx_rows, s_rows, mode)


def _conv_kernel(x_ref, w_ref, b_ref, o_ref, *, n_ctx):
    l = x_ref.shape[1]
    w = w_ref[...]
    for s0, n in ((0, n_ctx), (n_ctx, l - n_ctx)):
        x = x_ref[0, s0:s0 + n, :]
        row = lax.broadcasted_iota(jnp.int32, (n, 1), 0)
        acc = x * w[CONV_LEFT:CONV_LEFT + 1]
        for j in range(w.shape[0]):
            off = j - CONV_LEFT
            if off == 0:
                continue
            shifted = pltpu.roll(x, (-off) % n, 0)
            ok = (row + off >= 0) & (row + off < n)
            acc = acc + jnp.where(ok, shifted, 0.0) * w[j:j + 1]
        o_ref[0, s0:s0 + n, :] = acc + b_ref[...]


def _conv(xb, conv_w, conv_b, n_ctx):
    b, l, d = xb.shape
    dt = _pick(d, 256, LANES)
    return pl.pallas_call(
        functools.partial(_conv_kernel, n_ctx=n_ctx),
        grid=(b, d // dt),
        in_specs=[
            pl.BlockSpec((1, l, dt), lambda i, j: (i, 0, j)),
            pl.BlockSpec((conv_w.shape[0], dt), lambda i, j: (0, j)),
            pl.BlockSpec((1, dt), lambda i, j: (0, j)),
        ],
        out_specs=pl.BlockSpec((1, l, dt), lambda i, j: (i, 0, j)),
        out_shape=jax.ShapeDtypeStruct((b, l, d), F32),
        compiler_params=_cparams(("parallel", "parallel")),
        name="rg_conv",
    )(xb, conv_w, conv_b.reshape(1, d))


SCAN_PAD = 8
SCAN_UNROLL = 8


def _scan_kernel(*refs, nb, tc, nh, reverse):
    n_in = 8 if reverse else 6
    u_ref, wa_ref, ba_ref, wi_ref, bi_ref, lam_ref = refs[:6]
    o_ref = refs[n_in]
    scratch = refs[n_in + 1:]
    a_s, x_s, h_s = scratch[0:nh], scratch[nh:2 * nh], scratch[2 * nh:3 * nh]
    carry = scratch[3 * nh]
    ts = tc + SCAN_PAD

    @pl.when(pl.program_id(1) == 0)
    def _():
        carry[...] = jnp.zeros_like(carry)

    neg = -lam_ref[0]
    softplus = jnp.maximum(neg, 0.0) + jnp.log1p(jnp.exp(-jnp.abs(neg)))
    for bi in range(nb):
        u = u_ref[bi]
        ub = u.astype(BF16)
        r = jax.nn.sigmoid(_dot(ub, wa_ref[0, 0]) + ba_ref[0, 0])
        i = jax.nn.sigmoid(_dot(ub, wi_ref[0, 0]) + bi_ref[0, 0])
        log_a = (-RG_C) * r * softplus
        a = jnp.exp(log_a)
        xin = jnp.sqrt(1.0 - a * a) * (i * u)
        for p in range(nh):
            a_s[p][pl.ds(bi * ts, tc), :] = a[:, p * LANES:(p + 1) * LANES]
            x_s[p][pl.ds(bi * ts, tc), :] = xin[:, p * LANES:(p + 1) * LANES]

    def block(j, hs):
        hs = list(hs)
        for q in range(SCAN_UNROLL):
            t = j * SCAN_UNROLL + q
            if reverse:
                t = tc - 1 - t
            for p in range(nh):
                hs[p] = a_s[p][pl.ds(t, nb, stride=ts), :] * hs[p] + x_s[p][pl.ds(t, nb, stride=ts), :]
                h_s[p][pl.ds(t, nb, stride=ts), :] = hs[p]
        return tuple(hs)

    h0 = tuple(carry[:, p * LANES:(p + 1) * LANES] for p in range(nh))
    hs = lax.fori_loop(0, tc // SCAN_UNROLL, block, h0)
    for p in range(nh):
        carry[:, p * LANES:(p + 1) * LANES] = hs[p]
    for bi in range(nb):
        for p in range(nh):
            h = h_s[p][pl.ds(bi * ts, tc), :]
            cols = slice(p * LANES, (p + 1) * LANES)
            if reverse:
                hf_ref, gy_ref = refs[6], refs[7]
                o_ref[bi, :, cols] = ((hf_ref[bi, :, cols] + h) * gy_ref[bi, :, cols]).astype(o_ref.dtype)
            else:
                o_ref[bi, :, cols] = h


def _scan(u, w_a, b_a, w_i, b_i, lam, dirn, n_ctx, h_fwd=None, gy=None):
    b, l, d = u.shape
    n_blk, w = w_a.shape[1], w_a.shape[2]
    tc = _pick(n_ctx, 128)
    nt, nc = l // tc, n_ctx // tc
    reverse = dirn == 1

    def chunk(j):
        if not reverse:
            return j
        return jnp.where(j < nc, nc - 1 - j, nt - 1 - (j - nc))

    blk = pl.BlockSpec((b, tc, w), lambda g, j: (0, chunk(j), g))
    in_specs = [
        blk,
        pl.BlockSpec((1, 1, w, w), lambda g, j: (dirn, g, 0, 0)),
        pl.BlockSpec((1, 1, 1, w), lambda g, j: (dirn, g, 0, 0)),
        pl.BlockSpec((1, 1, w, w), lambda g, j: (dirn, g, 0, 0)),
        pl.BlockSpec((1, 1, 1, w), lambda g, j: (dirn, g, 0, 0)),
        pl.BlockSpec((1, 1, w), lambda g, j: (dirn, 0, g)),
    ]
    args = [u, w_a, b_a.reshape(2, n_blk, 1, w), w_i, b_i.reshape(2, n_blk, 1, w), lam.reshape(2, 1, d)]
    if reverse:
        in_specs += [blk, blk]
        args += [h_fwd, gy]
    ts = tc + SCAN_PAD
    return pl.pallas_call(
        functools.partial(_scan_kernel, nb=b, tc=tc, nh=w // LANES, reverse=reverse),
        grid=(n_blk, nt),
        in_specs=in_specs,
        out_specs=blk,
        out_shape=jax.ShapeDtypeStruct((b, l, d), BF16 if reverse else F32),
        scratch_shapes=[pltpu.VMEM((b * ts, LANES), F32)] * (3 * (w // LANES)) + [pltpu.VMEM((b, w), F32)],
        compiler_params=_cparams(("parallel", "arbitrary")),
        name="rg_scan_" + ("rev" if reverse else "fwd"),
    )(*args)


def _mod_tables(mod_out, b, d):
    tabs = []
    for i in range(mod_out.shape[0]):
        ml = mod_out[i, :b].reshape(b, MOD_ROWS, d)
        mc = jnp.broadcast_to(mod_out[i, b].reshape(1, MOD_ROWS, d), (b, MOD_ROWS, d))
        pad = jnp.zeros((b, TAB_ROWS - 2 * MOD_ROWS, d), F32)
        tabs.append(jnp.concatenate([mc, ml, pad], axis=1))
    return tabs


def _split_gu_kernel(w_ref, perm_ref, wg_ref, wl_ref):
    for j in range(w_ref.shape[1] // (2 * LANES)):
        y = _dot(w_ref[:, 2 * LANES * j:2 * LANES * (j + 1)].astype(BF16), perm_ref[...])
        wg_ref[:, LANES * j:LANES * (j + 1)] = y[:, :LANES].astype(BF16)
        wl_ref[:, LANES * j:LANES * (j + 1)] = y[:, LANES:].astype(BF16)


def _split_gu(w_gu):
    lead, n2 = w_gu.shape[:-1], w_gu.shape[-1]
    rows = int(np.prod(lead))
    tr = _pick(rows, 512)
    src = np.concatenate([np.arange(0, 2 * LANES, 2), np.arange(1, 2 * LANES, 2)])
    perm = jnp.asarray(np.arange(2 * LANES)[:, None] == src[None, :], BF16)
    out_spec = pl.BlockSpec((tr, n2 // 2), lambda i: (i, 0))
    wg, wl = pl.pallas_call(
        _split_gu_kernel,
        grid=(rows // tr,),
        in_specs=[pl.BlockSpec((tr, n2), lambda i: (i, 0)), pl.BlockSpec((2 * LANES, 2 * LANES), lambda i: (0, 0))],
        out_specs=[out_spec, out_spec],
        out_shape=[jax.ShapeDtypeStruct((rows, n2 // 2), BF16)] * 2,
        compiler_params=_cparams(("parallel",)),
        name="split_gu_weights",
    )(w_gu.reshape(rows, n2), perm)
    return wg.reshape(*lead, n2 // 2), wl.reshape(*lead, n2 // 2)


def _moe_weights(router_w, router_b, w_gu, b_gu, w_dn, b_dn):
    depth, d, n_exp = router_w.shape
    f = w_dn.shape[2]
    wg, wl = _split_gu(w_gu)
    return {
        "rw": jnp.pad(router_w, ((0, 0), (0, 0), (0, LANES - n_exp))),
        "rb": jnp.pad(router_b, ((0, 0), (0, LANES - n_exp))).reshape(depth, 1, LANES),
        "wg": wg.reshape(depth * n_exp, d, f),
        "wl": wl.reshape(depth * n_exp, d, f),
        "bg": b_gu[..., 0::2].reshape(depth * n_exp, 1, f),
        "bl": b_gu[..., 1::2].reshape(depth * n_exp, 1, f),
        "wd": w_dn.astype(BF16).reshape(depth * n_exp, f, d),
        "bd": b_dn.reshape(depth * n_exp, 1, d),
    }


def kernel(x, c, ctx, c_ctx, mod_w, mod_b, norm1_g, norm2_g, final_g, na_w_qkv, na_w_o, na_rpb, rg_w_y, rg_b_y,
           rg_w_x, rg_b_x, rg_conv_w, rg_conv_b, rg_w_a, rg_b_a, rg_w_i, rg_b_i, rg_lam, rg_w_out, rg_b_out,
           moe_router_w, moe_router_b, moe_w_gu, moe_b_gu, moe_w_dn, moe_b_dn):
    b, n_lat, d = x.shape
    n_ctx = ctx.shape[1]
    l = n_ctx + n_lat
    heads = na_rpb.shape[1]
    rows = n_lat // GRID_W
    kr = min((na_rpb.shape[2] + 1) // 2, rows)
    assert mod_w.shape[0] == 2 and n_lat % n_ctx == 0 and d % LANES == 0

    mod_rows = -(-(b + 1) // 8) * 8
    cc = jnp.concatenate([c, c_ctx[None, :], jnp.zeros((mod_rows - b - 1, d), F32)], axis=0)
    tab0, tab1 = _mod_tables(_modulation(cc, mod_w, mod_b), b, d)

    xs0 = jnp.concatenate([ctx, x], axis=1)

    h = _prenorm(xs0, tab0, norm1_g[0], n_ctx)
    qkv = _matmul(h.reshape(b * l, d), na_w_qkv[0].astype(BF16), jnp.zeros((3 * d,), F32), BF16)
    bias = _na_bias_table(na_rpb[0], rows, kr)
    o = _attention(qkv.reshape(b, l, 3 * d), bias, n_ctx, heads)
    n_exp = moe_router_w.shape[2]
    w = _moe_weights(moe_router_w, moe_router_b, moe_w_gu, moe_b_gu, moe_w_dn, moe_b_dn)
    x1, hrow, logits = _proj(o, na_w_o[0].astype(BF16), jnp.zeros((d,), F32), xs0, tab0, norm2_g[0],
                             w["rw"][0], w["rb"][0], n_ctx, latent_only=False)
    x2, h = _moe(hrow, logits, x1, tab0, tab1, norm1_g[1], w, 0, n_exp, n_ctx, "next")

    hf = h.reshape(b * l, d)
    xb = _matmul(hf, rg_w_x[0].astype(BF16), rg_b_x[0], F32).reshape(b, l, d)
    gy = _matmul(hf, rg_w_y[0].astype(BF16), rg_b_y[0], F32, act="gelu").reshape(b, l, d)
    u = _conv(xb, rg_conv_w[0], rg_conv_b[0], n_ctx)
    w_a, w_i = rg_w_a[0].astype(BF16), rg_w_i[0].astype(BF16)
    h_fwd = _scan(u, w_a, rg_b_a[0], w_i, rg_b_i[0], rg_lam[0], 0, n_ctx)
    hg = _scan(u, w_a, rg_b_a[0], w_i, rg_b_i[0], rg_lam[0], 1, n_ctx, h_fwd, gy)
    x1, hrow, logits = _proj(hg, rg_w_out[0].astype(BF16), rg_b_out[0], x2, tab1, norm2_g[1],
                             w["rw"][1], w["rb"][1], n_ctx, latent_only=True)
    _, out = _moe(hrow, logits, x1, tab1, tab1, final_g, w, 1, n_exp, 0, "final")
    return out
```

```python
import functools

import jax
import jax.numpy as jnp
import numpy as np
from jax import lax
from jax.experimental import pallas as pl
from jax.experimental.pallas import tpu as pltpu

F32 = jnp.float32
BF16 = jnp.bfloat16

LANES = 128
GRID_W = 64
TOP_K = 4
RG_C = 8.0
CONV_LEFT = 2
SWIGLU_ALPHA = 1.702
SWIGLU_LIMIT = 7.0
RMS_EPS = 1e-6
MOD_ROWS = 6
TAB_ROWS = 16
VMEM_LIMIT = 56 * 1024 * 1024


def _cparams(sem):
    return pltpu.CompilerParams(dimension_semantics=sem, vmem_limit_bytes=VMEM_LIMIT)


def _pick(n, pref, mult=8):
    for t in range(min(pref, n), 0, -1):
        if n % t == 0 and t % mult == 0:
            return t
    return n


def _dot(a, b):
    return jnp.dot(a, b, preferred_element_type=F32)


def _dot_nt(a, b):
    return lax.dot_general(a, b, (((1,), (1,)), ((), ())), preferred_element_type=F32)


def _split_bf16(x):
    hi = x.astype(BF16)
    lo = (x - hi.astype(F32)).astype(BF16)
    return hi, lo


def _dot3(a, w):
    a_hi, a_lo = _split_bf16(a)
    w_hi, w_lo = _split_bf16(w)
    return _dot(a_hi, w_hi) + _dot(a_lo, w_hi) + _dot(a_hi, w_lo)


def _norm_mod(x, g, shift, scale):
    y = x * lax.rsqrt(jnp.mean(x * x, axis=-1, keepdims=True) + RMS_EPS)
    return (y * g) * (1.0 + scale) + shift


def _to_token_rows(v):
    t, d = v.shape
    s_rows = d // LANES
    chunks = jnp.stack([v[:, s * LANES:(s + 1) * LANES] for s in range(s_rows)], axis=0)
    return pltpu.einshape("stl->tsl", chunks).reshape(t * s_rows, LANES)


def _from_token_rows(r, s_rows):
    return pltpu.einshape("tsl->stl", r.reshape(r.shape[0] // s_rows, s_rows, LANES))


def _tab_row(tab_ref, is_ctx, k):
    base = jnp.where(is_ctx, 0, MOD_ROWS)
    return tab_ref[0, pl.ds(base + k, 1), :]


def _mod_kernel(a_ref, w_ref, b_ref, o_ref):
    a = a_ref[...]
    a = a * jax.nn.sigmoid(a)
    o_ref[0] = _dot3(a, w_ref[0]) + b_ref[0]


def _modulation(cc, mod_w, mod_b):
    depth, d, n = mod_w.shape
    r = cc.shape[0]
    tn = _pick(n, 1024, LANES)
    return pl.pallas_call(
        _mod_kernel,
        grid=(depth, n // tn),
        in_specs=[
            pl.BlockSpec((r, d), lambda i, j: (0, 0)),
            pl.BlockSpec((1, d, tn), lambda i, j: (i, 0, j)),
            pl.BlockSpec((1, 1, tn), lambda i, j: (i, 0, j)),
        ],
        out_specs=pl.BlockSpec((1, r, tn), lambda i, j: (i, 0, j)),
        out_shape=jax.ShapeDtypeStruct((depth, r, n), F32),
        compiler_params=_cparams(("parallel", "parallel")),
        name="modulation",
    )(cc, mod_w, mod_b.reshape(depth, 1, n))


def _prenorm_kernel(x_ref, tab_ref, g_ref, o_ref, *, ctx_tiles):
    is_ctx = pl.program_id(1) < ctx_tiles
    h = _norm_mod(x_ref[0], g_ref[...], _tab_row(tab_ref, is_ctx, 0), _tab_row(tab_ref, is_ctx, 1))
    o_ref[0] = h.astype(BF16)


def _prenorm(x, tab, g, n_ctx):
    b, l, d = x.shape
    tm = n_ctx
    return pl.pallas_call(
        functools.partial(_prenorm_kernel, ctx_tiles=n_ctx // tm),
        grid=(b, l // tm),
        in_specs=[
            pl.BlockSpec((1, tm, d), lambda i, j: (i, j, 0)),
            pl.BlockSpec((1, TAB_ROWS, d), lambda i, j: (i, 0, 0)),
            pl.BlockSpec((1, d), lambda i, j: (0, 0)),
        ],
        out_specs=pl.BlockSpec((1, tm, d), lambda i, j: (i, j, 0)),
        out_shape=jax.ShapeDtypeStruct((b, l, d), BF16),
        compiler_params=_cparams(("parallel", "parallel")),
        name="prenorm",
    )(x, tab, g.reshape(1, d))


def _gelu_tanh(x):
    return 0.5 * x * (1.0 + jnp.tanh(np.sqrt(2.0 / np.pi) * (x + 0.044715 * (x * x * x))))


def _matmul_kernel(a_ref, w_ref, b_ref, o_ref, *, act):
    y = _dot(a_ref[...], w_ref[...]) + b_ref[...]
    if act == "gelu":
        y = _gelu_tanh(y)
    o_ref[...] = y.astype(o_ref.dtype)


def _matmul(a, w, bias, out_dtype, act=None):
    m, k = a.shape
    n = w.shape[1]
    tm = _pick(m, 1024)
    tn = _pick(n, 512, LANES)
    return pl.pallas_call(
        functools.partial(_matmul_kernel, act=act),
        grid=(m // tm, n // tn),
        in_specs=[
            pl.BlockSpec((tm, k), lambda i, j: (i, 0)),
            pl.BlockSpec((k, tn), lambda i, j: (0, j)),
            pl.BlockSpec((1, tn), lambda i, j: (0, j)),
        ],
        out_specs=pl.BlockSpec((tm, tn), lambda i, j: (i, j)),
        out_shape=jax.ShapeDtypeStruct((m, n), out_dtype),
        compiler_params=_cparams(("parallel", "parallel")),
        name="matmul_" + (act or "linear"),
    )(a, w, bias.reshape(1, n))


def _softmax_parts(parts):
    m = parts[0].max(axis=-1, keepdims=True)
    for s in parts[1:]:
        m = jnp.maximum(m, s.max(axis=-1, keepdims=True))
    ps = [jnp.exp(s - m) for s in parts]
    den = ps[0].sum(axis=-1, keepdims=True)
    for p in ps[1:]:
        den = den + p.sum(axis=-1, keepdims=True)
    return ps, den


ATTN_ROWS_PER_ITER = 8


def _attn_kernel(q_ref, k_ref, v_ref, bias_ref, o_ref, *, n_ctx, rows, kr, scale):
    kc = k_ref[0, 0:n_ctx, :]
    vc = v_ref[0, 0:n_ctx, :]
    (p,), den = _softmax_parts([_dot_nt(q_ref[0, 0:n_ctx, :], kc) * scale])
    o_ref[0, 0:n_ctx, :] = (_dot(p.astype(BF16), vc) / den).astype(BF16)

    group = next(g for g in (ATTN_ROWS_PER_ITER, 2, 1) if rows % g == 0)

    def row_group(i, carry):
        rr = [i * group + j for j in range(group)]
        rs = [jnp.clip(r - kr // 2, 0, rows - kr) for r in rr]
        q0 = [pl.multiple_of(n_ctx + r * GRID_W, GRID_W) for r in rr]
        k0 = [pl.multiple_of(n_ctx + s * GRID_W, GRID_W) for s in rs]
        scores = []
        for j in range(group):
            q = q_ref[0, pl.ds(q0[j], GRID_W), :]
            s_lat = _dot_nt(q, k_ref[0, pl.ds(k0[j], kr * GRID_W), :]) * scale + bias_ref[0, rr[j] - rs[j]]
            scores.append([s_lat, _dot_nt(q, kc) * scale])
        probs = [_softmax_parts(s) for s in scores]
        for j in range(group):
            (p_lat, p_ctx), den = probs[j]
            o = _dot(p_lat.astype(BF16), v_ref[0, pl.ds(k0[j], kr * GRID_W), :]) + _dot(p_ctx.astype(BF16), vc)
            o_ref[0, pl.ds(q0[j], GRID_W), :] = (o / den).astype(BF16)
        return carry

    lax.fori_loop(0, rows // group, row_group, 0)


def _na_bias_table(rpb, rows, kr):
    h, n_dr, n_dc = rpb.shape
    win_rows, win_cols = (n_dr + 1) // 2, (n_dc + 1) // 2
    col = np.arange(GRID_W)
    col_start = np.clip(col - win_cols // 2, 0, GRID_W - win_cols)
    col_mask = (col[None, :] >= col_start[:, None]) & (col[None, :] < col_start[:, None] + win_cols)
    dc_idx = np.clip(col[None, :] - col[:, None], 1 - win_cols, win_cols - 1) + win_cols - 1
    dr_idx = np.arange(kr)[None, :] - np.arange(kr)[:, None] + win_rows - 1
    t = rpb[:, dr_idx][:, :, :, dc_idx]
    t = jnp.where(col_mask[None, None, None], t.astype(F32), -jnp.inf)
    return t.transpose(0, 1, 3, 2, 4).reshape(h, kr, GRID_W, kr * GRID_W)


def _attention(qkv, bias, n_ctx, heads):
    b, l, d3 = qkv.shape
    d = d3 // 3
    dh = d // heads
    rows = (l - n_ctx) // GRID_W
    kr = bias.shape[1]
    kern = functools.partial(_attn_kernel, n_ctx=n_ctx, rows=rows, kr=kr, scale=dh ** -0.5)
    return pl.pallas_call(
        kern,
        grid=(heads, b),
        in_specs=[
            pl.BlockSpec((1, l, dh), lambda h, i: (i, 0, h)),
            pl.BlockSpec((1, l, dh), lambda h, i: (i, 0, heads + h)),
            pl.BlockSpec((1, l, dh), lambda h, i: (i, 0, 2 * heads + h)),
            pl.BlockSpec((1, kr, GRID_W, kr * GRID_W), lambda h, i: (h, 0, 0, 0)),
        ],
        out_specs=pl.BlockSpec((1, l, dh), lambda h, i: (i, 0, h)),
        out_shape=jax.ShapeDtypeStruct((b, l, d), BF16),
        compiler_params=_cparams(("parallel", "parallel")),
        name="na_attention",
    )(qkv, qkv, qkv, bias)


def _proj_kernel(a_ref, w_ref, b_ref, x_ref, tab_ref, g_ref, rw_ref, rb_ref,
                 x1_ref, hrow_ref, lg_ref, *, ctx_tiles, tile_off):
    is_ctx = pl.program_id(1) + tile_off < ctx_tiles
    y = _dot(a_ref[0], w_ref[...]) + b_ref[...]
    x1 = x_ref[0] + _tab_row(tab_ref, is_ctx, 2) * y
    x1_ref[0] = x1
    h2 = _norm_mod(x1, g_ref[...], _tab_row(tab_ref, is_ctx, 3), _tab_row(tab_ref, is_ctx, 4))
    lg_ref[...] = _dot3(h2, rw_ref[...]) + rb_ref[...]
    hrow_ref[...] = _to_token_rows(h2)


def _proj(a, w, bias, x, tab, g2, rw, rb, n_ctx, latent_only):
    b, l, d = x.shape
    tm = n_ctx
    off = n_ctx // tm if latent_only else 0
    nt = l // tm - off
    s_rows = d // LANES
    t = b * nt * tm
    kern = functools.partial(_proj_kernel, ctx_tiles=n_ctx // tm, tile_off=off)
    return pl.pallas_call(
        kern,
        grid=(b, nt),
        in_specs=[
            pl.BlockSpec((1, tm, d), lambda i, j: (i, j + off, 0)),
            pl.BlockSpec((d, d), lambda i, j: (0, 0)),
            pl.BlockSpec((1, d), lambda i, j: (0, 0)),
            pl.BlockSpec((1, tm, d), lambda i, j: (i, j + off, 0)),
            pl.BlockSpec((1, TAB_ROWS, d), lambda i, j: (i, 0, 0)),
            pl.BlockSpec((1, d), lambda i, j: (0, 0)),
            pl.BlockSpec((d, LANES), lambda i, j: (0, 0)),
            pl.BlockSpec((1, LANES), lambda i, j: (0, 0)),
        ],
        out_specs=[
            pl.BlockSpec((1, tm, d), lambda i, j: (i, j, 0)),
            pl.BlockSpec((tm * s_rows, LANES), lambda i, j: (i * nt + j, 0)),
            pl.BlockSpec((tm, LANES), lambda i, j: (i * nt + j, 0)),
        ],
        out_shape=[
            jax.ShapeDtypeStruct((b, nt * tm, d), F32),
            jax.ShapeDtypeStruct((t * s_rows, LANES), F32),
            jax.ShapeDtypeStruct((t, LANES), F32),
        ],
        compiler_params=_cparams(("parallel", "parallel")),
        name="mixer_proj",
    )(a, w, bias.reshape(1, d), x, tab, g2.reshape(1, d), rw, rb)


def _route_kernel(lg_ref, e_ref, g_ref, r_ref, cnt_ref, carry_ref, *, n_exp):
    @pl.when(pl.program_id(0) == 0)
    def _():
        carry_ref[...] = jnp.zeros_like(carry_ref)

    lg = lg_ref[...]
    tm = lg.shape[0]
    lane = lax.broadcasted_iota(jnp.int32, lg.shape, 1).astype(F32)
    cur = jnp.where(lane < n_exp, lg, -jnp.inf)
    multi = jnp.zeros(lg.shape, F32)
    vals, idxs = [], []
    for _ in range(TOP_K):
        m = cur.max(axis=-1, keepdims=True)
        idx = jnp.where(cur == m, lane, float(LANES)).min(axis=-1, keepdims=True)
        sel = lane == idx
        multi = jnp.where(sel, 1.0, multi)
        cur = jnp.where(sel, -jnp.inf, cur)
        vals.append(m)
        idxs.append(idx)
    exps = [jnp.exp(v - vals[0]) for v in vals]
    den = exps[0]
    for e in exps[1:]:
        den = den + e
    tri = (lax.broadcasted_iota(jnp.int32, (tm, tm), 0) > lax.broadcasted_iota(jnp.int32, (tm, tm), 1))
    pref = _dot(jnp.where(tri, 1.0, 0.0).astype(BF16), multi.astype(BF16))
    tot = carry_ref[...] + pref
    e_out = jnp.zeros(lg.shape, F32)
    g_out = jnp.zeros(lg.shape, F32)
    r_out = jnp.zeros(lg.shape, F32)
    for k in range(TOP_K):
        rank_k = jnp.where(lane == idxs[k], tot, 0.0).sum(axis=-1, keepdims=True)
        e_out = jnp.where(lane == k, idxs[k], e_out)
        g_out = jnp.where(lane == k, exps[k] / den, g_out)
        r_out = jnp.where(lane == k, rank_k, r_out)
    e_ref[...] = e_out.astype(jnp.int32)
    g_ref[...] = g_out
    r_ref[...] = r_out.astype(jnp.int32)
    carry_ref[...] = carry_ref[...] + multi.sum(axis=0, keepdims=True)
    cnt_ref[...] = carry_ref[...]


def _route(logits, n_exp):
    t = logits.shape[0]
    tm = _pick(t, 256)
    spec = pl.BlockSpec((tm, LANES), lambda i: (i, 0))
    return pl.pallas_call(
        functools.partial(_route_kernel, n_exp=n_exp),
        grid=(t // tm,),
        in_specs=[spec],
        out_specs=[spec, spec, spec, pl.BlockSpec((1, LANES), lambda i: (0, 0))],
        out_shape=[
            jax.ShapeDtypeStruct((t, LANES), jnp.int32),
            jax.ShapeDtypeStruct((t, LANES), F32),
            jax.ShapeDtypeStruct((t, LANES), jnp.int32),
            jax.ShapeDtypeStruct((1, LANES), F32),
        ],
        scratch_shapes=[pltpu.VMEM((1, LANES), F32)],
        compiler_params=_cparams(("arbitrary",)),
        name="router",
    )(logits)


def _dispatch_kernel(dest_ref, h_ref, xs_ref, sem, *, td, s_rows):
    def issue(t, carry):
        src = h_ref.at[pl.ds(pl.multiple_of(t * s_rows, s_rows), s_rows), :]
        for k in range(TOP_K):
            d = dest_ref[0, 0, t * TOP_K + k]
            dst = xs_ref.at[pl.ds(pl.multiple_of(d * s_rows, s_rows), s_rows), :]
            pltpu.make_async_copy(src, dst, sem).start()
        return carry

    lax.fori_loop(0, td, issue, 0)
    for _ in range(TOP_K):
        pltpu.make_async_copy(h_ref, xs_ref.at[pl.ds(0, td * s_rows), :], sem).wait()


def _dispatch(hrow, dest, s_rows):
    t = dest.shape[0] // TOP_K
    td = _pick(t, 256)
    nt = t // td
    return pl.pallas_call(
        functools.partial(_dispatch_kernel, td=td, s_rows=s_rows),
        grid=(nt,),
        in_specs=[
            pl.BlockSpec((1, 1, td * TOP_K), lambda i: (i, 0, 0), memory_space=pltpu.SMEM),
            pl.BlockSpec((td * s_rows, LANES), lambda i: (i, 0)),
        ],
        out_specs=pl.BlockSpec(memory_space=pl.ANY),
        out_shape=jax.ShapeDtypeStruct((t * TOP_K * s_rows, LANES), F32),
        scratch_shapes=[pltpu.SemaphoreType.DMA(())],
        compiler_params=pltpu.CompilerParams(dimension_semantics=("arbitrary",), vmem_limit_bytes=VMEM_LIMIT,
                                             has_side_effects=True),
        name="moe_dispatch",
    )(dest.reshape(nt, 1, td * TOP_K), hrow)


def _gmm_kernel(it_tile, it_e, it_lo, it_hi, it_valid,
                dst_ref, xs_ref, wg_ref, wl_ref, bg_ref, bl_ref, wd_ref, bd_ref, out_hbm,
                xb_ref, yrow_ref, sem, *, tg, s_rows, fk, n_items, trash_row):
    m = pl.program_id(0)
    tsub = _pick(tg, 128)
    f = wd_ref.shape[1]
    prev = jnp.maximum(m - 1, 0)
    prev_slot = (m + 1) % 2

    def scatter_prev():
        live = (m > 0) & (it_valid[prev] == 1)
        lo = jnp.where(live, it_lo[prev], 0)
        hi = jnp.where(live, it_hi[prev], 0)
        for r in range(tg):
            row = jnp.where((r >= lo) & (r < hi), dst_ref[0, 0, r], trash_row + r)
            pltpu.make_async_copy(
                yrow_ref.at[prev_slot, pl.ds(r * s_rows, s_rows), :],
                out_hbm.at[pl.ds(pl.multiple_of(row * s_rows, s_rows), s_rows), :],
                sem.at[prev_slot]).start()

    def wait_rows(slot):
        pltpu.make_async_copy(yrow_ref.at[slot], yrow_ref.at[slot], sem.at[slot]).wait()

    @pl.when(m == 0)
    def _():
        yrow_ref[...] = jnp.zeros_like(yrow_ref)

    @pl.when(m > 0)
    def _():
        wait_rows(m % 2)

    @pl.when(it_valid[m] == 0)
    def _():
        scatter_prev()

    @pl.when(it_valid[m] == 1)
    def _():
        scatter_prev()
        for t0 in range(0, tg, tsub):
            xt = _from_token_rows(xs_ref[pl.ds(t0 * s_rows, tsub * s_rows), :], s_rows)
            for s in range(s_rows):
                xb_ref[pl.ds(t0, tsub), s * LANES:(s + 1) * LANES] = xt[s].astype(BF16)

        x = xb_ref[...]
        y = None
        for f0 in range(0, f, fk):
            cols = slice(f0, f0 + fk)
            glu = jnp.minimum(_dot(x, wg_ref[0, :, cols]) + bg_ref[0, :, cols], SWIGLU_LIMIT)
            lin = jnp.clip(_dot(x, wl_ref[0, :, cols]) + bl_ref[0, :, cols], -SWIGLU_LIMIT, SWIGLU_LIMIT)
            act = glu * jax.nn.sigmoid(SWIGLU_ALPHA * glu) * (lin + 1.0)
            part = _dot(act.astype(BF16), wd_ref[0, cols, :])
            y = part if y is None else y + part
        y = y + bd_ref[0]

        for t0 in range(0, tg, tsub):
            yrow_ref[m % 2, pl.ds(t0 * s_rows, tsub * s_rows), :] = _to_token_rows(y[t0:t0 + tsub])

    @pl.when(m == n_items)
    def _():
        wait_rows(prev_slot)


def _gmm_items(counts, n_tiles, tg, max_items):
    n_exp = counts.shape[0]
    ends = jnp.cumsum(counts)
    starts = ends - counts
    def count_le(sorted_vals, q):
        return jnp.sum(sorted_vals[None, :] <= q[:, None], axis=1, dtype=jnp.int32)

    tile0 = jnp.arange(n_tiles, dtype=jnp.int32) * tg
    e_lo = jnp.minimum(count_le(ends, tile0), n_exp - 1)
    e_hi = jnp.minimum(count_le(ends, tile0 + tg - 1), n_exp - 1)
    n_items = e_hi - e_lo + 1
    item_end = jnp.cumsum(n_items)
    item_start = item_end - n_items
    total = item_end[-1]
    m = jnp.arange(max_items, dtype=jnp.int32)
    valid = m < total
    tile = jnp.minimum(count_le(item_end, m), n_tiles - 1)
    e = jnp.where(valid, e_lo[tile] + (m - item_start[tile]), e_hi[n_tiles - 1]).astype(jnp.int32)
    lo = jnp.clip(starts[e] - tile * tg, 0, tg).astype(jnp.int32)
    hi = jnp.clip(ends[e] - tile * tg, 0, tg).astype(jnp.int32)
    return tile, e, lo, hi, valid.astype(jnp.int32)


GMM_ROW_TILE = 256
GMM_F_CHUNK = 512


def _gmm(xs, counts, dst_rows, w, layer, s_rows):
    n_exp = counts.shape[0]
    d, f = w["wg"].shape[1:]
    p = xs.shape[0] // s_rows
    tg = _pick(p, GMM_ROW_TILE)
    fk = _pick(f, GMM_F_CHUNK, LANES)
    n_tiles = p // tg
    n_items = n_tiles + n_exp - 1
    tile, e, lo, hi, valid = _gmm_items(counts, n_tiles, tg, n_items + 1)
    items = (tile, e + layer * n_exp, lo, hi, valid)

    def wspec(block, buffers=1):
        return pl.BlockSpec(block, lambda m, t, e, lo, hi, va: (e[m], 0, 0), pipeline_mode=pl.Buffered(buffers))

    grid_spec = pltpu.PrefetchScalarGridSpec(
        num_scalar_prefetch=5,
        grid=(n_items + 1,),
        in_specs=[
            pl.BlockSpec((1, 1, tg), lambda m, t, e, lo, hi, va: (t[jnp.maximum(m - 1, 0)], 0, 0),
                         memory_space=pltpu.SMEM),
            pl.BlockSpec((tg * s_rows, LANES), lambda m, t, e, lo, hi, va: (t[m], 0)),
            wspec((1, d, f), 2),
            wspec((1, d, f), 2),
            wspec((1, 1, f), 2),
            wspec((1, 1, f), 2),
            wspec((1, f, d)),
            wspec((1, 1, d), 2),
        ],
        out_specs=pl.BlockSpec(memory_space=pl.ANY),
        scratch_shapes=[
            pltpu.VMEM((tg, d), BF16),
            pltpu.VMEM((2, tg * s_rows, LANES), F32),
            pltpu.SemaphoreType.DMA((2,)),
        ],
    )
    return pl.pallas_call(
        functools.partial(_gmm_kernel, tg=tg, s_rows=s_rows, fk=fk, n_items=n_items, trash_row=p),
        grid_spec=grid_spec,
        out_shape=jax.ShapeDtypeStruct(((p + tg) * s_rows, LANES), F32),
        compiler_params=pltpu.CompilerParams(dimension_semantics=("arbitrary",), vmem_limit_bytes=VMEM_LIMIT,
                                             has_side_effects=True),
        name="moe_experts",
    )(*items, dst_rows.reshape(n_tiles, 1, tg), xs, w["wg"], w["wl"], w["bg"], w["bl"], w["wd"], w["bd"])


def _combine_kernel(*refs, s_rows, ctx_tiles, mode):
    y_refs = refs[:TOP_K]
    gate_ref, x_ref, tab_ref, tabn_ref, g_ref, x2_ref, h_ref, moe_ref = refs[TOP_K:]
    gates = gate_ref[...]
    for k in range(TOP_K):
        chunks = _from_token_rows(y_refs[k][...], s_rows)
        for s in range(s_rows):
            term = chunks[s] * gates[:, k:k + 1]
            if k == 0:
                moe_ref[:, s * LANES:(s + 1) * LANES] = term
            else:
                moe_ref[:, s * LANES:(s + 1) * LANES] += term

    is_ctx = pl.program_id(1) < ctx_tiles
    x2 = x_ref[0] + _tab_row(tab_ref, is_ctx, 5) * moe_ref[...]
    x2_ref[0] = x2
    if mode == "next":
        h = _norm_mod(x2, g_ref[...], _tab_row(tabn_ref, is_ctx, 0), _tab_row(tabn_ref, is_ctx, 1))
        h_ref[0] = h.astype(h_ref.dtype)
    else:
        y = x2 * lax.rsqrt(jnp.mean(x2 * x2, axis=-1, keepdims=True) + RMS_EPS)
        h_ref[0] = (y * g_ref[...]).astype(h_ref.dtype)


def _combine(yk, gates, x1, tab, tab_next, g_next, n_ctx_rows, s_rows, mode):
    b, l, d = x1.shape
    tc = _pick(min(l, 128) if n_ctx_rows == 0 else n_ctx_rows, 128)
    nt = l // tc
    n_steps = b * nt
    kern = functools.partial(_combine_kernel, s_rows=s_rows, ctx_tiles=n_ctx_rows // tc, mode=mode)
    out_dtype = BF16 if mode == "next" else F32

    def y_spec(k):
        return pl.BlockSpec((tc * s_rows, LANES), lambda i, j: (k * n_steps + i * nt + j, 0))

    return pl.pallas_call(
        kern,
        grid=(b, nt),
        in_specs=[y_spec(k) for k in range(TOP_K)] + [
            pl.BlockSpec((tc, LANES), lambda i, j: (i * nt + j, 0)),
            pl.BlockSpec((1, tc, d), lambda i, j: (i, j, 0)),
            pl.BlockSpec((1, TAB_ROWS, d), lambda i, j: (i, 0, 0)),
            pl.BlockSpec((1, TAB_ROWS, d), lambda i, j: (i, 0, 0)),
            pl.BlockSpec((1, d), lambda i, j: (0, 0)),
        ],
        out_specs=[
            pl.BlockSpec((1, tc, d), lambda i, j: (i, j, 0)),
            pl.BlockSpec((1, tc, d), lambda i, j: (i, j, 0)),
        ],
        out_shape=[
            jax.ShapeDtypeStruct((b, l, d), F32),
            jax.ShapeDtypeStruct((b, l, d), out_dtype),
        ],
        scratch_shapes=[pltpu.VMEM((tc, d), F32)],
        compiler_params=_cparams(("parallel", "parallel")),
        name="moe_combine_" + mode,
    )(*([yk] * TOP_K), gates, x1, tab, tab_next, g_next.reshape(1, d))


def _moe(hrow, logits, x1, tab, tab_next, g_next, w, layer, n_exp, n_ctx_rows, mode):
    s_rows = x1.shape[-1] // LANES
    e_pad, g_pad, r_pad, cnt = _route(logits, n_exp)
    top_e, rank = e_pad[:, :TOP_K], r_pad[:, :TOP_K]
    t = top_e.shape[0]
    counts = cnt[0, :n_exp].astype(jnp.int32)
    starts = jnp.cumsum(counts) - counts
    dest = (starts[top_e] + rank).reshape(-1)
    tk = jnp.arange(t * TOP_K, dtype=jnp.int32)
    dst_rows = jnp.zeros((t * TOP_K,), jnp.int32).at[dest].set((tk % TOP_K) * t + tk // TOP_K)
    xs = _dispatch(hrow, dest, s_rows)
    yk = _gmm(xs, counts, dst_rows, w, layer, s_rows)
    return _combine(yk, g_pad, x1, tab, tab_next, g_next, n_ctx_rows, s_rows, mode)


def _conv_kernel(x_ref, w_ref, b_ref, o_ref, *, n_ctx):
    l = x_ref.shape[1]
    w = w_ref[...]
    for s0, n in ((0, n_ctx), (n_ctx, l - n_ctx)):
        x = x_ref[0, s0:s0 + n, :]
        row = lax.broadcasted_iota(jnp.int32, (n, 1), 0)
        acc = x * w[CONV_LEFT:CONV_LEFT + 1]
        for j in range(w.shape[0]):
            off = j - CONV_LEFT
            if off == 0:
                continue
            shifted = pltpu.roll(x, (-off) % n, 0)
            ok = (row + off >= 0) & (row + off < n)
            acc = acc + jnp.where(ok, shifted, 0.0) * w[j:j + 1]
        o_ref[0, s0:s0 + n, :] = acc + b_ref[...]


def _conv(xb, conv_w, conv_b, n_ctx):
    b, l, d = xb.shape
    dt = _pick(d, 256, LANES)
    return pl.pallas_call(
        functools.partial(_conv_kernel, n_ctx=n_ctx),
        grid=(b, d // dt),
        in_specs=[
            pl.BlockSpec((1, l, dt), lambda i, j: (i, 0, j)),
            pl.BlockSpec((conv_w.shape[0], dt), lambda i, j: (0, j)),
            pl.BlockSpec((1, dt), lambda i, j: (0, j)),
        ],
        out_specs=pl.BlockSpec((1, l, dt), lambda i, j: (i, 0, j)),
        out_shape=jax.ShapeDtypeStruct((b, l, d), F32),
        compiler_params=_cparams(("parallel", "parallel")),
        name="rg_conv",
    )(xb, conv_w, conv_b.reshape(1, d))


SCAN_PAD = 8
SCAN_UNROLL = 8


def _scan_kernel(*refs, nb, tc, nh, reverse):
    n_in = 8 if reverse else 6
    u_ref, wa_ref, ba_ref, wi_ref, bi_ref, lam_ref = refs[:6]
    o_ref = refs[n_in]
    scratch = refs[n_in + 1:]
    a_s, x_s, h_s = scratch[0:nh], scratch[nh:2 * nh], scratch[2 * nh:3 * nh]
    carry = scratch[3 * nh]
    ts = tc + SCAN_PAD

    @pl.when(pl.program_id(1) == 0)
    def _():
        carry[...] = jnp.zeros_like(carry)

    neg = -lam_ref[0]
    softplus = jnp.maximum(neg, 0.0) + jnp.log1p(jnp.exp(-jnp.abs(neg)))
    for bi in range(nb):
        u = u_ref[bi]
        ub = u.astype(BF16)
        r = jax.nn.sigmoid(_dot(ub, wa_ref[0, 0]) + ba_ref[0, 0])
        i = jax.nn.sigmoid(_dot(ub, wi_ref[0, 0]) + bi_ref[0, 0])
        log_a = (-RG_C) * r * softplus
        a = jnp.exp(log_a)
        xin = jnp.sqrt(1.0 - a * a) * (i * u)
        for p in range(nh):
            a_s[p][pl.ds(bi * ts, tc), :] = a[:, p * LANES:(p + 1) * LANES]
            x_s[p][pl.ds(bi * ts, tc), :] = xin[:, p * LANES:(p + 1) * LANES]

    def block(j, hs):
        hs = list(hs)
        for q in range(SCAN_UNROLL):
            t = j * SCAN_UNROLL + q
            if reverse:
                t = tc - 1 - t
            for p in range(nh):
                hs[p] = a_s[p][pl.ds(t, nb, stride=ts), :] * hs[p] + x_s[p][pl.ds(t, nb, stride=ts), :]
                h_s[p][pl.ds(t, nb, stride=ts), :] = hs[p]
        return tuple(hs)

    h0 = tuple(carry[:, p * LANES:(p + 1) * LANES] for p in range(nh))
    hs = lax.fori_loop(0, tc // SCAN_UNROLL, block, h0)
    for p in range(nh):
        carry[:, p * LANES:(p + 1) * LANES] = hs[p]
    for bi in range(nb):
        for p in range(nh):
            h = h_s[p][pl.ds(bi * ts, tc), :]
            cols = slice(p * LANES, (p + 1) * LANES)
            if reverse:
                hf_ref, gy_ref = refs[6], refs[7]
                o_ref[bi, :, cols] = ((hf_ref[bi, :, cols] + h) * gy_ref[bi, :, cols]).astype(o_ref.dtype)
            else:
                o_ref[bi, :, cols] = h


def _scan(u, w_a, b_a, w_i, b_i, lam, dirn, n_ctx, h_fwd=None, gy=None):
    b, l, d = u.shape
    n_blk, w = w_a.shape[1], w_a.shape[2]
    tc = _pick(n_ctx, 128)
    nt, nc = l // tc, n_ctx // tc
    reverse = dirn == 1

    def chunk(j):
        if not reverse:
            return j
        return jnp.where(j < nc, nc - 1 - j, nt - 1 - (j - nc))

    blk = pl.BlockSpec((b, tc, w), lambda g, j: (0, chunk(j), g))
    in_specs = [
        blk,
        pl.BlockSpec((1, 1, w, w), lambda g, j: (dirn, g, 0, 0)),
        pl.BlockSpec((1, 1, 1, w), lambda g, j: (dirn, g, 0, 0)),
        pl.BlockSpec((1, 1, w, w), lambda g, j: (dirn, g, 0, 0)),
        pl.BlockSpec((1, 1, 1, w), lambda g, j: (dirn, g, 0, 0)),
        pl.BlockSpec((1, 1, w), lambda g, j: (dirn, 0, g)),
    ]
    args = [u, w_a, b_a.reshape(2, n_blk, 1, w), w_i, b_i.reshape(2, n_blk, 1, w), lam.reshape(2, 1, d)]
    if reverse:
        in_specs += [blk, blk]
        args += [h_fwd, gy]
    ts = tc + SCAN_PAD
    return pl.pallas_call(
        functools.partial(_scan_kernel, nb=b, tc=tc, nh=w // LANES, reverse=reverse),
        grid=(n_blk, nt),
        in_specs=in_specs,
        out_specs=blk,
        out_shape=jax.ShapeDtypeStruct((b, l, d), BF16 if reverse else F32),
        scratch_shapes=[pltpu.VMEM((b * ts, LANES), F32)] * (3 * (w // LANES)) + [pltpu.VMEM((b, w), F32)],
        compiler_params=_cparams(("parallel", "arbitrary")),
        name="rg_scan_" + ("rev" if reverse else "fwd"),
    )(*args)


def _mod_tables(mod_out, b, d):
    tabs = []
    for i in range(mod_out.shape[0]):
        ml = mod_out[i, :b].reshape(b, MOD_ROWS, d)
        mc = jnp.broadcast_to(mod_out[i, b].reshape(1, MOD_ROWS, d), (b, MOD_ROWS, d))
        pad = jnp.zeros((b, TAB_ROWS - 2 * MOD_ROWS, d), F32)
        tabs.append(jnp.concatenate([mc, ml, pad], axis=1))
    return tabs


def _split_gu_kernel(w_ref, perm_ref, wg_ref, wl_ref):
    for j in range(w_ref.shape[1] // (2 * LANES)):
        y = _dot(w_ref[:, 2 * LANES * j:2 * LANES * (j + 1)].astype(BF16), perm_ref[...])
        wg_ref[:, LANES * j:LANES * (j + 1)] = y[:, :LANES].astype(BF16)
        wl_ref[:, LANES * j:LANES * (j + 1)] = y[:, LANES:].astype(BF16)


def _split_gu(w_gu):
    lead, n2 = w_gu.shape[:-1], w_gu.shape[-1]
    rows = int(np.prod(lead))
    tr = _pick(rows, 512)
    src = np.concatenate([np.arange(0, 2 * LANES, 2), np.arange(1, 2 * LANES, 2)])
    perm = jnp.asarray(np.arange(2 * LANES)[:, None] == src[None, :], BF16)
    out_spec = pl.BlockSpec((tr, n2 // 2), lambda i: (i, 0))
    wg, wl = pl.pallas_call(
        _split_gu_kernel,
        grid=(rows // tr,),
        in_specs=[pl.BlockSpec((tr, n2), lambda i: (i, 0)), pl.BlockSpec((2 * LANES, 2 * LANES), lambda i: (0, 0))],
        out_specs=[out_spec, out_spec],
        out_shape=[jax.ShapeDtypeStruct((rows, n2 // 2), BF16)] * 2,
        compiler_params=_cparams(("parallel",)),
        name="split_gu_weights",
    )(w_gu.reshape(rows, n2), perm)
    return wg.reshape(*lead, n2 // 2), wl.reshape(*lead, n2 // 2)


def _moe_weights(router_w, router_b, w_gu, b_gu, w_dn, b_dn):
    depth, d, n_exp = router_w.shape
    f = w_dn.shape[2]
    wg, wl = _split_gu(w_gu)
    return {
        "rw": jnp.pad(router_w, ((0, 0), (0, 0), (0, LANES - n_exp))),
        "rb": jnp.pad(router_b, ((0, 0), (0, LANES - n_exp))).reshape(depth, 1, LANES),
        "wg": wg.reshape(depth * n_exp, d, f),
        "wl": wl.reshape(depth * n_exp, d, f),
        "bg": b_gu[..., 0::2].reshape(depth * n_exp, 1, f),
        "bl": b_gu[..., 1::2].reshape(depth * n_exp, 1, f),
        "wd": w_dn.astype(BF16).reshape(depth * n_exp, f, d),
        "bd": b_dn.reshape(depth * n_exp, 1, d),
    }


def kernel(x, c, ctx, c_ctx, mod_w, mod_b, norm1_g, norm2_g, final_g, na_w_qkv, na_w_o, na_rpb, rg_w_y, rg_b_y,
           rg_w_x, rg_b_x, rg_conv_w, rg_conv_b, rg_w_a, rg_b_a, rg_w_i, rg_b_i, rg_lam, rg_w_out, rg_b_out,
           moe_router_w, moe_router_b, moe_w_gu, moe_b_gu, moe_w_dn, moe_b_dn):
    b, n_lat, d = x.shape
    n_ctx = ctx.shape[1]
    l = n_ctx + n_lat
    heads = na_rpb.shape[1]
    rows = n_lat // GRID_W
    kr = min((na_rpb.shape[2] + 1) // 2, rows)
    assert mod_w.shape[0] == 2 and n_lat % n_ctx == 0 and d % LANES == 0

    mod_rows = -(-(b + 1) // 8) * 8
    cc = jnp.concatenate([c, c_ctx[None, :], jnp.zeros((mod_rows - b - 1, d), F32)], axis=0)
    tab0, tab1 = _mod_tables(_modulation(cc, mod_w, mod_b), b, d)

    xs0 = jnp.concatenate([ctx, x], axis=1)

    h = _prenorm(xs0, tab0, norm1_g[0], n_ctx)
    qkv = _matmul(h.reshape(b * l, d), na_w_qkv[0].astype(BF16), jnp.zeros((3 * d,), F32), BF16)
    bias = _na_bias_table(na_rpb[0], rows, kr)
    o = _attention(qkv.reshape(b, l, 3 * d), bias, n_ctx, heads)
    n_exp = moe_router_w.shape[2]
    w = _moe_weights(moe_router_w, moe_router_b, moe_w_gu, moe_b_gu, moe_w_dn, moe_b_dn)
    x1, hrow, logits = _proj(o, na_w_o[0].astype(BF16), jnp.zeros((d,), F32), xs0, tab0, norm2_g[0],
                             w["rw"][0], w["rb"][0], n_ctx, latent_only=False)
    x2, h = _moe(hrow, logits, x1, tab0, tab1, norm1_g[1], w, 0, n_exp, n_ctx, "next")

    hf = h.reshape(b * l, d)
    xb = _matmul(hf, rg_w_x[0].astype(BF16), rg_b_x[0], F32).reshape(b, l, d)
    gy = _matmul(hf, rg_w_y[0].astype(BF16), rg_b_y[0], F32, act="gelu").reshape(b, l, d)
    u = _conv(xb, rg_conv_w[0], rg_conv_b[0], n_ctx)
    w_a, w_i = rg_w_a[0].astype(BF16), rg_w_i[0].astype(BF16)
    h_fwd = _scan(u, w_a, rg_b_a[0], w_i, rg_b_i[0], rg_lam[0], 0, n_ctx)
    hg = _scan(u, w_a, rg_b_a[0], w_i, rg_b_i[0], rg_lam[0], 1, n_ctx, h_fwd, gy)
    x1, hrow, logits = _proj(hg, rg_w_out[0].astype(BF16), rg_b_out[0], x2, tab1, norm2_g[1],
                             w["rw"][1], w["rb"][1], n_ctx, latent_only=True)
    _, out = _moe(hrow, logits, x1, tab1, tab1, final_g, w, 1, n_exp, 0, "final")
    return out
```

```python
import functools

import jax
import jax.numpy as jnp
import numpy as np
from jax import lax
from jax.experimental import pallas as pl
from jax.experimental.pallas import tpu as pltpu

F32 = jnp.float32
BF16 = jnp.bfloat16

LANES = 128
GRID_W = 64
TOP_K = 4
RG_C = 8.0
CONV_LEFT = 2
SWIGLU_ALPHA = 1.702
SWIGLU_LIMIT = 7.0
RMS_EPS = 1e-6
MOD_ROWS = 6
TAB_ROWS = 16
VMEM_LIMIT = 56 * 1024 * 1024


def _cparams(sem):
    return pltpu.CompilerParams(dimension_semantics=sem, vmem_limit_bytes=VMEM_LIMIT)


def _pick(n, pref, mult=8):
    for t in range(min(pref, n), 0, -1):
        if n % t == 0 and t % mult == 0:
            return t
    return n


def _dot(a, b):
    return jnp.dot(a, b, preferred_element_type=F32)


def _dot_nt(a, b):
    return lax.dot_general(a, b, (((1,), (1,)), ((), ())), preferred_element_type=F32)


def _split_bf16(x):
    hi = x.astype(BF16)
    lo = (x - hi.astype(F32)).astype(BF16)
    return hi, lo


def _dot3(a, w):
    a_hi, a_lo = _split_bf16(a)
    w_hi, w_lo = _split_bf16(w)
    return _dot(a_hi, w_hi) + _dot(a_lo, w_hi) + _dot(a_hi, w_lo)


def _norm_mod(x, g, shift, scale):
    y = x * lax.rsqrt(jnp.mean(x * x, axis=-1, keepdims=True) + RMS_EPS)
    return (y * g) * (1.0 + scale) + shift


def _to_token_rows(v):
    t, d = v.shape
    s_rows = d // LANES
    chunks = jnp.stack([v[:, s * LANES:(s + 1) * LANES] for s in range(s_rows)], axis=0)
    return pltpu.einshape("stl->tsl", chunks).reshape(t * s_rows, LANES)


def _from_token_rows(r, s_rows):
    return pltpu.einshape("tsl->stl", r.reshape(r.shape[0] // s_rows, s_rows, LANES))


def _tab_row(tab_ref, is_ctx, k):
    base = jnp.where(is_ctx, 0, MOD_ROWS)
    return tab_ref[0, pl.ds(base + k, 1), :]


def _mod_kernel(a_ref, w_ref, b_ref, o_ref):
    a = a_ref[...]
    a = a * jax.nn.sigmoid(a)
    o_ref[0] = _dot3(a, w_ref[0]) + b_ref[0]


def _modulation(cc, mod_w, mod_b):
    depth, d, n = mod_w.shape
    r = cc.shape[0]
    tn = _pick(n, 1024, LANES)
    return pl.pallas_call(
        _mod_kernel,
        grid=(depth, n // tn),
        in_specs=[
            pl.BlockSpec((r, d), lambda i, j: (0, 0)),
            pl.BlockSpec((1, d, tn), lambda i, j: (i, 0, j)),
            pl.BlockSpec((1, 1, tn), lambda i, j: (i, 0, j)),
        ],
        out_specs=pl.BlockSpec((1, r, tn), lambda i, j: (i, 0, j)),
        out_shape=jax.ShapeDtypeStruct((depth, r, n), F32),
        compiler_params=_cparams(("parallel", "parallel")),
        name="modulation",
    )(cc, mod_w, mod_b.reshape(depth, 1, n))


def _prenorm_kernel(ctx_ref, x_ref, tab_ref, g_ref, o_ref):
    is_ctx = pl.program_id(1) == 0
    xin = jnp.where(is_ctx, ctx_ref[0], x_ref[0])
    h = _norm_mod(xin, g_ref[...], _tab_row(tab_ref, is_ctx, 0), _tab_row(tab_ref, is_ctx, 1))
    o_ref[0] = h.astype(BF16)


def _prenorm(ctx, x, tab, g):
    b, n_ctx, d = ctx.shape
    tm = n_ctx
    nt = 1 + x.shape[1] // tm
    return pl.pallas_call(
        _prenorm_kernel,
        grid=(b, nt),
        in_specs=[
            pl.BlockSpec((1, tm, d), lambda i, j: (i, 0, 0)),
            pl.BlockSpec((1, tm, d), lambda i, j: (i, jnp.maximum(j - 1, 0), 0)),
            pl.BlockSpec((1, TAB_ROWS, d), lambda i, j: (i, 0, 0)),
            pl.BlockSpec((1, d), lambda i, j: (0, 0)),
        ],
        out_specs=pl.BlockSpec((1, tm, d), lambda i, j: (i, j, 0)),
        out_shape=jax.ShapeDtypeStruct((b, nt * tm, d), BF16),
        compiler_params=_cparams(("parallel", "parallel")),
        name="prenorm",
    )(ctx, x, tab, g.reshape(1, d))


def _gelu_tanh(x):
    return 0.5 * x * (1.0 + jnp.tanh(np.sqrt(2.0 / np.pi) * (x + 0.044715 * (x * x * x))))


def _matmul_kernel(a_ref, w_ref, b_ref, o_ref, *, act):
    y = _dot(a_ref[...], w_ref[...]) + b_ref[...]
    if act == "gelu":
        y = _gelu_tanh(y)
    o_ref[...] = y.astype(o_ref.dtype)


def _matmul(a, w, bias, out_dtype, act=None):
    m, k = a.shape
    n = w.shape[1]
    tm = _pick(m, 1024)
    tn = _pick(n, 512, LANES)
    return pl.pallas_call(
        functools.partial(_matmul_kernel, act=act),
        grid=(m // tm, n // tn),
        in_specs=[
            pl.BlockSpec((tm, k), lambda i, j: (i, 0)),
            pl.BlockSpec((k, tn), lambda i, j: (0, j)),
            pl.BlockSpec((1, tn), lambda i, j: (0, j)),
        ],
        out_specs=pl.BlockSpec((tm, tn), lambda i, j: (i, j)),
        out_shape=jax.ShapeDtypeStruct((m, n), out_dtype),
        compiler_params=_cparams(("parallel", "parallel")),
        name="matmul_" + (act or "linear"),
    )(a, w, bias.reshape(1, n))


def _softmax_parts(parts):
    m = parts[0].max(axis=-1, keepdims=True)
    for s in parts[1:]:
        m = jnp.maximum(m, s.max(axis=-1, keepdims=True))
    ps = [jnp.exp(s - m) for s in parts]
    den = ps[0].sum(axis=-1, keepdims=True)
    for p in ps[1:]:
        den = den + p.sum(axis=-1, keepdims=True)
    return ps, den


ATTN_ROWS_PER_ITER = 16


def _attn_kernel(q_ref, k_ref, v_ref, bias_ref, o_ref, *, n_ctx, rows, kr, scale):
    kc = k_ref[0, 0:n_ctx, :]
    vc = v_ref[0, 0:n_ctx, :]
    (p,), den = _softmax_parts([_dot_nt(q_ref[0, 0:n_ctx, :], kc) * scale])
    o_ref[0, 0:n_ctx, :] = (_dot(p.astype(BF16), vc) / den).astype(BF16)

    group = next(g for g in (ATTN_ROWS_PER_ITER, 2, 1) if rows % g == 0)

    def row_group(i, carry):
        rr = [i * group + j for j in range(group)]
        rs = [jnp.clip(r - kr // 2, 0, rows - kr) for r in rr]
        q0 = [pl.multiple_of(n_ctx + r * GRID_W, GRID_W) for r in rr]
        k0 = [pl.multiple_of(n_ctx + s * GRID_W, GRID_W) for s in rs]
        scores = []
        for j in range(group):
            q = q_ref[0, pl.ds(q0[j], GRID_W), :]
            s_lat = _dot_nt(q, k_ref[0, pl.ds(k0[j], kr * GRID_W), :]) * scale + bias_ref[0, rr[j] - rs[j]]
            scores.append([s_lat, _dot_nt(q, kc) * scale])
        probs = [_softmax_parts(s) for s in scores]
        for j in range(group):
            (p_lat, p_ctx), den = probs[j]
            o = _dot(p_lat.astype(BF16), v_ref[0, pl.ds(k0[j], kr * GRID_W), :]) + _dot(p_ctx.astype(BF16), vc)
            o_ref[0, pl.ds(q0[j], GRID_W), :] = (o / den).astype(BF16)
        return carry

    lax.fori_loop(0, rows // group, row_group, 0)


def _na_bias_table(rpb, rows, kr):
    h, n_dr, n_dc = rpb.shape
    win_rows, win_cols = (n_dr + 1) // 2, (n_dc + 1) // 2
    col = np.arange(GRID_W)
    col_start = np.clip(col - win_cols // 2, 0, GRID_W - win_cols)
    col_mask = (col[None, :] >= col_start[:, None]) & (col[None, :] < col_start[:, None] + win_cols)
    dc_idx = np.clip(col[None, :] - col[:, None], 1 - win_cols, win_cols - 1) + win_cols - 1
    dr_idx = np.arange(kr)[None, :] - np.arange(kr)[:, None] + win_rows - 1
    t = rpb[:, dr_idx][:, :, :, dc_idx]
    t = jnp.where(col_mask[None, None, None], t.astype(F32), -jnp.inf)
    return t.transpose(0, 1, 3, 2, 4).reshape(h, kr, GRID_W, kr * GRID_W)


def _attention(qkv, bias, n_ctx, heads):
    b, l, d3 = qkv.shape
    d = d3 // 3
    dh = d // heads
    rows = (l - n_ctx) // GRID_W
    kr = bias.shape[1]
    kern = functools.partial(_attn_kernel, n_ctx=n_ctx, rows=rows, kr=kr, scale=dh ** -0.5)
    return pl.pallas_call(
        kern,
        grid=(heads, b),
        in_specs=[
            pl.BlockSpec((1, l, dh), lambda h, i: (i, 0, h)),
            pl.BlockSpec((1, l, dh), lambda h, i: (i, 0, heads + h)),
            pl.BlockSpec((1, l, dh), lambda h, i: (i, 0, 2 * heads + h)),
            pl.BlockSpec((1, kr, GRID_W, kr * GRID_W), lambda h, i: (h, 0, 0, 0)),
        ],
        out_specs=pl.BlockSpec((1, l, dh), lambda h, i: (i, 0, h)),
        out_shape=jax.ShapeDtypeStruct((b, l, d), BF16),
        compiler_params=_cparams(("parallel", "parallel")),
        name="na_attention",
    )(qkv, qkv, qkv, bias)


def _proj_kernel(*refs, ctx_tiles, tile_off, split_residual):
    if split_residual:
        a_ref, w_ref, b_ref, ctx_ref, x_ref, tab_ref, g_ref, rw_ref, rb_ref, x1_ref, hrow_ref, lg_ref = refs
    else:
        a_ref, w_ref, b_ref, x_ref, tab_ref, g_ref, rw_ref, rb_ref, x1_ref, hrow_ref, lg_ref = refs
    is_ctx = pl.program_id(1) + tile_off < ctx_tiles
    resid = jnp.where(is_ctx, ctx_ref[0], x_ref[0]) if split_residual else x_ref[0]
    y = _dot(a_ref[0], w_ref[...]) + b_ref[...]
    x1 = resid + _tab_row(tab_ref, is_ctx, 2) * y
    x1_ref[0] = x1
    h2 = _norm_mod(x1, g_ref[...], _tab_row(tab_ref, is_ctx, 3), _tab_row(tab_ref, is_ctx, 4))
    lg_ref[...] = _dot(h2.astype(BF16), rw_ref[...]) + rb_ref[...]
    hrow_ref[...] = _to_token_rows(h2)


def _proj(a, w, bias, resid, tab, g2, rw, rb, n_ctx, latent_only):
    split = isinstance(resid, tuple)
    b, l, d = a.shape
    tm = n_ctx
    off = n_ctx // tm if latent_only else 0
    nt = l // tm - off
    s_rows = d // LANES
    t = b * nt * tm
    kern = functools.partial(_proj_kernel, ctx_tiles=n_ctx // tm, tile_off=off, split_residual=split)
    if split:
        assert not latent_only and n_ctx == tm
        resid_specs = [pl.BlockSpec((1, tm, d), lambda i, j: (i, 0, 0)),
                       pl.BlockSpec((1, tm, d), lambda i, j: (i, jnp.maximum(j - 1, 0), 0))]
        resid_args = list(resid)
    else:
        resid_specs = [pl.BlockSpec((1, tm, d), lambda i, j: (i, j + off, 0))]
        resid_args = [resid]
    return pl.pallas_call(
        kern,
        grid=(b, nt),
        in_specs=[
            pl.BlockSpec((1, tm, d), lambda i, j: (i, j + off, 0)),
            pl.BlockSpec((d, d), lambda i, j: (0, 0)),
            pl.BlockSpec((1, d), lambda i, j: (0, 0)),
        ] + resid_specs + [
            pl.BlockSpec((1, TAB_ROWS, d), lambda i, j: (i, 0, 0)),
            pl.BlockSpec((1, d), lambda i, j: (0, 0)),
            pl.BlockSpec((d, LANES), lambda i, j: (0, 0)),
            pl.BlockSpec((1, LANES), lambda i, j: (0, 0)),
        ],
        out_specs=[
            pl.BlockSpec((1, tm, d), lambda i, j: (i, j, 0)),
            pl.BlockSpec((tm * s_rows, LANES), lambda i, j: (i * nt + j, 0)),
            pl.BlockSpec((tm, LANES), lambda i, j: (i * nt + j, 0)),
        ],
        out_shape=[
            jax.ShapeDtypeStruct((b, nt * tm, d), F32),
            jax.ShapeDtypeStruct((t * s_rows, LANES), F32),
            jax.ShapeDtypeStruct((t, LANES), F32),
        ],
        compiler_params=_cparams(("parallel", "parallel")),
        name="mixer_proj",
    )(a, w, bias.reshape(1, d), *resid_args, tab, g2.reshape(1, d), rw, rb)


def _route_kernel(lg_ref, e_ref, g_ref, r_ref, cnt_ref, carry_ref, *, n_exp):
    @pl.when(pl.program_id(0) == 0)
    def _():
        carry_ref[...] = jnp.zeros_like(carry_ref)

    lg = lg_ref[...]
    tm = lg.shape[0]
    lane = lax.broadcasted_iota(jnp.int32, lg.shape, 1).astype(F32)
    cur = jnp.where(lane < n_exp, lg, -jnp.inf)
    multi = jnp.zeros(lg.shape, F32)
    vals, idxs = [], []
    for _ in range(TOP_K):
        m = cur.max(axis=-1, keepdims=True)
        idx = jnp.where(cur == m, lane, float(LANES)).min(axis=-1, keepdims=True)
        sel = lane == idx
        multi = jnp.where(sel, 1.0, multi)
        cur = jnp.where(sel, -jnp.inf, cur)
        vals.append(m)
        idxs.append(idx)
    exps = [jnp.exp(v - vals[0]) for v in vals]
    den = exps[0]
    for e in exps[1:]:
        den = den + e
    tri = (lax.broadcasted_iota(jnp.int32, (tm, tm), 0) > lax.broadcasted_iota(jnp.int32, (tm, tm), 1))
    pref = _dot(jnp.where(tri, 1.0, 0.0).astype(BF16), multi.astype(BF16))
    tot = carry_ref[...] + pref
    e_out = jnp.zeros(lg.shape, F32)
    g_out = jnp.zeros(lg.shape, F32)
    r_out = jnp.zeros(lg.shape, F32)
    for k in range(TOP_K):
        rank_k = jnp.where(lane == idxs[k], tot, 0.0).sum(axis=-1, keepdims=True)
        e_out = jnp.where(lane == k, idxs[k], e_out)
        g_out = jnp.where(lane == k, exps[k] / den, g_out)
        r_out = jnp.where(lane == k, rank_k, r_out)
    e_ref[...] = e_out.astype(jnp.int32)
    g_ref[...] = g_out
    r_ref[...] = r_out.astype(jnp.int32)
    carry_ref[...] = carry_ref[...] + multi.sum(axis=0, keepdims=True)
    cnt_ref[...] = carry_ref[...]


def _route(logits, n_exp):
    t = logits.shape[0]
    tm = _pick(t, 256)
    spec = pl.BlockSpec((tm, LANES), lambda i: (i, 0))
    return pl.pallas_call(
        functools.partial(_route_kernel, n_exp=n_exp),
        grid=(t // tm,),
        in_specs=[spec],
        out_specs=[spec, spec, spec, pl.BlockSpec((1, LANES), lambda i: (0, 0))],
        out_shape=[
            jax.ShapeDtypeStruct((t, LANES), jnp.int32),
            jax.ShapeDtypeStruct((t, LANES), F32),
            jax.ShapeDtypeStruct((t, LANES), jnp.int32),
            jax.ShapeDtypeStruct((1, LANES), F32),
        ],
        scratch_shapes=[pltpu.VMEM((1, LANES), F32)],
        compiler_params=_cparams(("arbitrary",)),
        name="router",
    )(logits)


def _dispatch_kernel(dest_ref, h_ref, xs_ref, sem, *, td, s_rows):
    def issue(t, carry):
        src = h_ref.at[pl.ds(pl.multiple_of(t * s_rows, s_rows), s_rows), :]
        for k in range(TOP_K):
            d = dest_ref[0, 0, t * TOP_K + k]
            dst = xs_ref.at[pl.ds(pl.multiple_of(d * s_rows, s_rows), s_rows), :]
            pltpu.make_async_copy(src, dst, sem).start()
        return carry

    lax.fori_loop(0, td, issue, 0)
    for _ in range(TOP_K):
        pltpu.make_async_copy(h_ref, xs_ref.at[pl.ds(0, td * s_rows), :], sem).wait()


def _dispatch(hrow, dest, s_rows):
    t = dest.shape[0] // TOP_K
    td = _pick(t, 256)
    nt = t // td
    return pl.pallas_call(
        functools.partial(_dispatch_kernel, td=td, s_rows=s_rows),
        grid=(nt,),
        in_specs=[
            pl.BlockSpec((1, 1, td * TOP_K), lambda i: (i, 0, 0), memory_space=pltpu.SMEM),
            pl.BlockSpec((td * s_rows, LANES), lambda i: (i, 0)),
        ],
        out_specs=pl.BlockSpec(memory_space=pl.ANY),
        out_shape=jax.ShapeDtypeStruct((t * TOP_K * s_rows, LANES), F32),
        scratch_shapes=[pltpu.SemaphoreType.DMA(())],
        compiler_params=pltpu.CompilerParams(dimension_semantics=("arbitrary",), vmem_limit_bytes=VMEM_LIMIT,
                                             has_side_effects=True),
        name="moe_dispatch",
    )(dest.reshape(nt, 1, td * TOP_K), hrow)


def _gmm_kernel(it_tile, it_e, it_lo, it_hi, it_first, it_valid,
                xs_ref, wg_ref, wl_ref, bg_ref, bl_ref, wd_ref, bd_ref, ys_ref, xb_ref, *, tg, s_rows, fk):
    m = pl.program_id(0)
    tsub = _pick(tg, 128)
    f = wd_ref.shape[1]

    @pl.when(it_valid[m] == 1)
    def _():
        for t0 in range(0, tg, tsub):
            xt = _from_token_rows(xs_ref[pl.ds(t0 * s_rows, tsub * s_rows), :], s_rows)
            for s in range(s_rows):
                xb_ref[pl.ds(t0, tsub), s * LANES:(s + 1) * LANES] = xt[s].astype(BF16)

        x = xb_ref[...]
        y = None
        for f0 in range(0, f, fk):
            cols = slice(f0, f0 + fk)
            glu = jnp.minimum(_dot(x, wg_ref[0, :, cols]) + bg_ref[0, :, cols], SWIGLU_LIMIT)
            lin = jnp.clip(_dot(x, wl_ref[0, :, cols]) + bl_ref[0, :, cols], -SWIGLU_LIMIT, SWIGLU_LIMIT)
            act = glu * jax.nn.sigmoid(SWIGLU_ALPHA * glu) * (lin + 1.0)
            part = _dot(act.astype(BF16), wd_ref[0, cols, :])
            y = part if y is None else y + part
        y = y + bd_ref[0]

        for t0 in range(0, tg, tsub):
            rows = pl.ds(t0 * s_rows, tsub * s_rows)
            yr = _to_token_rows(y[t0:t0 + tsub])
            row = t0 * s_rows + lax.broadcasted_iota(jnp.int32, (tsub * s_rows, 1), 0)
            mine = (row >= it_lo[m] * s_rows) & (row < it_hi[m] * s_rows)

            @pl.when(it_first[m] == 1)
            def _():
                ys_ref[rows, :] = jnp.where(mine, yr, 0.0)

            @pl.when(it_first[m] == 0)
            def _():
                ys_ref[rows, :] = jnp.where(mine, yr, ys_ref[rows, :])


def _gmm_items(counts, n_tiles, tg, max_items):
    n_exp = counts.shape[0]
    ends = jnp.cumsum(counts)
    starts = ends - counts
    def count_le(sorted_vals, q):
        return jnp.sum(sorted_vals[None, :] <= q[:, None], axis=1, dtype=jnp.int32)

    tile0 = jnp.arange(n_tiles, dtype=jnp.int32) * tg
    e_lo = jnp.minimum(count_le(ends, tile0), n_exp - 1)
    e_hi = jnp.minimum(count_le(ends, tile0 + tg - 1), n_exp - 1)
    n_items = e_hi - e_lo + 1
    item_end = jnp.cumsum(n_items)
    item_start = item_end - n_items
    total = item_end[-1]
    m = jnp.arange(max_items, dtype=jnp.int32)
    valid = m < total
    tile = jnp.minimum(count_le(item_end, m), n_tiles - 1)
    e = jnp.where(valid, e_lo[tile] + (m - item_start[tile]), e_hi[n_tiles - 1]).astype(jnp.int32)
    lo = jnp.clip(starts[e] - tile * tg, 0, tg).astype(jnp.int32)
    hi = jnp.clip(ends[e] - tile * tg, 0, tg).astype(jnp.int32)
    first = (m == item_start[tile]).astype(jnp.int32)
    return tile, e, lo, hi, first, valid.astype(jnp.int32)


GMM_ROW_TILE = 256
GMM_F_CHUNK = 1024


def _gmm(xs, counts, w, layer, s_rows):
    n_exp = counts.shape[0]
    d, f = w["wg"].shape[1:]
    p = xs.shape[0] // s_rows
    tg = _pick(p, GMM_ROW_TILE)
    fk = _pick(f, GMM_F_CHUNK, LANES)
    n_tiles = p // tg
    max_items = n_tiles + n_exp - 1
    tile, e, lo, hi, first, valid = _gmm_items(counts, n_tiles, tg, max_items)
    items = (tile, e + layer * n_exp, lo, hi, first, valid)

    def wspec(block, buffers=1):
        return pl.BlockSpec(block, lambda m, t, e, lo, hi, fi, va: (e[m], 0, 0), pipeline_mode=pl.Buffered(buffers))

    rows_spec = pl.BlockSpec((tg * s_rows, LANES), lambda m, t, e, lo, hi, fi, va: (t[m], 0))
    grid_spec = pltpu.PrefetchScalarGridSpec(
        num_scalar_prefetch=6,
        grid=(max_items,),
        in_specs=[
            rows_spec,
            wspec((1, d, f), 2),
            wspec((1, d, f), 2),
            wspec((1, 1, f), 2),
            wspec((1, 1, f), 2),
            wspec((1, f, d)),
            wspec((1, 1, d), 2),
        ],
        out_specs=rows_spec,
        scratch_shapes=[pltpu.VMEM((tg, d), BF16)],
    )
    return pl.pallas_call(
        functools.partial(_gmm_kernel, tg=tg, s_rows=s_rows, fk=fk),
        grid_spec=grid_spec,
        out_shape=jax.ShapeDtypeStruct(xs.shape, F32),
        compiler_params=_cparams(("arbitrary",)),
        name="moe_experts",
    )(*items, xs, w["wg"], w["wl"], w["bg"], w["bl"], w["wd"], w["bd"])


def _combine_kernel(dest_ref, dest_next_ref, ys_ref, gate_ref, x_ref, tab_ref, tabn_ref, g_ref,
                    x2_ref, h_ref, buf_ref, moe_ref, sem, *, tc, s_rows, ctx_tiles, n_steps, mode):
    step = pl.program_id(0) * pl.num_programs(1) + pl.program_id(1)
    slot = step % 2

    def gather(dref, slot_idx):
        def issue(t, carry):
            for k in range(TOP_K):
                d = dref[0, 0, t * TOP_K + k]
                src = ys_ref.at[pl.ds(pl.multiple_of(d * s_rows, s_rows), s_rows), :]
                dst = buf_ref.at[slot_idx, pl.ds(pl.multiple_of((k * tc + t) * s_rows, s_rows), s_rows), :]
                pltpu.make_async_copy(src, dst, sem.at[slot_idx]).start()
            return carry

        lax.fori_loop(0, tc, issue, 0)

    @pl.when(step == 0)
    def _():
        gather(dest_ref, 0)

    @pl.when(step + 1 < n_steps)
    def _():
        gather(dest_next_ref, 1 - slot)

    pltpu.make_async_copy(buf_ref.at[slot], buf_ref.at[slot], sem.at[slot]).wait()

    gates = gate_ref[...]
    for k in range(TOP_K):
        chunks = _from_token_rows(buf_ref[slot, pl.ds(k * tc * s_rows, tc * s_rows), :], s_rows)
        for s in range(s_rows):
            term = chunks[s] * gates[:, k:k + 1]
            if k == 0:
                moe_ref[:, s * LANES:(s + 1) * LANES] = term
            else:
                moe_ref[:, s * LANES:(s + 1) * LANES] += term

    is_ctx = pl.program_id(1) < ctx_tiles
    x2 = x_ref[0] + _tab_row(tab_ref, is_ctx, 5) * moe_ref[...]
    x2_ref[0] = x2
    if mode == "next":
        h = _norm_mod(x2, g_ref[...], _tab_row(tabn_ref, is_ctx, 0), _tab_row(tabn_ref, is_ctx, 1))
        h_ref[0] = h.astype(h_ref.dtype)
    else:
        y = x2 * lax.rsqrt(jnp.mean(x2 * x2, axis=-1, keepdims=True) + RMS_EPS)
        h_ref[0] = (y * g_ref[...]).astype(h_ref.dtype)


def _combine(ys, dest, gates, x1, tab, tab_next, g_next, n_ctx_rows, s_rows, mode):
    b, l, d = x1.shape
    tc = _pick(min(l, 128) if n_ctx_rows == 0 else n_ctx_rows, 128)
    nt = l // tc
    n_steps = b * nt
    dest3 = dest.reshape(n_steps, 1, tc * TOP_K)
    kern = functools.partial(_combine_kernel, tc=tc, s_rows=s_rows, ctx_tiles=n_ctx_rows // tc,
                             n_steps=n_steps, mode=mode)
    out_dtype = BF16 if mode == "next" else F32
    return pl.pallas_call(
        kern,
        grid=(b, nt),
        in_specs=[
            pl.BlockSpec((1, 1, tc * TOP_K), lambda i, j: (i * nt + j, 0, 0), memory_space=pltpu.SMEM),
            pl.BlockSpec((1, 1, tc * TOP_K), lambda i, j: (jnp.minimum(i * nt + j + 1, n_steps - 1), 0, 0),
                         memory_space=pltpu.SMEM),
            pl.BlockSpec(memory_space=pl.ANY),
            pl.BlockSpec((tc, LANES), lambda i, j: (i * nt + j, 0)),
            pl.BlockSpec((1, tc, d), lambda i, j: (i, j, 0)),
            pl.BlockSpec((1, TAB_ROWS, d), lambda i, j: (i, 0, 0)),
            pl.BlockSpec((1, TAB_ROWS, d), lambda i, j: (i, 0, 0)),
            pl.BlockSpec((1, d), lambda i, j: (0, 0)),
        ],
        out_specs=[
            pl.BlockSpec((1, tc, d), lambda i, j: (i, j, 0)),
            pl.BlockSpec((1, tc, d), lambda i, j: (i, j, 0)),
        ],
        out_shape=[
            jax.ShapeDtypeStruct((b, l, d), F32),
            jax.ShapeDtypeStruct((b, l, d), out_dtype),
        ],
        scratch_shapes=[
            pltpu.VMEM((2, tc * TOP_K * s_rows, LANES), F32),
            pltpu.VMEM((tc, d), F32),
            pltpu.SemaphoreType.DMA((2,)),
        ],
        compiler_params=_cparams(("arbitrary", "arbitrary")),
        name="moe_combine_" + mode,
    )(dest3, dest3, ys, gates, x1, tab, tab_next, g_next.reshape(1, d))


def _moe(hrow, logits, x1, tab, tab_next, g_next, w, layer, n_exp, n_ctx_rows, mode):
    s_rows = x1.shape[-1] // LANES
    e_pad, g_pad, r_pad, cnt = _route(logits, n_exp)
    top_e, rank = e_pad[:, :TOP_K], r_pad[:, :TOP_K]
    counts = cnt[0, :n_exp].astype(jnp.int32)
    starts = jnp.cumsum(counts) - counts
    first_slot = jnp.sum(jnp.where(top_e[..., None] == jnp.arange(n_exp, dtype=jnp.int32), starts, 0), axis=-1)
    dest = (first_slot + rank).reshape(-1)
    xs = _dispatch(hrow, dest, s_rows)
    ys = _gmm(xs, counts, w, layer, s_rows)
    return _combine(ys, dest, g_pad, x1, tab, tab_next, g_next, n_ctx_rows, s_rows, mode)


def _conv_kernel(x_ref, w_ref, b_ref, o_ref, *, n_ctx):
    l = x_ref.shape[1]
    w = w_ref[...]
    for s0, n in ((0, n_ctx), (n_ctx, l - n_ctx)):
        x = x_ref[0, s0:s0 + n, :]
        row = lax.broadcasted_iota(jnp.int32, (n, 1), 0)
        acc = x * w[CONV_LEFT:CONV_LEFT + 1]
        for j in range(w.shape[0]):
            off = j - CONV_LEFT
            if off == 0:
                continue
            shifted = pltpu.roll(x, (-off) % n, 0)
            ok = (row + off >= 0) & (row + off < n)
            acc = acc + jnp.where(ok, shifted, 0.0) * w[j:j + 1]
        o_ref[0, s0:s0 + n, :] = acc + b_ref[...]


def _conv(xb, conv_w, conv_b, n_ctx):
    b, l, d = xb.shape
    dt = _pick(d, 256, LANES)
    return pl.pallas_call(
        functools.partial(_conv_kernel, n_ctx=n_ctx),
        grid=(b, d // dt),
        in_specs=[
            pl.BlockSpec((1, l, dt), lambda i, j: (i, 0, j)),
            pl.BlockSpec((conv_w.shape[0], dt), lambda i, j: (0, j)),
            pl.BlockSpec((1, dt), lambda i, j: (0, j)),
        ],
        out_specs=pl.BlockSpec((1, l, dt), lambda i, j: (i, 0, j)),
        out_shape=jax.ShapeDtypeStruct((b, l, d), F32),
        compiler_params=_cparams(("parallel", "parallel")),
        name="rg_conv",
    )(xb, conv_w, conv_b.reshape(1, d))


SCAN_PAD = 8
SCAN_UNROLL = 8


def _scan_kernel(*refs, nb, tc, nh, reverse):
    n_in = 8 if reverse else 6
    u_ref, wa_ref, ba_ref, wi_ref, bi_ref, lam_ref = refs[:6]
    o_ref = refs[n_in]
    scratch = refs[n_in + 1:]
    a_s, x_s, h_s = scratch[0:nh], scratch[nh:2 * nh], scratch[2 * nh:3 * nh]
    carry = scratch[3 * nh]
    ts = tc + SCAN_PAD

    @pl.when(pl.program_id(1) == 0)
    def _():
        carry[...] = jnp.zeros_like(carry)

    neg = -lam_ref[0]
    softplus = jnp.maximum(neg, 0.0) + jnp.log1p(jnp.exp(-jnp.abs(neg)))
    for bi in range(nb):
        u = u_ref[bi]
        ub = u.astype(BF16)
        r = jax.nn.sigmoid(_dot(ub, wa_ref[0, 0]) + ba_ref[0, 0])
        i = jax.nn.sigmoid(_dot(ub, wi_ref[0, 0]) + bi_ref[0, 0])
        log_a = (-RG_C) * r * softplus
        a = jnp.exp(log_a)
        xin = jnp.sqrt(1.0 - a * a) * (i * u)
        for p in range(nh):
            a_s[p][pl.ds(bi * ts, tc), :] = a[:, p * LANES:(p + 1) * LANES]
            x_s[p][pl.ds(bi * ts, tc), :] = xin[:, p * LANES:(p + 1) * LANES]

    def block(j, hs):
        hs = list(hs)
        for q in range(SCAN_UNROLL):
            t = j * SCAN_UNROLL + q
            if reverse:
                t = tc - 1 - t
            for p in range(nh):
                hs[p] = a_s[p][pl.ds(t, nb, stride=ts), :] * hs[p] + x_s[p][pl.ds(t, nb, stride=ts), :]
                h_s[p][pl.ds(t, nb, stride=ts), :] = hs[p]
        return tuple(hs)

    h0 = tuple(carry[:, p * LANES:(p + 1) * LANES] for p in range(nh))
    hs = lax.fori_loop(0, tc // SCAN_UNROLL, block, h0)
    for p in range(nh):
        carry[:, p * LANES:(p + 1) * LANES] = hs[p]
    for bi in range(nb):
        for p in range(nh):
            h = h_s[p][pl.ds(bi * ts, tc), :]
            cols = slice(p * LANES, (p + 1) * LANES)
            if reverse:
                hf_ref, gy_ref = refs[6], refs[7]
                o_ref[bi, :, cols] = ((hf_ref[bi, :, cols] + h) * gy_ref[bi, :, cols]).astype(o_ref.dtype)
            else:
                o_ref[bi, :, cols] = h


def _scan(u, w_a, b_a, w_i, b_i, lam, dirn, n_ctx, h_fwd=None, gy=None):
    b, l, d = u.shape
    n_blk, w = w_a.shape[1], w_a.shape[2]
    tc = _pick(n_ctx, 128)
    nt, nc = l // tc, n_ctx // tc
    reverse = dirn == 1

    def chunk(j):
        if not reverse:
            return j
        return jnp.where(j < nc, nc - 1 - j, nt - 1 - (j - nc))

    blk = pl.BlockSpec((b, tc, w), lambda g, j: (0, chunk(j), g))
    in_specs = [
        blk,
        pl.BlockSpec((1, 1, w, w), lambda g, j: (dirn, g, 0, 0)),
        pl.BlockSpec((1, 1, 1, w), lambda g, j: (dirn, g, 0, 0)),
        pl.BlockSpec((1, 1, w, w), lambda g, j: (dirn, g, 0, 0)),
        pl.BlockSpec((1, 1, 1, w), lambda g, j: (dirn, g, 0, 0)),
        pl.BlockSpec((1, 1, w), lambda g, j: (dirn, 0, g)),
    ]
    args = [u, w_a, b_a.reshape(2, n_blk, 1, w), w_i, b_i.reshape(2, n_blk, 1, w), lam.reshape(2, 1, d)]
    if reverse:
        in_specs += [blk, blk]
        args += [h_fwd, gy]
    ts = tc + SCAN_PAD
    return pl.pallas_call(
        functools.partial(_scan_kernel, nb=b, tc=tc, nh=w // LANES, reverse=reverse),
        grid=(n_blk, nt),
        in_specs=in_specs,
        out_specs=blk,
        out_shape=jax.ShapeDtypeStruct((b, l, d), BF16 if reverse else F32),
        scratch_shapes=[pltpu.VMEM((b * ts, LANES), F32)] * (3 * (w // LANES)) + [pltpu.VMEM((b, w), F32)],
        compiler_params=_cparams(("parallel", "arbitrary")),
        name="rg_scan_" + ("rev" if reverse else "fwd"),
    )(*args)


def _mod_tables(mod_out, b, d):
    tabs = []
    for i in range(mod_out.shape[0]):
        ml = mod_out[i, :b].reshape(b, MOD_ROWS, d)
        mc = jnp.broadcast_to(mod_out[i, b].reshape(1, MOD_ROWS, d), (b, MOD_ROWS, d))
        pad = jnp.zeros((b, TAB_ROWS - 2 * MOD_ROWS, d), F32)
        tabs.append(jnp.concatenate([mc, ml, pad], axis=1))
    return tabs


def _split_gu_kernel(w_ref, perm_ref, wg_ref, wl_ref):
    for j in range(w_ref.shape[1] // (2 * LANES)):
        y = _dot(w_ref[:, 2 * LANES * j:2 * LANES * (j + 1)].astype(BF16), perm_ref[...])
        wg_ref[:, LANES * j:LANES * (j + 1)] = y[:, :LANES].astype(BF16)
        wl_ref[:, LANES * j:LANES * (j + 1)] = y[:, LANES:].astype(BF16)


def _split_gu(w_gu):
    lead, n2 = w_gu.shape[:-1], w_gu.shape[-1]
    rows = int(np.prod(lead))
    tr = _pick(rows, 512)
    src = np.concatenate([np.arange(0, 2 * LANES, 2), np.arange(1, 2 * LANES, 2)])
    perm = jnp.asarray(np.arange(2 * LANES)[:, None] == src[None, :], BF16)
    out_spec = pl.BlockSpec((tr, n2 // 2), lambda i: (i, 0))
    wg, wl = pl.pallas_call(
        _split_gu_kernel,
        grid=(rows // tr,),
        in_specs=[pl.BlockSpec((tr, n2), lambda i: (i, 0)), pl.BlockSpec((2 * LANES, 2 * LANES), lambda i: (0, 0))],
        out_specs=[out_spec, out_spec],
        out_shape=[jax.ShapeDtypeStruct((rows, n2 // 2), BF16)] * 2,
        compiler_params=_cparams(("parallel",)),
        name="split_gu_weights",
    )(w_gu.reshape(rows, n2), perm)
    return wg.reshape(*lead, n2 // 2), wl.reshape(*lead, n2 // 2)


def _moe_weights(router_w, router_b, w_gu, b_gu, w_dn, b_dn):
    depth, d, n_exp = router_w.shape
    f = w_dn.shape[2]
    wg, wl = _split_gu(w_gu)
    return {
        "rw": jnp.pad(router_w, ((0, 0), (0, 0), (0, LANES - n_exp))).astype(BF16),
        "rb": jnp.pad(router_b, ((0, 0), (0, LANES - n_exp))).reshape(depth, 1, LANES),
        "wg": wg.reshape(depth * n_exp, d, f),
        "wl": wl.reshape(depth * n_exp, d, f),
        "bg": b_gu[..., 0::2].reshape(depth * n_exp, 1, f),
        "bl": b_gu[..., 1::2].reshape(depth * n_exp, 1, f),
        "wd": w_dn.astype(BF16).reshape(depth * n_exp, f, d),
        "bd": b_dn.reshape(depth * n_exp, 1, d),
    }


def kernel(x, c, ctx, c_ctx, mod_w, mod_b, norm1_g, norm2_g, final_g, na_w_qkv, na_w_o, na_rpb, rg_w_y, rg_b_y,
           rg_w_x, rg_b_x, rg_conv_w, rg_conv_b, rg_w_a, rg_b_a, rg_w_i, rg_b_i, rg_lam, rg_w_out, rg_b_out,
           moe_router_w, moe_router_b, moe_w_gu, moe_b_gu, moe_w_dn, moe_b_dn):
    b, n_lat, d = x.shape
    n_ctx = ctx.shape[1]
    l = n_ctx + n_lat
    heads = na_rpb.shape[1]
    rows = n_lat // GRID_W
    kr = min((na_rpb.shape[2] + 1) // 2, rows)
    assert mod_w.shape[0] == 2 and n_lat % n_ctx == 0 and d % LANES == 0

    mod_rows = -(-(b + 1) // 8) * 8
    cc = jnp.concatenate([c, c_ctx[None, :], jnp.zeros((mod_rows - b - 1, d), F32)], axis=0)
    tab0, tab1 = _mod_tables(_modulation(cc, mod_w, mod_b), b, d)

    h = _prenorm(ctx, x, tab0, norm1_g[0])
    qkv = _matmul(h.reshape(b * l, d), na_w_qkv[0].astype(BF16), jnp.zeros((3 * d,), F32), BF16)
    bias = _na_bias_table(na_rpb[0], rows, kr)
    o = _attention(qkv.reshape(b, l, 3 * d), bias, n_ctx, heads)
    n_exp = moe_router_w.shape[2]
    w = _moe_weights(moe_router_w, moe_router_b, moe_w_gu, moe_b_gu, moe_w_dn, moe_b_dn)
    x1, hrow, logits = _proj(o, na_w_o[0].astype(BF16), jnp.zeros((d,), F32), (ctx, x), tab0, norm2_g[0],
                             w["rw"][0], w["rb"][0], n_ctx, latent_only=False)
    x2, h = _moe(hrow, logits, x1, tab0, tab1, norm1_g[1], w, 0, n_exp, n_ctx, "next")

    hf = h.reshape(b * l, d)
    xb = _matmul(hf, rg_w_x[0].astype(BF16), rg_b_x[0], F32).reshape(b, l, d)
    gy = _matmul(hf, rg_w_y[0].astype(BF16), rg_b_y[0], F32, act="gelu").reshape(b, l, d)
    u = _conv(xb, rg_conv_w[0], rg_conv_b[0], n_ctx)
    w_a, w_i = rg_w_a[0].astype(BF16), rg_w_i[0].astype(BF16)
    h_fwd = _scan(u, w_a, rg_b_a[0], w_i, rg_b_i[0], rg_lam[0], 0, n_ctx)
    hg = _scan(u, w_a, rg_b_a[0], w_i, rg_b_i[0], rg_lam[0], 1, n_ctx, h_fwd, gy)
    x1, hrow, logits = _proj(hg, rg_w_out[0].astype(BF16), rg_b_out[0], x2, tab1, norm2_g[1],
                             w["rw"][1], w["rb"][1], n_ctx, latent_only=True)
    _, out = _moe(hrow, logits, x1, tab1, tab1, final_g, w, 1, n_exp, 0, "final")
    return out
```

```python
import functools

import jax
import jax.numpy as jnp
import numpy as np
from jax import lax
from jax.experimental import pallas as pl
from jax.experimental.pallas import tpu as pltpu

F32 = jnp.float32
BF16 = jnp.bfloat16

LANES = 128
GRID_W = 64
TOP_K = 4
RG_C = 8.0
CONV_LEFT = 2
SWIGLU_ALPHA = 1.702
SWIGLU_LIMIT = 7.0
RMS_EPS = 1e-6
MOD_ROWS = 6
TAB_ROWS = 16
VMEM_LIMIT = 56 * 1024 * 1024


def _cparams(sem):
    return pltpu.CompilerParams(dimension_semantics=sem, vmem_limit_bytes=VMEM_LIMIT)


def _pick(n, pref, mult=8):
    for t in range(min(pref, n), 0, -1):
        if n % t == 0 and t % mult == 0:
            return t
    return n


def _dot(a, b):
    return jnp.dot(a, b, preferred_element_type=F32)


def _dot_nt(a, b):
    return lax.dot_general(a, b, (((1,), (1,)), ((), ())), preferred_element_type=F32)


def _split_bf16(x):
    hi = x.astype(BF16)
    lo = (x - hi.astype(F32)).astype(BF16)
    return hi, lo


def _dot3(a, w):
    a_hi, a_lo = _split_bf16(a)
    w_hi, w_lo = _split_bf16(w)
    return _dot(a_hi, w_hi) + _dot(a_lo, w_hi) + _dot(a_hi, w_lo)


def _norm_mod(x, g, shift, scale):
    y = x * lax.rsqrt(jnp.mean(x * x, axis=-1, keepdims=True) + RMS_EPS)
    return (y * g) * (1.0 + scale) + shift


def _to_token_rows(v):
    t, d = v.shape
    s_rows = d // LANES
    chunks = jnp.stack([v[:, s * LANES:(s + 1) * LANES] for s in range(s_rows)], axis=0)
    return pltpu.einshape("stl->tsl", chunks).reshape(t * s_rows, LANES)


def _from_token_rows(r, s_rows):
    return pltpu.einshape("tsl->stl", r.reshape(r.shape[0] // s_rows, s_rows, LANES))


def _tab_row(tab_ref, is_ctx, k):
    base = jnp.where(is_ctx, 0, MOD_ROWS)
    return tab_ref[0, pl.ds(base + k, 1), :]


def _mod_kernel(a_ref, w_ref, b_ref, o_ref):
    a = a_ref[...]
    a = a * jax.nn.sigmoid(a)
    o_ref[0] = _dot3(a, w_ref[0]) + b_ref[0]


def _modulation(cc, mod_w, mod_b):
    depth, d, n = mod_w.shape
    r = cc.shape[0]
    tn = _pick(n, 1024, LANES)
    return pl.pallas_call(
        _mod_kernel,
        grid=(depth, n // tn),
        in_specs=[
            pl.BlockSpec((r, d), lambda i, j: (0, 0)),
            pl.BlockSpec((1, d, tn), lambda i, j: (i, 0, j)),
            pl.BlockSpec((1, 1, tn), lambda i, j: (i, 0, j)),
        ],
        out_specs=pl.BlockSpec((1, r, tn), lambda i, j: (i, 0, j)),
        out_shape=jax.ShapeDtypeStruct((depth, r, n), F32),
        compiler_params=_cparams(("parallel", "parallel")),
        name="modulation",
    )(cc, mod_w, mod_b.reshape(depth, 1, n))


def _prenorm_kernel(ctx_ref, x_ref, tab_ref, g_ref, o_ref):
    is_ctx = pl.program_id(1) == 0
    xin = jnp.where(is_ctx, ctx_ref[0], x_ref[0])
    h = _norm_mod(xin, g_ref[...], _tab_row(tab_ref, is_ctx, 0), _tab_row(tab_ref, is_ctx, 1))
    o_ref[0] = h.astype(BF16)


def _prenorm(ctx, x, tab, g):
    b, n_ctx, d = ctx.shape
    tm = n_ctx
    nt = 1 + x.shape[1] // tm
    return pl.pallas_call(
        _prenorm_kernel,
        grid=(b, nt),
        in_specs=[
            pl.BlockSpec((1, tm, d), lambda i, j: (i, 0, 0)),
            pl.BlockSpec((1, tm, d), lambda i, j: (i, jnp.maximum(j - 1, 0), 0)),
            pl.BlockSpec((1, TAB_ROWS, d), lambda i, j: (i, 0, 0)),
            pl.BlockSpec((1, d), lambda i, j: (0, 0)),
        ],
        out_specs=pl.BlockSpec((1, tm, d), lambda i, j: (i, j, 0)),
        out_shape=jax.ShapeDtypeStruct((b, nt * tm, d), BF16),
        compiler_params=_cparams(("parallel", "parallel")),
        name="prenorm",
    )(ctx, x, tab, g.reshape(1, d))


def _gelu_tanh(x):
    return 0.5 * x * (1.0 + jnp.tanh(np.sqrt(2.0 / np.pi) * (x + 0.044715 * (x * x * x))))


def _matmul_kernel(a_ref, w_ref, b_ref, o_ref, *, act):
    y = _dot(a_ref[...], w_ref[...]) + b_ref[...]
    if act == "gelu":
        y = _gelu_tanh(y)
    o_ref[...] = y.astype(o_ref.dtype)


def _matmul(a, w, bias, out_dtype, act=None):
    m, k = a.shape
    n = w.shape[1]
    tm = _pick(m, 1024)
    tn = _pick(n, 512, LANES)
    return pl.pallas_call(
        functools.partial(_matmul_kernel, act=act),
        grid=(m // tm, n // tn),
        in_specs=[
            pl.BlockSpec((tm, k), lambda i, j: (i, 0)),
            pl.BlockSpec((k, tn), lambda i, j: (0, j)),
            pl.BlockSpec((1, tn), lambda i, j: (0, j)),
        ],
        out_specs=pl.BlockSpec((tm, tn), lambda i, j: (i, j)),
        out_shape=jax.ShapeDtypeStruct((m, n), out_dtype),
        compiler_params=_cparams(("parallel", "parallel")),
        name="matmul_" + (act or "linear"),
    )(a, w, bias.reshape(1, n))


def _softmax_parts(parts):
    m = parts[0].max(axis=-1, keepdims=True)
    for s in parts[1:]:
        m = jnp.maximum(m, s.max(axis=-1, keepdims=True))
    ps = [jnp.exp(s - m) for s in parts]
    den = ps[0].sum(axis=-1, keepdims=True)
    for p in ps[1:]:
        den = den + p.sum(axis=-1, keepdims=True)
    return ps, den


ATTN_ROWS_PER_ITER = 16


def _attn_kernel(q_ref, k_ref, v_ref, bias_ref, o_ref, *, n_ctx, rows, kr, scale):
    kc = k_ref[0, 0:n_ctx, :]
    vc = v_ref[0, 0:n_ctx, :]
    (p,), den = _softmax_parts([_dot_nt(q_ref[0, 0:n_ctx, :], kc) * scale])
    o_ref[0, 0:n_ctx, :] = (_dot(p.astype(BF16), vc) / den).astype(BF16)

    group = next(g for g in (ATTN_ROWS_PER_ITER, 2, 1) if rows % g == 0)

    def row_group(i, carry):
        rr = [i * group + j for j in range(group)]
        rs = [jnp.clip(r - kr // 2, 0, rows - kr) for r in rr]
        q0 = [pl.multiple_of(n_ctx + r * GRID_W, GRID_W) for r in rr]
        k0 = [pl.multiple_of(n_ctx + s * GRID_W, GRID_W) for s in rs]
        scores = []
        for j in range(group):
            q = q_ref[0, pl.ds(q0[j], GRID_W), :]
            s_lat = _dot_nt(q, k_ref[0, pl.ds(k0[j], kr * GRID_W), :]) * scale + bias_ref[0, rr[j] - rs[j]]
            scores.append([s_lat, _dot_nt(q, kc) * scale])
        probs = [_softmax_parts(s) for s in scores]
        for j in range(group):
            (p_lat, p_ctx), den = probs[j]
            o = _dot(p_lat.astype(BF16), v_ref[0, pl.ds(k0[j], kr * GRID_W), :]) + _dot(p_ctx.astype(BF16), vc)
            o_ref[0, pl.ds(q0[j], GRID_W), :] = (o / den).astype(BF16)
        return carry

    lax.fori_loop(0, rows // group, row_group, 0)


def _na_bias_table(rpb, rows, kr):
    h, n_dr, n_dc = rpb.shape
    win_rows, win_cols = (n_dr + 1) // 2, (n_dc + 1) // 2
    col = np.arange(GRID_W)
    col_start = np.clip(col - win_cols // 2, 0, GRID_W - win_cols)
    col_mask = (col[None, :] >= col_start[:, None]) & (col[None, :] < col_start[:, None] + win_cols)
    dc_idx = np.clip(col[None, :] - col[:, None], 1 - win_cols, win_cols - 1) + win_cols - 1
    dr_idx = np.arange(kr)[None, :] - np.arange(kr)[:, None] + win_rows - 1
    t = rpb[:, dr_idx][:, :, :, dc_idx]
    t = jnp.where(col_mask[None, None, None], t.astype(F32), -jnp.inf)
    return t.transpose(0, 1, 3, 2, 4).reshape(h, kr, GRID_W, kr * GRID_W)


def _attention(qkv, bias, n_ctx, heads):
    b, l, d3 = qkv.shape
    d = d3 // 3
    dh = d // heads
    rows = (l - n_ctx) // GRID_W
    kr = bias.shape[1]
    kern = functools.partial(_attn_kernel, n_ctx=n_ctx, rows=rows, kr=kr, scale=dh ** -0.5)
    return pl.pallas_call(
        kern,
        grid=(heads, b),
        in_specs=[
            pl.BlockSpec((1, l, dh), lambda h, i: (i, 0, h)),
            pl.BlockSpec((1, l, dh), lambda h, i: (i, 0, heads + h)),
            pl.BlockSpec((1, l, dh), lambda h, i: (i, 0, 2 * heads + h)),
            pl.BlockSpec((1, kr, GRID_W, kr * GRID_W), lambda h, i: (h, 0, 0, 0)),
        ],
        out_specs=pl.BlockSpec((1, l, dh), lambda h, i: (i, 0, h)),
        out_shape=jax.ShapeDtypeStruct((b, l, d), BF16),
        compiler_params=_cparams(("parallel", "parallel")),
        name="na_attention",
    )(qkv, qkv, qkv, bias)


def _proj_kernel(*refs, ctx_tiles, tile_off, split_residual):
    if split_residual:
        a_ref, w_ref, b_ref, ctx_ref, x_ref, tab_ref, g_ref, rw_ref, rb_ref, x1_ref, hrow_ref, lg_ref = refs
    else:
        a_ref, w_ref, b_ref, x_ref, tab_ref, g_ref, rw_ref, rb_ref, x1_ref, hrow_ref, lg_ref = refs
    is_ctx = pl.program_id(1) + tile_off < ctx_tiles
    resid = jnp.where(is_ctx, ctx_ref[0], x_ref[0]) if split_residual else x_ref[0]
    y = _dot(a_ref[0], w_ref[...]) + b_ref[...]
    x1 = resid + _tab_row(tab_ref, is_ctx, 2) * y
    x1_ref[0] = x1
    h2 = _norm_mod(x1, g_ref[...], _tab_row(tab_ref, is_ctx, 3), _tab_row(tab_ref, is_ctx, 4))
    lg_ref[...] = _dot(h2.astype(BF16), rw_ref[...]) + rb_ref[...]
    hrow_ref[...] = _to_token_rows(h2)


def _proj(a, w, bias, resid, tab, g2, rw, rb, n_ctx, latent_only):
    split = isinstance(resid, tuple)
    b, l, d = a.shape
    tm = n_ctx
    off = n_ctx // tm if latent_only else 0
    nt = l // tm - off
    s_rows = d // LANES
    t = b * nt * tm
    kern = functools.partial(_proj_kernel, ctx_tiles=n_ctx // tm, tile_off=off, split_residual=split)
    if split:
        assert not latent_only and n_ctx == tm
        resid_specs = [pl.BlockSpec((1, tm, d), lambda i, j: (i, 0, 0)),
                       pl.BlockSpec((1, tm, d), lambda i, j: (i, jnp.maximum(j - 1, 0), 0))]
        resid_args = list(resid)
    else:
        resid_specs = [pl.BlockSpec((1, tm, d), lambda i, j: (i, j + off, 0))]
        resid_args = [resid]
    return pl.pallas_call(
        kern,
        grid=(b, nt),
        in_specs=[
            pl.BlockSpec((1, tm, d), lambda i, j: (i, j + off, 0)),
            pl.BlockSpec((d, d), lambda i, j: (0, 0)),
            pl.BlockSpec((1, d), lambda i, j: (0, 0)),
        ] + resid_specs + [
            pl.BlockSpec((1, TAB_ROWS, d), lambda i, j: (i, 0, 0)),
            pl.BlockSpec((1, d), lambda i, j: (0, 0)),
            pl.BlockSpec((d, LANES), lambda i, j: (0, 0)),
            pl.BlockSpec((1, LANES), lambda i, j: (0, 0)),
        ],
        out_specs=[
            pl.BlockSpec((1, tm, d), lambda i, j: (i, j, 0)),
            pl.BlockSpec((tm * s_rows, LANES), lambda i, j: (i * nt + j, 0)),
            pl.BlockSpec((tm, LANES), lambda i, j: (i * nt + j, 0)),
        ],
        out_shape=[
            jax.ShapeDtypeStruct((b, nt * tm, d), F32),
            jax.ShapeDtypeStruct((t * s_rows, LANES), F32),
            jax.ShapeDtypeStruct((t, LANES), F32),
        ],
        compiler_params=_cparams(("parallel", "parallel")),
        name="mixer_proj",
    )(a, w, bias.reshape(1, d), *resid_args, tab, g2.reshape(1, d), rw, rb)


def _route_kernel(lg_ref, e_ref, g_ref, r_ref, cnt_ref, carry_ref, *, n_exp):
    @pl.when(pl.program_id(0) == 0)
    def _():
        carry_ref[...] = jnp.zeros_like(carry_ref)

    lg = lg_ref[...]
    tm = lg.shape[0]
    lane = lax.broadcasted_iota(jnp.int32, lg.shape, 1).astype(F32)
    cur = jnp.where(lane < n_exp, lg, -jnp.inf)
    multi = jnp.zeros(lg.shape, F32)
    vals, idxs = [], []
    for _ in range(TOP_K):
        m = cur.max(axis=-1, keepdims=True)
        idx = jnp.where(cur == m, lane, float(LANES)).min(axis=-1, keepdims=True)
        sel = lane == idx
        multi = jnp.where(sel, 1.0, multi)
        cur = jnp.where(sel, -jnp.inf, cur)
        vals.append(m)
        idxs.append(idx)
    exps = [jnp.exp(v - vals[0]) for v in vals]
    den = exps[0]
    for e in exps[1:]:
        den = den + e
    tri = (lax.broadcasted_iota(jnp.int32, (tm, tm), 0) > lax.broadcasted_iota(jnp.int32, (tm, tm), 1))
    pref = _dot(jnp.where(tri, 1.0, 0.0).astype(BF16), multi.astype(BF16))
    tot = carry_ref[...] + pref
    e_out = jnp.zeros(lg.shape, F32)
    g_out = jnp.zeros(lg.shape, F32)
    r_out = jnp.zeros(lg.shape, F32)
    for k in range(TOP_K):
        rank_k = jnp.where(lane == idxs[k], tot, 0.0).sum(axis=-1, keepdims=True)
        e_out = jnp.where(lane == k, idxs[k], e_out)
        g_out = jnp.where(lane == k, exps[k] / den, g_out)
        r_out = jnp.where(lane == k, rank_k, r_out)
    e_ref[...] = e_out.astype(jnp.int32)
    g_ref[...] = g_out
    r_ref[...] = r_out.astype(jnp.int32)
    carry_ref[...] = carry_ref[...] + multi.sum(axis=0, keepdims=True)
    cnt_ref[...] = carry_ref[...]


def _route(logits, n_exp):
    t = logits.shape[0]
    tm = _pick(t, 256)
    spec = pl.BlockSpec((tm, LANES), lambda i: (i, 0))
    return pl.pallas_call(
        functools.partial(_route_kernel, n_exp=n_exp),
        grid=(t // tm,),
        in_specs=[spec],
        out_specs=[spec, spec, spec, pl.BlockSpec((1, LANES), lambda i: (0, 0))],
        out_shape=[
            jax.ShapeDtypeStruct((t, LANES), jnp.int32),
            jax.ShapeDtypeStruct((t, LANES), F32),
            jax.ShapeDtypeStruct((t, LANES), jnp.int32),
            jax.ShapeDtypeStruct((1, LANES), F32),
        ],
        scratch_shapes=[pltpu.VMEM((1, LANES), F32)],
        compiler_params=_cparams(("arbitrary",)),
        name="router",
    )(logits)


def _dispatch_kernel(dest_ref, h_ref, xs_ref, sem, *, td, s_rows):
    def issue(t, carry):
        src = h_ref.at[pl.ds(pl.multiple_of(t * s_rows, s_rows), s_rows), :]
        for k in range(TOP_K):
            d = dest_ref[0, 0, t * TOP_K + k]
            dst = xs_ref.at[pl.ds(pl.multiple_of(d * s_rows, s_rows), s_rows), :]
            pltpu.make_async_copy(src, dst, sem).start()
        return carry

    lax.fori_loop(0, td, issue, 0)
    for _ in range(TOP_K):
        pltpu.make_async_copy(h_ref, xs_ref.at[pl.ds(0, td * s_rows), :], sem).wait()


DISPATCH_TOKENS = 1024


def _dispatch(hrow, dest, s_rows):
    t = dest.shape[0] // TOP_K
    td = _pick(t, DISPATCH_TOKENS)
    nt = t // td
    return pl.pallas_call(
        functools.partial(_dispatch_kernel, td=td, s_rows=s_rows),
        grid=(nt,),
        in_specs=[
            pl.BlockSpec((1, 1, td * TOP_K), lambda i: (i, 0, 0), memory_space=pltpu.SMEM),
            pl.BlockSpec((td * s_rows, LANES), lambda i: (i, 0)),
        ],
        out_specs=pl.BlockSpec(memory_space=pl.ANY),
        out_shape=jax.ShapeDtypeStruct((t * TOP_K * s_rows, LANES), F32),
        scratch_shapes=[pltpu.SemaphoreType.DMA(())],
        compiler_params=pltpu.CompilerParams(dimension_semantics=("arbitrary",), vmem_limit_bytes=VMEM_LIMIT,
                                             has_side_effects=True),
        name="moe_dispatch",
    )(dest.reshape(nt, 1, td * TOP_K), hrow)


def _gmm_kernel(it_tile, it_e, it_lo, it_hi, it_first, it_valid,
                xs_ref, wg_ref, wl_ref, bg_ref, bl_ref, wd_ref, bd_ref, ys_ref, xb_ref, *, tg, s_rows, fk):
    m = pl.program_id(0)
    tsub = _pick(tg, 128)
    f = wd_ref.shape[1]

    @pl.when(it_valid[m] == 1)
    def _():
        for t0 in range(0, tg, tsub):
            xt = _from_token_rows(xs_ref[pl.ds(t0 * s_rows, tsub * s_rows), :], s_rows)
            for s in range(s_rows):
                xb_ref[pl.ds(t0, tsub), s * LANES:(s + 1) * LANES] = xt[s].astype(BF16)

        x = xb_ref[...]
        y = None
        for f0 in range(0, f, fk):
            cols = slice(f0, f0 + fk)
            glu = jnp.minimum(_dot(x, wg_ref[0, :, cols]) + bg_ref[0, :, cols], SWIGLU_LIMIT)
            lin = jnp.clip(_dot(x, wl_ref[0, :, cols]) + bl_ref[0, :, cols], -SWIGLU_LIMIT, SWIGLU_LIMIT)
            act = glu * jax.nn.sigmoid(SWIGLU_ALPHA * glu) * (lin + 1.0)
            part = _dot(act.astype(BF16), wd_ref[0, cols, :])
            y = part if y is None else y + part
        y = y + bd_ref[0]

        for t0 in range(0, tg, tsub):
            rows = pl.ds(t0 * s_rows, tsub * s_rows)
            yr = _to_token_rows(y[t0:t0 + tsub])
            row = t0 * s_rows + lax.broadcasted_iota(jnp.int32, (tsub * s_rows, 1), 0)
            mine = (row >= it_lo[m] * s_rows) & (row < it_hi[m] * s_rows)

            @pl.when(it_first[m] == 1)
            def _():
                ys_ref[rows, :] = jnp.where(mine, yr, 0.0)

            @pl.when(it_first[m] == 0)
            def _():
                ys_ref[rows, :] = jnp.where(mine, yr, ys_ref[rows, :])


def _gmm_items(counts, n_tiles, tg, max_items):
    n_exp = counts.shape[0]
    ends = jnp.cumsum(counts)
    starts = ends - counts
    def count_le(sorted_vals, q):
        return jnp.sum(sorted_vals[None, :] <= q[:, None], axis=1, dtype=jnp.int32)

    tile0 = jnp.arange(n_tiles, dtype=jnp.int32) * tg
    e_lo = jnp.minimum(count_le(ends, tile0), n_exp - 1)
    e_hi = jnp.minimum(count_le(ends, tile0 + tg - 1), n_exp - 1)
    n_items = e_hi - e_lo + 1
    item_end = jnp.cumsum(n_items)
    item_start = item_end - n_items
    total = item_end[-1]
    m = jnp.arange(max_items, dtype=jnp.int32)
    valid = m < total
    tile = jnp.minimum(count_le(item_end, m), n_tiles - 1)
    e = jnp.where(valid, e_lo[tile] + (m - item_start[tile]), e_hi[n_tiles - 1]).astype(jnp.int32)
    lo = jnp.clip(starts[e] - tile * tg, 0, tg).astype(jnp.int32)
    hi = jnp.clip(ends[e] - tile * tg, 0, tg).astype(jnp.int32)
    first = (m == item_start[tile]).astype(jnp.int32)
    return tile, e, lo, hi, first, valid.astype(jnp.int32)


GMM_ROW_TILE = 256
GMM_F_CHUNK = 1024


def _gmm(xs, counts, w, layer, s_rows):
    n_exp = counts.shape[0]
    d, f = w["wg"].shape[1:]
    p = xs.shape[0] // s_rows
    tg = _pick(p, GMM_ROW_TILE)
    fk = _pick(f, GMM_F_CHUNK, LANES)
    n_tiles = p // tg
    max_items = n_tiles + n_exp - 1
    tile, e, lo, hi, first, valid = _gmm_items(counts, n_tiles, tg, max_items)
    items = (tile, e + layer * n_exp, lo, hi, first, valid)

    def wspec(block, buffers=1):
        return pl.BlockSpec(block, lambda m, t, e, lo, hi, fi, va: (e[m], 0, 0), pipeline_mode=pl.Buffered(buffers))

    rows_spec = pl.BlockSpec((tg * s_rows, LANES), lambda m, t, e, lo, hi, fi, va: (t[m], 0))
    grid_spec = pltpu.PrefetchScalarGridSpec(
        num_scalar_prefetch=6,
        grid=(max_items,),
        in_specs=[
            rows_spec,
            wspec((1, d, f), 2),
            wspec((1, d, f), 2),
            wspec((1, 1, f), 2),
            wspec((1, 1, f), 2),
            wspec((1, f, d)),
            wspec((1, 1, d), 2),
        ],
        out_specs=rows_spec,
        scratch_shapes=[pltpu.VMEM((tg, d), BF16)],
    )
    return pl.pallas_call(
        functools.partial(_gmm_kernel, tg=tg, s_rows=s_rows, fk=fk),
        grid_spec=grid_spec,
        out_shape=jax.ShapeDtypeStruct(xs.shape, F32),
        compiler_params=_cparams(("arbitrary",)),
        name="moe_experts",
    )(*items, xs, w["wg"], w["wl"], w["bg"], w["bl"], w["wd"], w["bd"])


def _combine_kernel(dest_ref, dest_next_ref, ys_ref, gate_ref, x_ref, tab_ref, tabn_ref, g_ref,
                    x2_ref, h_ref, buf0_ref, buf1_ref, moe_ref, sem, *, tc, s_rows, ctx_tiles, n_steps, mode):
    step = pl.program_id(0) * pl.num_programs(1) + pl.program_id(1)
    bufs = (buf0_ref, buf1_ref)

    def gather(dref, slot):
        for t in range(tc):
            for k in range(TOP_K):
                d = dref[0, 0, t * TOP_K + k]
                src = ys_ref.at[pl.ds(pl.multiple_of(d * s_rows, s_rows), s_rows), :]
                dst = bufs[slot].at[pl.ds((k * tc + t) * s_rows, s_rows), :]
                pltpu.make_async_copy(src, dst, sem.at[slot]).start()

    def wait(slot):
        pltpu.make_async_copy(bufs[slot], bufs[slot], sem.at[slot]).wait()

    @pl.when(step == 0)
    def _():
        gather(dest_ref, 0)

    def body(slot):
        wait(slot)
        gather(dest_next_ref, 1 - slot)
        gates = gate_ref[...]
        for k in range(TOP_K):
            chunks = _from_token_rows(bufs[slot][pl.ds(k * tc * s_rows, tc * s_rows), :], s_rows)
            for s in range(s_rows):
                term = chunks[s] * gates[:, k:k + 1]
                if k == 0:
                    moe_ref[:, s * LANES:(s + 1) * LANES] = term
                else:
                    moe_ref[:, s * LANES:(s + 1) * LANES] += term

        is_ctx = pl.program_id(1) < ctx_tiles
        x2 = x_ref[0] + _tab_row(tab_ref, is_ctx, 5) * moe_ref[...]
        x2_ref[0] = x2
        if mode == "next":
            h = _norm_mod(x2, g_ref[...], _tab_row(tabn_ref, is_ctx, 0), _tab_row(tabn_ref, is_ctx, 1))
            h_ref[0] = h.astype(h_ref.dtype)
        else:
            y = x2 * lax.rsqrt(jnp.mean(x2 * x2, axis=-1, keepdims=True) + RMS_EPS)
            h_ref[0] = (y * g_ref[...]).astype(h_ref.dtype)

        @pl.when(step == n_steps - 1)
        def _():
            wait(1 - slot)

    for slot in range(2):
        pl.when(step % 2 == slot)(functools.partial(body, slot))


def _combine(ys, dest, gates, x1, tab, tab_next, g_next, n_ctx_rows, s_rows, mode):
    b, l, d = x1.shape
    tc = _pick(min(l, 128) if n_ctx_rows == 0 else n_ctx_rows, 128)
    nt = l // tc
    n_steps = b * nt
    dest3 = dest.reshape(n_steps, 1, tc * TOP_K)
    kern = functools.partial(_combine_kernel, tc=tc, s_rows=s_rows, ctx_tiles=n_ctx_rows // tc,
                             n_steps=n_steps, mode=mode)
    out_dtype = BF16 if mode == "next" else F32
    return pl.pallas_call(
        kern,
        grid=(b, nt),
        in_specs=[
            pl.BlockSpec((1, 1, tc * TOP_K), lambda i, j: (i * nt + j, 0, 0), memory_space=pltpu.SMEM),
            pl.BlockSpec((1, 1, tc * TOP_K), lambda i, j: (jnp.minimum(i * nt + j + 1, n_steps - 1), 0, 0),
                         memory_space=pltpu.SMEM),
            pl.BlockSpec(memory_space=pl.ANY),
            pl.BlockSpec((tc, LANES), lambda i, j: (i * nt + j, 0)),
            pl.BlockSpec((1, tc, d), lambda i, j: (i, j, 0)),
            pl.BlockSpec((1, TAB_ROWS, d), lambda i, j: (i, 0, 0)),
            pl.BlockSpec((1, TAB_ROWS, d), lambda i, j: (i, 0, 0)),
            pl.BlockSpec((1, d), lambda i, j: (0, 0)),
        ],
        out_specs=[
            pl.BlockSpec((1, tc, d), lambda i, j: (i, j, 0)),
            pl.BlockSpec((1, tc, d), lambda i, j: (i, j, 0)),
        ],
        out_shape=[
            jax.ShapeDtypeStruct((b, l, d), F32),
            jax.ShapeDtypeStruct((b, l, d), out_dtype),
        ],
        scratch_shapes=[
            pltpu.VMEM((tc * TOP_K * s_rows, LANES), F32),
            pltpu.VMEM((tc * TOP_K * s_rows, LANES), F32),
            pltpu.VMEM((tc, d), F32),
            pltpu.SemaphoreType.DMA((2,)),
        ],
        compiler_params=_cparams(("arbitrary", "arbitrary")),
        name="moe_combine_" + mode,
    )(dest3, dest3, ys, gates, x1, tab, tab_next, g_next.reshape(1, d))


def _moe(hrow, logits, x1, tab, tab_next, g_next, w, layer, n_exp, n_ctx_rows, mode):
    s_rows = x1.shape[-1] // LANES
    e_pad, g_pad, r_pad, cnt = _route(logits, n_exp)
    top_e, rank = e_pad[:, :TOP_K], r_pad[:, :TOP_K]
    counts = cnt[0, :n_exp].astype(jnp.int32)
    starts = jnp.cumsum(counts) - counts
    first_slot = jnp.sum(jnp.where(top_e[..., None] == jnp.arange(n_exp, dtype=jnp.int32), starts, 0), axis=-1)
    dest = (first_slot + rank).reshape(-1)
    xs = _dispatch(hrow, dest, s_rows)
    ys = _gmm(xs, counts, w, layer, s_rows)
    return _combine(ys, dest, g_pad, x1, tab, tab_next, g_next, n_ctx_rows, s_rows, mode)


def _conv_kernel(x_ref, w_ref, b_ref, o_ref, *, n_ctx):
    l = x_ref.shape[1]
    w = w_ref[...]
    for s0, n in ((0, n_ctx), (n_ctx, l - n_ctx)):
        x = x_ref[0, s0:s0 + n, :]
        row = lax.broadcasted_iota(jnp.int32, (n, 1), 0)
        acc = x * w[CONV_LEFT:CONV_LEFT + 1]
        for j in range(w.shape[0]):
            off = j - CONV_LEFT
            if off == 0:
                continue
            shifted = pltpu.roll(x, (-off) % n, 0)
            ok = (row + off >= 0) & (row + off < n)
            acc = acc + jnp.where(ok, shifted, 0.0) * w[j:j + 1]
        o_ref[0, s0:s0 + n, :] = acc + b_ref[...]


def _conv(xb, conv_w, conv_b, n_ctx):
    b, l, d = xb.shape
    dt = _pick(d, 256, LANES)
    return pl.pallas_call(
        functools.partial(_conv_kernel, n_ctx=n_ctx),
        grid=(b, d // dt),
        in_specs=[
            pl.BlockSpec((1, l, dt), lambda i, j: (i, 0, j)),
            pl.BlockSpec((conv_w.shape[0], dt), lambda i, j: (0, j)),
            pl.BlockSpec((1, dt), lambda i, j: (0, j)),
        ],
        out_specs=pl.BlockSpec((1, l, dt), lambda i, j: (i, 0, j)),
        out_shape=jax.ShapeDtypeStruct((b, l, d), F32),
        compiler_params=_cparams(("parallel", "parallel")),
        name="rg_conv",
    )(xb, conv_w, conv_b.reshape(1, d))


SCAN_PAD = 8
SCAN_UNROLL = 8


def _scan_kernel(*refs, nb, tc, nh, reverse):
    n_in = 8 if reverse else 6
    u_ref, wa_ref, ba_ref, wi_ref, bi_ref, lam_ref = refs[:6]
    o_ref = refs[n_in]
    scratch = refs[n_in + 1:]
    a_s, x_s, h_s = scratch[0:nh], scratch[nh:2 * nh], scratch[2 * nh:3 * nh]
    carry = scratch[3 * nh]
    ts = tc + SCAN_PAD

    @pl.when(pl.program_id(1) == 0)
    def _():
        carry[...] = jnp.zeros_like(carry)

    neg = -lam_ref[0]
    softplus = jnp.maximum(neg, 0.0) + jnp.log1p(jnp.exp(-jnp.abs(neg)))
    for bi in range(nb):
        u = u_ref[bi]
        ub = u.astype(BF16)
        r = jax.nn.sigmoid(_dot(ub, wa_ref[0, 0]) + ba_ref[0, 0])
        i = jax.nn.sigmoid(_dot(ub, wi_ref[0, 0]) + bi_ref[0, 0])
        log_a = (-RG_C) * r * softplus
        a = jnp.exp(log_a)
        xin = jnp.sqrt(1.0 - a * a) * (i * u)
        for p in range(nh):
            a_s[p][pl.ds(bi * ts, tc), :] = a[:, p * LANES:(p + 1) * LANES]
            x_s[p][pl.ds(bi * ts, tc), :] = xin[:, p * LANES:(p + 1) * LANES]

    def block(j, hs):
        hs = list(hs)
        for q in range(SCAN_UNROLL):
            t = j * SCAN_UNROLL + q
            if reverse:
                t = tc - 1 - t
            for p in range(nh):
                hs[p] = a_s[p][pl.ds(t, nb, stride=ts), :] * hs[p] + x_s[p][pl.ds(t, nb, stride=ts), :]
                h_s[p][pl.ds(t, nb, stride=ts), :] = hs[p]
        return tuple(hs)

    h0 = tuple(carry[:, p * LANES:(p + 1) * LANES] for p in range(nh))
    hs = lax.fori_loop(0, tc // SCAN_UNROLL, block, h0)
    for p in range(nh):
        carry[:, p * LANES:(p + 1) * LANES] = hs[p]
    for bi in range(nb):
        for p in range(nh):
            h = h_s[p][pl.ds(bi * ts, tc), :]
            cols = slice(p * LANES, (p + 1) * LANES)
            if reverse:
                hf_ref, gy_ref = refs[6], refs[7]
                o_ref[bi, :, cols] = ((hf_ref[bi, :, cols] + h) * gy_ref[bi, :, cols]).astype(o_ref.dtype)
            else:
                o_ref[bi, :, cols] = h


def _scan(u, w_a, b_a, w_i, b_i, lam, dirn, n_ctx, h_fwd=None, gy=None):
    b, l, d = u.shape
    n_blk, w = w_a.shape[1], w_a.shape[2]
    tc = _pick(n_ctx, 128)
    nt, nc = l // tc, n_ctx // tc
    reverse = dirn == 1

    def chunk(j):
        if not reverse:
            return j
        return jnp.where(j < nc, nc - 1 - j, nt - 1 - (j - nc))

    blk = pl.BlockSpec((b, tc, w), lambda g, j: (0, chunk(j), g))
    in_specs = [
        blk,
        pl.BlockSpec((1, 1, w, w), lambda g, j: (dirn, g, 0, 0)),
        pl.BlockSpec((1, 1, 1, w), lambda g, j: (dirn, g, 0, 0)),
        pl.BlockSpec((1, 1, w, w), lambda g, j: (dirn, g, 0, 0)),
        pl.BlockSpec((1, 1, 1, w), lambda g, j: (dirn, g, 0, 0)),
        pl.BlockSpec((1, 1, w), lambda g, j: (dirn, 0, g)),
    ]
    args = [u, w_a, b_a.reshape(2, n_blk, 1, w), w_i, b_i.reshape(2, n_blk, 1, w), lam.reshape(2, 1, d)]
    if reverse:
        in_specs += [blk, blk]
        args += [h_fwd, gy]
    ts = tc + SCAN_PAD
    return pl.pallas_call(
        functools.partial(_scan_kernel, nb=b, tc=tc, nh=w // LANES, reverse=reverse),
        grid=(n_blk, nt),
        in_specs=in_specs,
        out_specs=blk,
        out_shape=jax.ShapeDtypeStruct((b, l, d), BF16 if reverse else F32),
        scratch_shapes=[pltpu.VMEM((b * ts, LANES), F32)] * (3 * (w // LANES)) + [pltpu.VMEM((b, w), F32)],
        compiler_params=_cparams(("parallel", "arbitrary")),
        name="rg_scan_" + ("rev" if reverse else "fwd"),
    )(*args)


def _mod_tables(mod_out, b, d):
    tabs = []
    for i in range(mod_out.shape[0]):
        ml = mod_out[i, :b].reshape(b, MOD_ROWS, d)
        mc = jnp.broadcast_to(mod_out[i, b].reshape(1, MOD_ROWS, d), (b, MOD_ROWS, d))
        pad = jnp.zeros((b, TAB_ROWS - 2 * MOD_ROWS, d), F32)
        tabs.append(jnp.concatenate([mc, ml, pad], axis=1))
    return tabs


def _split_gu_kernel(w_ref, perm_ref, wg_ref, wl_ref):
    for j in range(w_ref.shape[1] // (2 * LANES)):
        y = _dot(w_ref[:, 2 * LANES * j:2 * LANES * (j + 1)].astype(BF16), perm_ref[...])
        wg_ref[:, LANES * j:LANES * (j + 1)] = y[:, :LANES].astype(BF16)
        wl_ref[:, LANES * j:LANES * (j + 1)] = y[:, LANES:].astype(BF16)


def _split_gu(w_gu):
    lead, n2 = w_gu.shape[:-1], w_gu.shape[-1]
    rows = int(np.prod(lead))
    tr = _pick(rows, 512)
    src = np.concatenate([np.arange(0, 2 * LANES, 2), np.arange(1, 2 * LANES, 2)])
    perm = jnp.asarray(np.arange(2 * LANES)[:, None] == src[None, :], BF16)
    out_spec = pl.BlockSpec((tr, n2 // 2), lambda i: (i, 0))
    wg, wl = pl.pallas_call(
        _split_gu_kernel,
        grid=(rows // tr,),
        in_specs=[pl.BlockSpec((tr, n2), lambda i: (i, 0)), pl.BlockSpec((2 * LANES, 2 * LANES), lambda i: (0, 0))],
        out_specs=[out_spec, out_spec],
        out_shape=[jax.ShapeDtypeStruct((rows, n2 // 2), BF16)] * 2,
        compiler_params=_cparams(("parallel",)),
        name="split_gu_weights",
    )(w_gu.reshape(rows, n2), perm)
    return wg.reshape(*lead, n2 // 2), wl.reshape(*lead, n2 // 2)


def _moe_weights(router_w, router_b, w_gu, b_gu, w_dn, b_dn):
    depth, d, n_exp = router_w.shape
    f = w_dn.shape[2]
    wg, wl = _split_gu(w_gu)
    return {
        "rw": jnp.pad(router_w, ((0, 0), (0, 0), (0, LANES - n_exp))).astype(BF16),
        "rb": jnp.pad(router_b, ((0, 0), (0, LANES - n_exp))).reshape(depth, 1, LANES),
        "wg": wg.reshape(depth * n_exp, d, f),
        "wl": wl.reshape(depth * n_exp, d, f),
        "bg": b_gu[..., 0::2].reshape(depth * n_exp, 1, f),
        "bl": b_gu[..., 1::2].reshape(depth * n_exp, 1, f),
        "wd": w_dn.astype(BF16).reshape(depth * n_exp, f, d),
        "bd": b_dn.reshape(depth * n_exp, 1, d),
    }


def kernel(x, c, ctx, c_ctx, mod_w, mod_b, norm1_g, norm2_g, final_g, na_w_qkv, na_w_o, na_rpb, rg_w_y, rg_b_y,
           rg_w_x, rg_b_x, rg_conv_w, rg_conv_b, rg_w_a, rg_b_a, rg_w_i, rg_b_i, rg_lam, rg_w_out, rg_b_out,
           moe_router_w, moe_router_b, moe_w_gu, moe_b_gu, moe_w_dn, moe_b_dn):
    b, n_lat, d = x.shape
    n_ctx = ctx.shape[1]
    l = n_ctx + n_lat
    heads = na_rpb.shape[1]
    rows = n_lat // GRID_W
    kr = min((na_rpb.shape[2] + 1) // 2, rows)
    assert mod_w.shape[0] == 2 and n_lat % n_ctx == 0 and d % LANES == 0

    mod_rows = -(-(b + 1) // 8) * 8
    cc = jnp.concatenate([c, c_ctx[None, :], jnp.zeros((mod_rows - b - 1, d), F32)], axis=0)
    tab0, tab1 = _mod_tables(_modulation(cc, mod_w, mod_b), b, d)

    h = _prenorm(ctx, x, tab0, norm1_g[0])
    qkv = _matmul(h.reshape(b * l, d), na_w_qkv[0].astype(BF16), jnp.zeros((3 * d,), F32), BF16)
    bias = _na_bias_table(na_rpb[0], rows, kr)
    o = _attention(qkv.reshape(b, l, 3 * d), bias, n_ctx, heads)
    n_exp = moe_router_w.shape[2]
    w = _moe_weights(moe_router_w, moe_router_b, moe_w_gu, moe_b_gu, moe_w_dn, moe_b_dn)
    x1, hrow, logits = _proj(o, na_w_o[0].astype(BF16), jnp.zeros((d,), F32), (ctx, x), tab0, norm2_g[0],
                             w["rw"][0], w["rb"][0], n_ctx, latent_only=False)
    x2, h = _moe(hrow, logits, x1, tab0, tab1, norm1_g[1], w, 0, n_exp, n_ctx, "next")

    hf = h.reshape(b * l, d)
    xb = _matmul(hf, rg_w_x[0].astype(BF16), rg_b_x[0], F32).reshape(b, l, d)
    gy = _matmul(hf, rg_w_y[0].astype(BF16), rg_b_y[0], F32, act="gelu").reshape(b, l, d)
    u = _conv(xb, rg_conv_w[0], rg_conv_b[0], n_ctx)
    w_a, w_i = rg_w_a[0].astype(BF16), rg_w_i[0].astype(BF16)
    h_fwd = _scan(u, w_a, rg_b_a[0], w_i, rg_b_i[0], rg_lam[0], 0, n_ctx)
    hg = _scan(u, w_a, rg_b_a[0], w_i, rg_b_i[0], rg_lam[0], 1, n_ctx, h_fwd, gy)
    x1, hrow, logits = _proj(hg, rg_w_out[0].astype(BF16), rg_b_out[0], x2, tab1, norm2_g[1],
                             w["rw"][1], w["rb"][1], n_ctx, latent_only=True)
    _, out = _moe(hrow, logits, x1, tab1, tab1, final_g, w, 1, n_exp, 0, "final")
    return out
```

```python
import functools

import jax
import jax.numpy as jnp
import numpy as np
from jax import lax
from jax.experimental import pallas as pl
from jax.experimental.pallas import tpu as pltpu

F32 = jnp.float32
BF16 = jnp.bfloat16

LANES = 128
GRID_W = 64
TOP_K = 4
RG_C = 8.0
CONV_LEFT = 2
SWIGLU_ALPHA = 1.702
SWIGLU_LIMIT = 7.0
RMS_EPS = 1e-6
MOD_ROWS = 6
TAB_ROWS = 16
VMEM_LIMIT = 56 * 1024 * 1024


def _cparams(sem):
    return pltpu.CompilerParams(dimension_semantics=sem, vmem_limit_bytes=VMEM_LIMIT)


def _pick(n, pref, mult=8):
    for t in range(min(pref, n), 0, -1):
        if n % t == 0 and t % mult == 0:
            return t
    return n


def _dot(a, b):
    return jnp.dot(a, b, preferred_element_type=F32)


def _dot_nt(a, b):
    return lax.dot_general(a, b, (((1,), (1,)), ((), ())), preferred_element_type=F32)


def _split_bf16(x):
    hi = x.astype(BF16)
    lo = (x - hi.astype(F32)).astype(BF16)
    return hi, lo


def _dot3(a, w):
    a_hi, a_lo = _split_bf16(a)
    w_hi, w_lo = _split_bf16(w)
    return _dot(a_hi, w_hi) + _dot(a_lo, w_hi) + _dot(a_hi, w_lo)


def _norm_mod(x, g, shift, scale):
    y = x * lax.rsqrt(jnp.mean(x * x, axis=-1, keepdims=True) + RMS_EPS)
    return (y * g) * (1.0 + scale) + shift


def _to_token_rows(v):
    t, d = v.shape
    s_rows = d // LANES
    chunks = jnp.stack([v[:, s * LANES:(s + 1) * LANES] for s in range(s_rows)], axis=0)
    return pltpu.einshape("stl->tsl", chunks).reshape(t * s_rows, LANES)


def _from_token_rows(r, s_rows):
    return pltpu.einshape("tsl->stl", r.reshape(r.shape[0] // s_rows, s_rows, LANES))


def _tab_row(tab_ref, is_ctx, k):
    base = jnp.where(is_ctx, 0, MOD_ROWS)
    return tab_ref[0, pl.ds(base + k, 1), :]


def _mod_kernel(a_ref, w_ref, b_ref, o_ref):
    a = a_ref[...]
    a = a * jax.nn.sigmoid(a)
    o_ref[0] = _dot3(a, w_ref[0]) + b_ref[0]


def _modulation(cc, mod_w, mod_b):
    depth, d, n = mod_w.shape
    r = cc.shape[0]
    tn = _pick(n, 1024, LANES)
    return pl.pallas_call(
        _mod_kernel,
        grid=(depth, n // tn),
        in_specs=[
            pl.BlockSpec((r, d), lambda i, j: (0, 0)),
            pl.BlockSpec((1, d, tn), lambda i, j: (i, 0, j)),
            pl.BlockSpec((1, 1, tn), lambda i, j: (i, 0, j)),
        ],
        out_specs=pl.BlockSpec((1, r, tn), lambda i, j: (i, 0, j)),
        out_shape=jax.ShapeDtypeStruct((depth, r, n), F32),
        compiler_params=_cparams(("parallel", "parallel")),
        name="modulation",
    )(cc, mod_w, mod_b.reshape(depth, 1, n))


def _prenorm_kernel(ctx_ref, x_ref, tab_ref, g_ref, o_ref):
    is_ctx = pl.program_id(1) == 0
    xin = jnp.where(is_ctx, ctx_ref[0], x_ref[0])
    h = _norm_mod(xin, g_ref[...], _tab_row(tab_ref, is_ctx, 0), _tab_row(tab_ref, is_ctx, 1))
    o_ref[0] = h.astype(BF16)


def _prenorm(ctx, x, tab, g):
    b, n_ctx, d = ctx.shape
    tm = n_ctx
    nt = 1 + x.shape[1] // tm
    return pl.pallas_call(
        _prenorm_kernel,
        grid=(b, nt),
        in_specs=[
            pl.BlockSpec((1, tm, d), lambda i, j: (i, 0, 0)),
            pl.BlockSpec((1, tm, d), lambda i, j: (i, jnp.maximum(j - 1, 0), 0)),
            pl.BlockSpec((1, TAB_ROWS, d), lambda i, j: (i, 0, 0)),
            pl.BlockSpec((1, d), lambda i, j: (0, 0)),
        ],
        out_specs=pl.BlockSpec((1, tm, d), lambda i, j: (i, j, 0)),
        out_shape=jax.ShapeDtypeStruct((b, nt * tm, d), BF16),
        compiler_params=_cparams(("parallel", "parallel")),
        name="prenorm",
    )(ctx, x, tab, g.reshape(1, d))


def _gelu_tanh(x):
    return 0.5 * x * (1.0 + jnp.tanh(np.sqrt(2.0 / np.pi) * (x + 0.044715 * (x * x * x))))


def _matmul_kernel(a_ref, w_ref, b_ref, o_ref, *, act):
    y = _dot(a_ref[...], w_ref[...]) + b_ref[...]
    if act == "gelu":
        y = _gelu_tanh(y)
    o_ref[...] = y.astype(o_ref.dtype)


def _matmul(a, w, bias, out_dtype, act=None):
    m, k = a.shape
    n = w.shape[1]
    tm = _pick(m, 1024)
    tn = _pick(n, 512, LANES)
    return pl.pallas_call(
        functools.partial(_matmul_kernel, act=act),
        grid=(m // tm, n // tn),
        in_specs=[
            pl.BlockSpec((tm, k), lambda i, j: (i, 0)),
            pl.BlockSpec((k, tn), lambda i, j: (0, j)),
            pl.BlockSpec((1, tn), lambda i, j: (0, j)),
        ],
        out_specs=pl.BlockSpec((tm, tn), lambda i, j: (i, j)),
        out_shape=jax.ShapeDtypeStruct((m, n), out_dtype),
        compiler_params=_cparams(("parallel", "parallel")),
        name="matmul_" + (act or "linear"),
    )(a, w, bias.reshape(1, n))


def _softmax_parts(parts):
    m = parts[0].max(axis=-1, keepdims=True)
    for s in parts[1:]:
        m = jnp.maximum(m, s.max(axis=-1, keepdims=True))
    ps = [jnp.exp(s - m) for s in parts]
    den = ps[0].sum(axis=-1, keepdims=True)
    for p in ps[1:]:
        den = den + p.sum(axis=-1, keepdims=True)
    return ps, den


ATTN_ROWS_PER_ITER = 16


def _attn_kernel(q_ref, k_ref, v_ref, bias_ref, wgu_ref, perm_ref, wdn_ref, o_ref, wg_ref, wl_ref, wd_ref,
                 *, n_ctx, rows, kr, scale):
    kc = k_ref[0, 0:n_ctx, :]
    vc = v_ref[0, 0:n_ctx, :]
    (p,), den = _softmax_parts([_dot_nt(q_ref[0, 0:n_ctx, :], kc) * scale])
    o_ref[0, 0:n_ctx, :] = (_dot(p.astype(BF16), vc) / den).astype(BF16)

    group = next(g for g in (ATTN_ROWS_PER_ITER, 2, 1) if rows % g == 0)
    n_iter = rows // group
    gu_rows, dn_rows = wgu_ref.shape[0], wdn_ref.shape[0]
    spread = gu_rows % (8 * n_iter) == 0 and dn_rows % (8 * n_iter) == 0

    def convert(i, n_parts):
        r_gu, r_dn = gu_rows // n_parts, dn_rows // n_parts
        gu = pl.ds(pl.multiple_of(i * r_gu, 8), r_gu)
        dn = pl.ds(pl.multiple_of(i * r_dn, 8), r_dn)
        for j in range(wgu_ref.shape[1] // (2 * LANES)):
            y = _dot(wgu_ref[gu, 2 * LANES * j:2 * LANES * (j + 1)].astype(BF16), perm_ref[...])
            wg_ref[gu, LANES * j:LANES * (j + 1)] = y[:, :LANES].astype(BF16)
            wl_ref[gu, LANES * j:LANES * (j + 1)] = y[:, LANES:].astype(BF16)
        wd_ref[dn, :] = wdn_ref[dn, :].astype(BF16)

    if not spread:
        convert(0, 1)

    def row_group(i, carry):
        if spread:
            convert(i, n_iter)
        rr = [i * group + j for j in range(group)]
        rs = [jnp.clip(r - kr // 2, 0, rows - kr) for r in rr]
        q0 = [pl.multiple_of(n_ctx + r * GRID_W, GRID_W) for r in rr]
        k0 = [pl.multiple_of(n_ctx + s * GRID_W, GRID_W) for s in rs]
        scores = []
        for j in range(group):
            q = q_ref[0, pl.ds(q0[j], GRID_W), :]
            s_lat = _dot_nt(q, k_ref[0, pl.ds(k0[j], kr * GRID_W), :]) * scale + bias_ref[0, rr[j] - rs[j]]
            scores.append([s_lat, _dot_nt(q, kc) * scale])
        probs = [_softmax_parts(s) for s in scores]
        for j in range(group):
            (p_lat, p_ctx), den = probs[j]
            o = _dot(p_lat.astype(BF16), v_ref[0, pl.ds(k0[j], kr * GRID_W), :]) + _dot(p_ctx.astype(BF16), vc)
            o_ref[0, pl.ds(q0[j], GRID_W), :] = (o / den).astype(BF16)
        return carry

    lax.fori_loop(0, n_iter, row_group, 0)


def _na_bias_table(rpb, rows, kr):
    h, n_dr, n_dc = rpb.shape
    win_rows, win_cols = (n_dr + 1) // 2, (n_dc + 1) // 2
    col = np.arange(GRID_W)
    col_start = np.clip(col - win_cols // 2, 0, GRID_W - win_cols)
    col_mask = (col[None, :] >= col_start[:, None]) & (col[None, :] < col_start[:, None] + win_cols)
    dc_idx = np.clip(col[None, :] - col[:, None], 1 - win_cols, win_cols - 1) + win_cols - 1
    dr_idx = np.arange(kr)[None, :] - np.arange(kr)[:, None] + win_rows - 1
    t = rpb[:, dr_idx][:, :, :, dc_idx]
    t = jnp.where(col_mask[None, None, None], t.astype(F32), -jnp.inf)
    return t.transpose(0, 1, 3, 2, 4).reshape(h, kr, GRID_W, kr * GRID_W)


def _attention(qkv, bias, n_ctx, heads, w_gu, w_dn):
    b, l, d3 = qkv.shape
    d = d3 // 3
    dh = d // heads
    rows = (l - n_ctx) // GRID_W
    kr = bias.shape[1]
    n_steps = heads * b
    gu_rows, n2 = w_gu.shape
    dn_rows, dn_cols = w_dn.shape
    assert gu_rows % (8 * n_steps) == 0 and dn_rows % (8 * n_steps) == 0
    tg, td = gu_rows // n_steps, dn_rows // n_steps
    src = np.concatenate([np.arange(0, 2 * LANES, 2), np.arange(1, 2 * LANES, 2)])
    perm = jnp.asarray(np.arange(2 * LANES)[:, None] == src[None, :], BF16)
    kern = functools.partial(_attn_kernel, n_ctx=n_ctx, rows=rows, kr=kr, scale=dh ** -0.5)
    slab = lambda h, i: (h * b + i, 0)
    return pl.pallas_call(
        kern,
        grid=(heads, b),
        in_specs=[
            pl.BlockSpec((1, l, dh), lambda h, i: (i, 0, h)),
            pl.BlockSpec((1, l, dh), lambda h, i: (i, 0, heads + h)),
            pl.BlockSpec((1, l, dh), lambda h, i: (i, 0, 2 * heads + h)),
            pl.BlockSpec((1, kr, GRID_W, kr * GRID_W), lambda h, i: (h, 0, 0, 0)),
            pl.BlockSpec((tg, n2), slab),
            pl.BlockSpec((2 * LANES, 2 * LANES), lambda h, i: (0, 0)),
            pl.BlockSpec((td, dn_cols), slab),
        ],
        out_specs=[
            pl.BlockSpec((1, l, dh), lambda h, i: (i, 0, h)),
            pl.BlockSpec((tg, n2 // 2), slab),
            pl.BlockSpec((tg, n2 // 2), slab),
            pl.BlockSpec((td, dn_cols), slab),
        ],
        out_shape=[
            jax.ShapeDtypeStruct((b, l, d), BF16),
            jax.ShapeDtypeStruct((gu_rows, n2 // 2), BF16),
            jax.ShapeDtypeStruct((gu_rows, n2 // 2), BF16),
            jax.ShapeDtypeStruct((dn_rows, dn_cols), BF16),
        ],
        compiler_params=_cparams(("parallel", "parallel")),
        name="na_attention",
    )(qkv, qkv, qkv, bias, w_gu, perm, w_dn)


def _proj_kernel(*refs, ctx_tiles, tile_off, split_residual):
    if split_residual:
        a_ref, w_ref, b_ref, ctx_ref, x_ref, tab_ref, g_ref, rw_ref, rb_ref, x1_ref, hrow_ref, lg_ref = refs
    else:
        a_ref, w_ref, b_ref, x_ref, tab_ref, g_ref, rw_ref, rb_ref, x1_ref, hrow_ref, lg_ref = refs
    is_ctx = pl.program_id(1) + tile_off < ctx_tiles
    resid = jnp.where(is_ctx, ctx_ref[0], x_ref[0]) if split_residual else x_ref[0]
    y = _dot(a_ref[0], w_ref[...]) + b_ref[...]
    x1 = resid + _tab_row(tab_ref, is_ctx, 2) * y
    x1_ref[0] = x1
    h2 = _norm_mod(x1, g_ref[...], _tab_row(tab_ref, is_ctx, 3), _tab_row(tab_ref, is_ctx, 4))
    lg_ref[...] = _dot(h2.astype(BF16), rw_ref[...]) + rb_ref[...]
    hrow_ref[...] = _to_token_rows(h2)


def _proj(a, w, bias, resid, tab, g2, rw, rb, n_ctx, latent_only):
    split = isinstance(resid, tuple)
    b, l, d = a.shape
    tm = n_ctx
    off = n_ctx // tm if latent_only else 0
    nt = l // tm - off
    s_rows = d // LANES
    t = b * nt * tm
    kern = functools.partial(_proj_kernel, ctx_tiles=n_ctx // tm, tile_off=off, split_residual=split)
    if split:
        assert not latent_only and n_ctx == tm
        resid_specs = [pl.BlockSpec((1, tm, d), lambda i, j: (i, 0, 0)),
                       pl.BlockSpec((1, tm, d), lambda i, j: (i, jnp.maximum(j - 1, 0), 0))]
        resid_args = list(resid)
    else:
        resid_specs = [pl.BlockSpec((1, tm, d), lambda i, j: (i, j + off, 0))]
        resid_args = [resid]
    return pl.pallas_call(
        kern,
        grid=(b, nt),
        in_specs=[
            pl.BlockSpec((1, tm, d), lambda i, j: (i, j + off, 0)),
            pl.BlockSpec((d, d), lambda i, j: (0, 0)),
            pl.BlockSpec((1, d), lambda i, j: (0, 0)),
        ] + resid_specs + [
            pl.BlockSpec((1, TAB_ROWS, d), lambda i, j: (i, 0, 0)),
            pl.BlockSpec((1, d), lambda i, j: (0, 0)),
            pl.BlockSpec((d, LANES), lambda i, j: (0, 0)),
            pl.BlockSpec((1, LANES), lambda i, j: (0, 0)),
        ],
        out_specs=[
            pl.BlockSpec((1, tm, d), lambda i, j: (i, j, 0)),
            pl.BlockSpec((tm * s_rows, LANES), lambda i, j: (i * nt + j, 0)),
            pl.BlockSpec((tm, LANES), lambda i, j: (i * nt + j, 0)),
        ],
        out_shape=[
            jax.ShapeDtypeStruct((b, nt * tm, d), F32),
            jax.ShapeDtypeStruct((t * s_rows, LANES), F32),
            jax.ShapeDtypeStruct((t, LANES), F32),
        ],
        compiler_params=_cparams(("parallel", "parallel")),
        name="mixer_proj",
    )(a, w, bias.reshape(1, d), *resid_args, tab, g2.reshape(1, d), rw, rb)


def _route_kernel(lg_ref, e_ref, g_ref, r_ref, cnt_ref, carry_ref, *, n_exp):
    @pl.when(pl.program_id(0) == 0)
    def _():
        carry_ref[...] = jnp.zeros_like(carry_ref)

    lg = lg_ref[...]
    tm = lg.shape[0]
    lane = lax.broadcasted_iota(jnp.int32, lg.shape, 1).astype(F32)
    cur = jnp.where(lane < n_exp, lg, -jnp.inf)
    multi = jnp.zeros(lg.shape, F32)
    vals, idxs = [], []
    for _ in range(TOP_K):
        m = cur.max(axis=-1, keepdims=True)
        idx = jnp.where(cur == m, lane, float(LANES)).min(axis=-1, keepdims=True)
        sel = lane == idx
        multi = jnp.where(sel, 1.0, multi)
        cur = jnp.where(sel, -jnp.inf, cur)
        vals.append(m)
        idxs.append(idx)
    exps = [jnp.exp(v - vals[0]) for v in vals]
    den = exps[0]
    for e in exps[1:]:
        den = den + e
    tri = (lax.broadcasted_iota(jnp.int32, (tm, tm), 0) > lax.broadcasted_iota(jnp.int32, (tm, tm), 1))
    pref = _dot(jnp.where(tri, 1.0, 0.0).astype(BF16), multi.astype(BF16))
    tot = carry_ref[...] + pref
    e_out = jnp.zeros(lg.shape, F32)
    g_out = jnp.zeros(lg.shape, F32)
    r_out = jnp.zeros(lg.shape, F32)
    for k in range(TOP_K):
        rank_k = jnp.where(lane == idxs[k], tot, 0.0).sum(axis=-1, keepdims=True)
        e_out = jnp.where(lane == k, idxs[k], e_out)
        g_out = jnp.where(lane == k, exps[k] / den, g_out)
        r_out = jnp.where(lane == k, rank_k, r_out)
    e_ref[...] = e_out.astype(jnp.int32)
    g_ref[...] = g_out
    r_ref[...] = r_out.astype(jnp.int32)
    carry_ref[...] = carry_ref[...] + multi.sum(axis=0, keepdims=True)
    cnt_ref[...] = carry_ref[...]


def _route(logits, n_exp):
    t = logits.shape[0]
    tm = _pick(t, 256)
    spec = pl.BlockSpec((tm, LANES), lambda i: (i, 0))
    return pl.pallas_call(
        functools.partial(_route_kernel, n_exp=n_exp),
        grid=(t // tm,),
        in_specs=[spec],
        out_specs=[spec, spec, spec, pl.BlockSpec((1, LANES), lambda i: (0, 0))],
        out_shape=[
            jax.ShapeDtypeStruct((t, LANES), jnp.int32),
            jax.ShapeDtypeStruct((t, LANES), F32),
            jax.ShapeDtypeStruct((t, LANES), jnp.int32),
            jax.ShapeDtypeStruct((1, LANES), F32),
        ],
        scratch_shapes=[pltpu.VMEM((1, LANES), F32)],
        compiler_params=_cparams(("arbitrary",)),
        name="router",
    )(logits)


def _dispatch_kernel(dest_ref, h_ref, xs_ref, sem, *, td, s_rows):
    def issue(t, carry):
        src = h_ref.at[pl.ds(pl.multiple_of(t * s_rows, s_rows), s_rows), :]
        for k in range(TOP_K):
            d = dest_ref[0, 0, t * TOP_K + k]
            dst = xs_ref.at[pl.ds(pl.multiple_of(d * s_rows, s_rows), s_rows), :]
            pltpu.make_async_copy(src, dst, sem).start()
        return carry

    lax.fori_loop(0, td, issue, 0)
    for _ in range(TOP_K):
        pltpu.make_async_copy(h_ref, xs_ref.at[pl.ds(0, td * s_rows), :], sem).wait()


DISPATCH_TOKENS = 1024


def _dispatch(hrow, dest, s_rows):
    t = dest.shape[0] // TOP_K
    td = _pick(t, DISPATCH_TOKENS)
    nt = t // td
    return pl.pallas_call(
        functools.partial(_dispatch_kernel, td=td, s_rows=s_rows),
        grid=(nt,),
        in_specs=[
            pl.BlockSpec((1, 1, td * TOP_K), lambda i: (i, 0, 0), memory_space=pltpu.SMEM),
            pl.BlockSpec((td * s_rows, LANES), lambda i: (i, 0)),
        ],
        out_specs=pl.BlockSpec(memory_space=pl.ANY),
        out_shape=jax.ShapeDtypeStruct((t * TOP_K * s_rows, LANES), F32),
        scratch_shapes=[pltpu.SemaphoreType.DMA(())],
        compiler_params=pltpu.CompilerParams(dimension_semantics=("arbitrary",), vmem_limit_bytes=VMEM_LIMIT,
                                             has_side_effects=True),
        name="moe_dispatch",
    )(dest.reshape(nt, 1, td * TOP_K), hrow)


def _gmm_kernel(it_tile, it_e, it_lo, it_hi, it_first, it_valid,
                xs_ref, wg_ref, wl_ref, bg_ref, bl_ref, wd_ref, bd_ref, ys_ref, xb_ref, *, tg, s_rows, fk):
    m = pl.program_id(0)
    tsub = _pick(tg, 128)
    f = wd_ref.shape[1]

    @pl.when(it_valid[m] == 1)
    def _():
        for t0 in range(0, tg, tsub):
            xt = _from_token_rows(xs_ref[pl.ds(t0 * s_rows, tsub * s_rows), :], s_rows)
            for s in range(s_rows):
                xb_ref[pl.ds(t0, tsub), s * LANES:(s + 1) * LANES] = xt[s].astype(BF16)

        x = xb_ref[...]
        y = None
        for f0 in range(0, f, fk):
            cols = slice(f0, f0 + fk)
            glu = jnp.minimum(_dot(x, wg_ref[0, :, cols]) + bg_ref[0, :, cols], SWIGLU_LIMIT)
            lin = jnp.clip(_dot(x, wl_ref[0, :, cols]) + bl_ref[0, :, cols], -SWIGLU_LIMIT, SWIGLU_LIMIT)
            act = glu * jax.nn.sigmoid(SWIGLU_ALPHA * glu) * (lin + 1.0)
            part = _dot(act.astype(BF16), wd_ref[0, cols, :])
            y = part if y is None else y + part
        y = y + bd_ref[0]

        for t0 in range(0, tg, tsub):
            rows = pl.ds(t0 * s_rows, tsub * s_rows)
            yr = _to_token_rows(y[t0:t0 + tsub])
            row = t0 * s_rows + lax.broadcasted_iota(jnp.int32, (tsub * s_rows, 1), 0)
            mine = (row >= it_lo[m] * s_rows) & (row < it_hi[m] * s_rows)

            @pl.when(it_first[m] == 1)
            def _():
                ys_ref[rows, :] = jnp.where(mine, yr, 0.0)

            @pl.when(it_first[m] == 0)
            def _():
                ys_ref[rows, :] = jnp.where(mine, yr, ys_ref[rows, :])


def _gmm_items(counts, n_tiles, tg, max_items):
    n_exp = counts.shape[0]
    ends = jnp.cumsum(counts)
    starts = ends - counts
    def count_le(sorted_vals, q):
        return jnp.sum(sorted_vals[None, :] <= q[:, None], axis=1, dtype=jnp.int32)

    tile0 = jnp.arange(n_tiles, dtype=jnp.int32) * tg
    e_lo = jnp.minimum(count_le(ends, tile0), n_exp - 1)
    e_hi = jnp.minimum(count_le(ends, tile0 + tg - 1), n_exp - 1)
    n_items = e_hi - e_lo + 1
    item_end = jnp.cumsum(n_items)
    item_start = item_end - n_items
    total = item_end[-1]
    m = jnp.arange(max_items, dtype=jnp.int32)
    valid = m < total
    tile = jnp.minimum(count_le(item_end, m), n_tiles - 1)
    e = jnp.where(valid, e_lo[tile] + (m - item_start[tile]), e_hi[n_tiles - 1]).astype(jnp.int32)
    lo = jnp.clip(starts[e] - tile * tg, 0, tg).astype(jnp.int32)
    hi = jnp.clip(ends[e] - tile * tg, 0, tg).astype(jnp.int32)
    first = (m == item_start[tile]).astype(jnp.int32)
    return tile, e, lo, hi, first, valid.astype(jnp.int32)


GMM_ROW_TILE = 256
GMM_F_CHUNK = 1024


def _gmm(xs, counts, w, layer, s_rows):
    n_exp = counts.shape[0]
    d, f = w["wg"].shape[1:]
    p = xs.shape[0] // s_rows
    tg = _pick(p, GMM_ROW_TILE)
    fk = _pick(f, GMM_F_CHUNK, LANES)
    n_tiles = p // tg
    max_items = n_tiles + n_exp - 1
    tile, e, lo, hi, first, valid = _gmm_items(counts, n_tiles, tg, max_items)
    items = (tile, e + layer * n_exp, lo, hi, first, valid)

    def wspec(block, buffers=1):
        return pl.BlockSpec(block, lambda m, t, e, lo, hi, fi, va: (e[m], 0, 0), pipeline_mode=pl.Buffered(buffers))

    rows_spec = pl.BlockSpec((tg * s_rows, LANES), lambda m, t, e, lo, hi, fi, va: (t[m], 0))
    grid_spec = pltpu.PrefetchScalarGridSpec(
        num_scalar_prefetch=6,
        grid=(max_items,),
        in_specs=[
            rows_spec,
            wspec((1, d, f), 2),
            wspec((1, d, f), 2),
            wspec((1, 1, f), 2),
            wspec((1, 1, f), 2),
            wspec((1, f, d)),
            wspec((1, 1, d), 2),
        ],
        out_specs=rows_spec,
        scratch_shapes=[pltpu.VMEM((tg, d), BF16)],
    )
    return pl.pallas_call(
        functools.partial(_gmm_kernel, tg=tg, s_rows=s_rows, fk=fk),
        grid_spec=grid_spec,
        out_shape=jax.ShapeDtypeStruct(xs.shape, F32),
        compiler_params=_cparams(("arbitrary",)),
        name="moe_experts",
    )(*items, xs, w["wg"], w["wl"], w["bg"], w["bl"], w["wd"], w["bd"])


def _combine_kernel(dest_ref, dest_next_ref, ys_ref, gate_ref, x_ref, tab_ref, tabn_ref, g_ref,
                    x2_ref, h_ref, buf_ref, moe_ref, sem, *, tc, s_rows, ctx_tiles, n_steps, mode):
    step = pl.program_id(0) * pl.num_programs(1) + pl.program_id(1)
    slot = step % 2

    def gather(dref, slot_idx):
        def issue(t, carry):
            for k in range(TOP_K):
                d = dref[0, 0, t * TOP_K + k]
                src = ys_ref.at[pl.ds(pl.multiple_of(d * s_rows, s_rows), s_rows), :]
                dst = buf_ref.at[slot_idx, pl.ds(pl.multiple_of((k * tc + t) * s_rows, s_rows), s_rows), :]
                pltpu.make_async_copy(src, dst, sem.at[slot_idx]).start()
            return carry

        lax.fori_loop(0, tc, issue, 0)

    @pl.when(step == 0)
    def _():
        gather(dest_ref, 0)

    @pl.when(step + 1 < n_steps)
    def _():
        gather(dest_next_ref, 1 - slot)

    pltpu.make_async_copy(buf_ref.at[slot], buf_ref.at[slot], sem.at[slot]).wait()

    gates = gate_ref[...]
    for k in range(TOP_K):
        chunks = _from_token_rows(buf_ref[slot, pl.ds(k * tc * s_rows, tc * s_rows), :], s_rows)
        for s in range(s_rows):
            term = chunks[s] * gates[:, k:k + 1]
            if k == 0:
                moe_ref[:, s * LANES:(s + 1) * LANES] = term
            else:
                moe_ref[:, s * LANES:(s + 1) * LANES] += term

    is_ctx = pl.program_id(1) < ctx_tiles
    x2 = x_ref[0] + _tab_row(tab_ref, is_ctx, 5) * moe_ref[...]
    x2_ref[0] = x2
    if mode == "next":
        h = _norm_mod(x2, g_ref[...], _tab_row(tabn_ref, is_ctx, 0), _tab_row(tabn_ref, is_ctx, 1))
        h_ref[0] = h.astype(h_ref.dtype)
    else:
        y = x2 * lax.rsqrt(jnp.mean(x2 * x2, axis=-1, keepdims=True) + RMS_EPS)
        h_ref[0] = (y * g_ref[...]).astype(h_ref.dtype)


def _combine(ys, dest, gates, x1, tab, tab_next, g_next, n_ctx_rows, s_rows, mode):
    b, l, d = x1.shape
    tc = _pick(min(l, 128) if n_ctx_rows == 0 else n_ctx_rows, 128)
    nt = l // tc
    n_steps = b * nt
    dest3 = dest.reshape(n_steps, 1, tc * TOP_K)
    kern = functools.partial(_combine_kernel, tc=tc, s_rows=s_rows, ctx_tiles=n_ctx_rows // tc,
                             n_steps=n_steps, mode=mode)
    out_dtype = BF16 if mode == "next" else F32
    return pl.pallas_call(
        kern,
        grid=(b, nt),
        in_specs=[
            pl.BlockSpec((1, 1, tc * TOP_K), lambda i, j: (i * nt + j, 0, 0), memory_space=pltpu.SMEM),
            pl.BlockSpec((1, 1, tc * TOP_K), lambda i, j: (jnp.minimum(i * nt + j + 1, n_steps - 1), 0, 0),
                         memory_space=pltpu.SMEM),
            pl.BlockSpec(memory_space=pl.ANY),
            pl.BlockSpec((tc, LANES), lambda i, j: (i * nt + j, 0)),
            pl.BlockSpec((1, tc, d), lambda i, j: (i, j, 0)),
            pl.BlockSpec((1, TAB_ROWS, d), lambda i, j: (i, 0, 0)),
            pl.BlockSpec((1, TAB_ROWS, d), lambda i, j: (i, 0, 0)),
            pl.BlockSpec((1, d), lambda i, j: (0, 0)),
        ],
        out_specs=[
            pl.BlockSpec((1, tc, d), lambda i, j: (i, j, 0)),
            pl.BlockSpec((1, tc, d), lambda i, j: (i, j, 0)),
        ],
        out_shape=[
            jax.ShapeDtypeStruct((b, l, d), F32),
            jax.ShapeDtypeStruct((b, l, d), out_dtype),
        ],
        scratch_shapes=[
            pltpu.VMEM((2, tc * TOP_K * s_rows, LANES), F32),
            pltpu.VMEM((tc, d), F32),
            pltpu.SemaphoreType.DMA((2,)),
        ],
        compiler_params=_cparams(("arbitrary", "arbitrary")),
        name="moe_combine_" + mode,
    )(dest3, dest3, ys, gates, x1, tab, tab_next, g_next.reshape(1, d))


def _moe(hrow, logits, x1, tab, tab_next, g_next, w, layer, n_exp, n_ctx_rows, mode):
    s_rows = x1.shape[-1] // LANES
    e_pad, g_pad, r_pad, cnt = _route(logits, n_exp)
    top_e, rank = e_pad[:, :TOP_K], r_pad[:, :TOP_K]
    counts = cnt[0, :n_exp].astype(jnp.int32)
    starts = jnp.cumsum(counts) - counts
    first_slot = jnp.sum(jnp.where(top_e[..., None] == jnp.arange(n_exp, dtype=jnp.int32), starts, 0), axis=-1)
    dest = (first_slot + rank).reshape(-1)
    xs = _dispatch(hrow, dest, s_rows)
    ys = _gmm(xs, counts, w, layer, s_rows)
    return _combine(ys, dest, g_pad, x1, tab, tab_next, g_next, n_ctx_rows, s_rows, mode)


def _conv_kernel(x_ref, w_ref, b_ref, o_ref, *, n_ctx):
    l = x_ref.shape[1]
    w = w_ref[...]
    for s0, n in ((0, n_ctx), (n_ctx, l - n_ctx)):
        x = x_ref[0, s0:s0 + n, :]
        row = lax.broadcasted_iota(jnp.int32, (n, 1), 0)
        acc = x * w[CONV_LEFT:CONV_LEFT + 1]
        for j in range(w.shape[0]):
            off = j - CONV_LEFT
            if off == 0:
                continue
            shifted = pltpu.roll(x, (-off) % n, 0)
            ok = (row + off >= 0) & (row + off < n)
            acc = acc + jnp.where(ok, shifted, 0.0) * w[j:j + 1]
        o_ref[0, s0:s0 + n, :] = acc + b_ref[...]


def _conv(xb, conv_w, conv_b, n_ctx):
    b, l, d = xb.shape
    dt = _pick(d, 256, LANES)
    return pl.pallas_call(
        functools.partial(_conv_kernel, n_ctx=n_ctx),
        grid=(b, d // dt),
        in_specs=[
            pl.BlockSpec((1, l, dt), lambda i, j: (i, 0, j)),
            pl.BlockSpec((conv_w.shape[0], dt), lambda i, j: (0, j)),
            pl.BlockSpec((1, dt), lambda i, j: (0, j)),
        ],
        out_specs=pl.BlockSpec((1, l, dt), lambda i, j: (i, 0, j)),
        out_shape=jax.ShapeDtypeStruct((b, l, d), F32),
        compiler_params=_cparams(("parallel", "parallel")),
        name="rg_conv",
    )(xb, conv_w, conv_b.reshape(1, d))


SCAN_PAD = 8
SCAN_UNROLL = 8


def _scan_kernel(*refs, nb, tc, nh, reverse):
    n_in = 8 if reverse else 6
    u_ref, wa_ref, ba_ref, wi_ref, bi_ref, lam_ref = refs[:6]
    o_ref = refs[n_in]
    scratch = refs[n_in + 1:]
    a_s, x_s, h_s = scratch[0:nh], scratch[nh:2 * nh], scratch[2 * nh:3 * nh]
    carry = scratch[3 * nh]
    ts = tc + SCAN_PAD

    @pl.when(pl.program_id(1) == 0)
    def _():
        carry[...] = jnp.zeros_like(carry)

    neg = -lam_ref[0]
    softplus = jnp.maximum(neg, 0.0) + jnp.log1p(jnp.exp(-jnp.abs(neg)))
    for bi in range(nb):
        u = u_ref[bi]
        ub = u.astype(BF16)
        r = jax.nn.sigmoid(_dot(ub, wa_ref[0, 0]) + ba_ref[0, 0])
        i = jax.nn.sigmoid(_dot(ub, wi_ref[0, 0]) + bi_ref[0, 0])
        log_a = (-RG_C) * r * softplus
        a = jnp.exp(log_a)
        xin = jnp.sqrt(1.0 - a * a) * (i * u)
        for p in range(nh):
            a_s[p][pl.ds(bi * ts, tc), :] = a[:, p * LANES:(p + 1) * LANES]
            x_s[p][pl.ds(bi * ts, tc), :] = xin[:, p * LANES:(p + 1) * LANES]

    def block(j, hs):
        hs = list(hs)
        for q in range(SCAN_UNROLL):
            t = j * SCAN_UNROLL + q
            if reverse:
                t = tc - 1 - t
            for p in range(nh):
                hs[p] = a_s[p][pl.ds(t, nb, stride=ts), :] * hs[p] + x_s[p][pl.ds(t, nb, stride=ts), :]
                h_s[p][pl.ds(t, nb, stride=ts), :] = hs[p]
        return tuple(hs)

    h0 = tuple(carry[:, p * LANES:(p + 1) * LANES] for p in range(nh))
    hs = lax.fori_loop(0, tc // SCAN_UNROLL, block, h0)
    for p in range(nh):
        carry[:, p * LANES:(p + 1) * LANES] = hs[p]
    for bi in range(nb):
        for p in range(nh):
            h = h_s[p][pl.ds(bi * ts, tc), :]
            cols = slice(p * LANES, (p + 1) * LANES)
            if reverse:
                hf_ref, gy_ref = refs[6], refs[7]
                o_ref[bi, :, cols] = ((hf_ref[bi, :, cols] + h) * gy_ref[bi, :, cols]).astype(o_ref.dtype)
            else:
                o_ref[bi, :, cols] = h


def _scan(u, w_a, b_a, w_i, b_i, lam, dirn, n_ctx, h_fwd=None, gy=None):
    b, l, d = u.shape
    n_blk, w = w_a.shape[1], w_a.shape[2]
    tc = _pick(n_ctx, 128)
    nt, nc = l // tc, n_ctx // tc
    reverse = dirn == 1

    def chunk(j):
        if not reverse:
            return j
        return jnp.where(j < nc, nc - 1 - j, nt - 1 - (j - nc))

    blk = pl.BlockSpec((b, tc, w), lambda g, j: (0, chunk(j), g))
    in_specs = [
        blk,
        pl.BlockSpec((1, 1, w, w), lambda g, j: (dirn, g, 0, 0)),
        pl.BlockSpec((1, 1, 1, w), lambda g, j: (dirn, g, 0, 0)),
        pl.BlockSpec((1, 1, w, w), lambda g, j: (dirn, g, 0, 0)),
        pl.BlockSpec((1, 1, 1, w), lambda g, j: (dirn, g, 0, 0)),
        pl.BlockSpec((1, 1, w), lambda g, j: (dirn, 0, g)),
    ]
    args = [u, w_a, b_a.reshape(2, n_blk, 1, w), w_i, b_i.reshape(2, n_blk, 1, w), lam.reshape(2, 1, d)]
    if reverse:
        in_specs += [blk, blk]
        args += [h_fwd, gy]
    ts = tc + SCAN_PAD
    return pl.pallas_call(
        functools.partial(_scan_kernel, nb=b, tc=tc, nh=w // LANES, reverse=reverse),
        grid=(n_blk, nt),
        in_specs=in_specs,
        out_specs=blk,
        out_shape=jax.ShapeDtypeStruct((b, l, d), BF16 if reverse else F32),
        scratch_shapes=[pltpu.VMEM((b * ts, LANES), F32)] * (3 * (w // LANES)) + [pltpu.VMEM((b, w), F32)],
        compiler_params=_cparams(("parallel", "arbitrary")),
        name="rg_scan_" + ("rev" if reverse else "fwd"),
    )(*args)


def _mod_tables(mod_out, b, d):
    tabs = []
    for i in range(mod_out.shape[0]):
        ml = mod_out[i, :b].reshape(b, MOD_ROWS, d)
        mc = jnp.broadcast_to(mod_out[i, b].reshape(1, MOD_ROWS, d), (b, MOD_ROWS, d))
        pad = jnp.zeros((b, TAB_ROWS - 2 * MOD_ROWS, d), F32)
        tabs.append(jnp.concatenate([mc, ml, pad], axis=1))
    return tabs


def _moe_weights(router_w, router_b, wg, wl, b_gu, wd, b_dn):
    depth, d, n_exp = router_w.shape
    f = wg.shape[-1]
    return {
        "rw": jnp.pad(router_w, ((0, 0), (0, 0), (0, LANES - n_exp))).astype(BF16),
        "rb": jnp.pad(router_b, ((0, 0), (0, LANES - n_exp))).reshape(depth, 1, LANES),
        "wg": wg.reshape(depth * n_exp, d, f),
        "wl": wl.reshape(depth * n_exp, d, f),
        "bg": b_gu[..., 0::2].reshape(depth * n_exp, 1, f),
        "bl": b_gu[..., 1::2].reshape(depth * n_exp, 1, f),
        "wd": wd.reshape(depth * n_exp, f, d),
        "bd": b_dn.reshape(depth * n_exp, 1, d),
    }


def kernel(x, c, ctx, c_ctx, mod_w, mod_b, norm1_g, norm2_g, final_g, na_w_qkv, na_w_o, na_rpb, rg_w_y, rg_b_y,
           rg_w_x, rg_b_x, rg_conv_w, rg_conv_b, rg_w_a, rg_b_a, rg_w_i, rg_b_i, rg_lam, rg_w_out, rg_b_out,
           moe_router_w, moe_router_b, moe_w_gu, moe_b_gu, moe_w_dn, moe_b_dn):
    b, n_lat, d = x.shape
    n_ctx = ctx.shape[1]
    l = n_ctx + n_lat
    heads = na_rpb.shape[1]
    rows = n_lat // GRID_W
    kr = min((na_rpb.shape[2] + 1) // 2, rows)
    assert mod_w.shape[0] == 2 and n_lat % n_ctx == 0 and d % LANES == 0

    mod_rows = -(-(b + 1) // 8) * 8
    cc = jnp.concatenate([c, c_ctx[None, :], jnp.zeros((mod_rows - b - 1, d), F32)], axis=0)
    tab0, tab1 = _mod_tables(_modulation(cc, mod_w, mod_b), b, d)

    h = _prenorm(ctx, x, tab0, norm1_g[0])
    qkv = _matmul(h.reshape(b * l, d), na_w_qkv[0].astype(BF16), jnp.zeros((3 * d,), F32), BF16)
    bias = _na_bias_table(na_rpb[0], rows, kr)
    o, wg, wl, wd = _attention(qkv.reshape(b, l, 3 * d), bias, n_ctx, heads,
                               moe_w_gu.reshape(-1, moe_w_gu.shape[-1]), moe_w_dn.reshape(-1, moe_w_dn.shape[-1]))
    n_exp = moe_router_w.shape[2]
    w = _moe_weights(moe_router_w, moe_router_b, wg, wl, moe_b_gu, wd, moe_b_dn)
    x1, hrow, logits = _proj(o, na_w_o[0].astype(BF16), jnp.zeros((d,), F32), (ctx, x), tab0, norm2_g[0],
                             w["rw"][0], w["rb"][0], n_ctx, latent_only=False)
    x2, h = _moe(hrow, logits, x1, tab0, tab1, norm1_g[1], w, 0, n_exp, n_ctx, "next")

    hf = h.reshape(b * l, d)
    xb = _matmul(hf, rg_w_x[0].astype(BF16), rg_b_x[0], F32).reshape(b, l, d)
    gy = _matmul(hf, rg_w_y[0].astype(BF16), rg_b_y[0], F32, act="gelu").reshape(b, l, d)
    u = _conv(xb, rg_conv_w[0], rg_conv_b[0], n_ctx)
    w_a, w_i = rg_w_a[0].astype(BF16), rg_w_i[0].astype(BF16)
    h_fwd = _scan(u, w_a, rg_b_a[0], w_i, rg_b_i[0], rg_lam[0], 0, n_ctx)
    hg = _scan(u, w_a, rg_b_a[0], w_i, rg_b_i[0], rg_lam[0], 1, n_ctx, h_fwd, gy)
    x1, hrow, logits = _proj(hg, rg_w_out[0].astype(BF16), rg_b_out[0], x2, tab1, norm2_g[1],
                             w["rw"][1], w["rb"][1], n_ctx, latent_only=True)
    _, out = _moe(hrow, logits, x1, tab1, tab1, final_g, w, 1, n_exp, 0, "final")
    return out
```

```python
import functools

import jax
import jax.numpy as jnp
import numpy as np
from jax import lax
from jax.experimental import pallas as pl
from jax.experimental.pallas import tpu as pltpu

F32 = jnp.float32
BF16 = jnp.bfloat16

LANES = 128
GRID_W = 64
TOP_K = 4
RG_C = 8.0
CONV_LEFT = 2
SWIGLU_ALPHA = 1.702
SWIGLU_LIMIT = 7.0
RMS_EPS = 1e-6
MOD_ROWS = 6
TAB_ROWS = 16
VMEM_LIMIT = 56 * 1024 * 1024


def _cparams(sem):
    return pltpu.CompilerParams(dimension_semantics=sem, vmem_limit_bytes=VMEM_LIMIT)


def _pick(n, pref, mult=8):
    for t in range(min(pref, n), 0, -1):
        if n % t == 0 and t % mult == 0:
            return t
    return n


def _dot(a, b):
    return jnp.dot(a, b, preferred_element_type=F32)


def _dot_nt(a, b):
    return lax.dot_general(a, b, (((1,), (1,)), ((), ())), preferred_element_type=F32)


def _split_bf16(x):
    hi = x.astype(BF16)
    lo = (x - hi.astype(F32)).astype(BF16)
    return hi, lo


def _dot3(a, w):
    a_hi, a_lo = _split_bf16(a)
    w_hi, w_lo = _split_bf16(w)
    return _dot(a_hi, w_hi) + _dot(a_lo, w_hi) + _dot(a_hi, w_lo)


def _norm_mod(x, g, shift, scale):
    y = x * lax.rsqrt(jnp.mean(x * x, axis=-1, keepdims=True) + RMS_EPS)
    return (y * g) * (1.0 + scale) + shift


def _to_token_rows(v):
    t, d = v.shape
    s_rows = d // LANES
    chunks = jnp.stack([v[:, s * LANES:(s + 1) * LANES] for s in range(s_rows)], axis=0)
    return pltpu.einshape("stl->tsl", chunks).reshape(t * s_rows, LANES)


def _from_token_rows(r, s_rows):
    return pltpu.einshape("tsl->stl", r.reshape(r.shape[0] // s_rows, s_rows, LANES))


def _tab_row(tab_ref, is_ctx, k):
    base = jnp.where(is_ctx, 0, MOD_ROWS)
    return tab_ref[0, pl.ds(base + k, 1), :]


def _mod_kernel(a_ref, w_ref, b_ref, o_ref):
    a = a_ref[...]
    a = a * jax.nn.sigmoid(a)
    o_ref[0] = _dot3(a, w_ref[0]) + b_ref[0]


def _modulation(cc, mod_w, mod_b):
    depth, d, n = mod_w.shape
    r = cc.shape[0]
    tn = _pick(n, 1024, LANES)
    return pl.pallas_call(
        _mod_kernel,
        grid=(depth, n // tn),
        in_specs=[
            pl.BlockSpec((r, d), lambda i, j: (0, 0)),
            pl.BlockSpec((1, d, tn), lambda i, j: (i, 0, j)),
            pl.BlockSpec((1, 1, tn), lambda i, j: (i, 0, j)),
        ],
        out_specs=pl.BlockSpec((1, r, tn), lambda i, j: (i, 0, j)),
        out_shape=jax.ShapeDtypeStruct((depth, r, n), F32),
        compiler_params=_cparams(("parallel", "parallel")),
        name="modulation",
    )(cc, mod_w, mod_b.reshape(depth, 1, n))


def _prenorm_kernel(ctx_ref, x_ref, tab_ref, g_ref, o_ref):
    is_ctx = pl.program_id(1) == 0
    xin = jnp.where(is_ctx, ctx_ref[0], x_ref[0])
    h = _norm_mod(xin, g_ref[...], _tab_row(tab_ref, is_ctx, 0), _tab_row(tab_ref, is_ctx, 1))
    o_ref[0] = h.astype(BF16)


def _prenorm(ctx, x, tab, g):
    b, n_ctx, d = ctx.shape
    tm = n_ctx
    nt = 1 + x.shape[1] // tm
    return pl.pallas_call(
        _prenorm_kernel,
        grid=(b, nt),
        in_specs=[
            pl.BlockSpec((1, tm, d), lambda i, j: (i, 0, 0)),
            pl.BlockSpec((1, tm, d), lambda i, j: (i, jnp.maximum(j - 1, 0), 0)),
            pl.BlockSpec((1, TAB_ROWS, d), lambda i, j: (i, 0, 0)),
            pl.BlockSpec((1, d), lambda i, j: (0, 0)),
        ],
        out_specs=pl.BlockSpec((1, tm, d), lambda i, j: (i, j, 0)),
        out_shape=jax.ShapeDtypeStruct((b, nt * tm, d), BF16),
        compiler_params=_cparams(("parallel", "parallel")),
        name="prenorm",
    )(ctx, x, tab, g.reshape(1, d))


def _gelu_tanh(x):
    return 0.5 * x * (1.0 + jnp.tanh(np.sqrt(2.0 / np.pi) * (x + 0.044715 * (x * x * x))))


def _matmul_kernel(a_ref, w_ref, b_ref, o_ref, *, act):
    y = _dot(a_ref[...], w_ref[...]) + b_ref[...]
    if act == "gelu":
        y = _gelu_tanh(y)
    o_ref[...] = y.astype(o_ref.dtype)


def _matmul(a, w, bias, out_dtype, act=None):
    m, k = a.shape
    n = w.shape[1]
    tm = _pick(m, 1024)
    tn = _pick(n, 512, LANES)
    return pl.pallas_call(
        functools.partial(_matmul_kernel, act=act),
        grid=(m // tm, n // tn),
        in_specs=[
            pl.BlockSpec((tm, k), lambda i, j: (i, 0)),
            pl.BlockSpec((k, tn), lambda i, j: (0, j)),
            pl.BlockSpec((1, tn), lambda i, j: (0, j)),
        ],
        out_specs=pl.BlockSpec((tm, tn), lambda i, j: (i, j)),
        out_shape=jax.ShapeDtypeStruct((m, n), out_dtype),
        compiler_params=_cparams(("parallel", "parallel")),
        name="matmul_" + (act or "linear"),
    )(a, w, bias.reshape(1, n))


QKV_ROW_TILE = 1152
QKV_COL_TILE = 768


def _matmul_cast_kernel(a_ref, w_ref, side_ref, o_ref, side_o_ref):
    o_ref[...] = _dot(a_ref[...], w_ref[...]).astype(o_ref.dtype)
    side_o_ref[...] = side_ref[...].astype(side_o_ref.dtype)


def _matmul_and_cast(a, w, side):
    m, k = a.shape
    n = w.shape[1]
    tm = _pick(m, QKV_ROW_TILE)
    tn = _pick(n, QKV_COL_TILE, LANES)
    nj = n // tn
    steps = (m // tm) * nj
    rows, cols = side.shape
    assert rows % (8 * steps) == 0
    slab = pl.BlockSpec((rows // steps, cols), lambda i, j: (i * nj + j, 0))
    return pl.pallas_call(
        _matmul_cast_kernel,
        grid=(m // tm, nj),
        in_specs=[
            pl.BlockSpec((tm, k), lambda i, j: (i, 0)),
            pl.BlockSpec((k, tn), lambda i, j: (0, j)),
            slab,
        ],
        out_specs=[pl.BlockSpec((tm, tn), lambda i, j: (i, j)), slab],
        out_shape=[jax.ShapeDtypeStruct((m, n), BF16), jax.ShapeDtypeStruct((rows, cols), BF16)],
        compiler_params=_cparams(("parallel", "parallel")),
        name="matmul_and_cast",
    )(a, w, side)


def _softmax_parts(parts):
    m = parts[0].max(axis=-1, keepdims=True)
    for s in parts[1:]:
        m = jnp.maximum(m, s.max(axis=-1, keepdims=True))
    ps = [jnp.exp(s - m) for s in parts]
    den = ps[0].sum(axis=-1, keepdims=True)
    for p in ps[1:]:
        den = den + p.sum(axis=-1, keepdims=True)
    return ps, den


ATTN_ROWS_PER_ITER = 16


def _attn_kernel(q_ref, k_ref, v_ref, bias_ref, wgu_ref, perm_ref, o_ref, wg_ref, wl_ref, *, n_ctx, rows, kr, scale):
    kc = k_ref[0, 0:n_ctx, :]
    vc = v_ref[0, 0:n_ctx, :]
    (p,), den = _softmax_parts([_dot_nt(q_ref[0, 0:n_ctx, :], kc) * scale])
    o_ref[0, 0:n_ctx, :] = (_dot(p.astype(BF16), vc) / den).astype(BF16)

    group = next(g for g in (ATTN_ROWS_PER_ITER, 2, 1) if rows % g == 0)
    n_iter = rows // group
    gu_rows = wgu_ref.shape[0]
    spread = gu_rows % (8 * n_iter) == 0

    def convert(i, n_parts):
        r_gu = gu_rows // n_parts
        gu = pl.ds(pl.multiple_of(i * r_gu, 8), r_gu)
        for j in range(wgu_ref.shape[1] // (2 * LANES)):
            y = _dot(wgu_ref[gu, 2 * LANES * j:2 * LANES * (j + 1)].astype(BF16), perm_ref[...])
            wg_ref[gu, LANES * j:LANES * (j + 1)] = y[:, :LANES].astype(BF16)
            wl_ref[gu, LANES * j:LANES * (j + 1)] = y[:, LANES:].astype(BF16)

    if not spread:
        convert(0, 1)

    def row_group(i, carry):
        if spread:
            convert(i, n_iter)
        rr = [i * group + j for j in range(group)]
        rs = [jnp.clip(r - kr // 2, 0, rows - kr) for r in rr]
        q0 = [pl.multiple_of(n_ctx + r * GRID_W, GRID_W) for r in rr]
        k0 = [pl.multiple_of(n_ctx + s * GRID_W, GRID_W) for s in rs]
        scores = []
        for j in range(group):
            q = q_ref[0, pl.ds(q0[j], GRID_W), :]
            s_lat = _dot_nt(q, k_ref[0, pl.ds(k0[j], kr * GRID_W), :]) * scale + bias_ref[0, rr[j] - rs[j]]
            scores.append([s_lat, _dot_nt(q, kc) * scale])
        probs = [_softmax_parts(s) for s in scores]
        for j in range(group):
            (p_lat, p_ctx), den = probs[j]
            o = _dot(p_lat.astype(BF16), v_ref[0, pl.ds(k0[j], kr * GRID_W), :]) + _dot(p_ctx.astype(BF16), vc)
            o_ref[0, pl.ds(q0[j], GRID_W), :] = (o / den).astype(BF16)
        return carry

    lax.fori_loop(0, n_iter, row_group, 0)


def _na_bias_table(rpb, rows, kr):
    h, n_dr, n_dc = rpb.shape
    win_rows, win_cols = (n_dr + 1) // 2, (n_dc + 1) // 2
    col = np.arange(GRID_W)
    col_start = np.clip(col - win_cols // 2, 0, GRID_W - win_cols)
    col_mask = (col[None, :] >= col_start[:, None]) & (col[None, :] < col_start[:, None] + win_cols)
    dc_idx = np.clip(col[None, :] - col[:, None], 1 - win_cols, win_cols - 1) + win_cols - 1
    dr_idx = np.arange(kr)[None, :] - np.arange(kr)[:, None] + win_rows - 1
    t = rpb[:, dr_idx][:, :, :, dc_idx]
    t = jnp.where(col_mask[None, None, None], t.astype(F32), -jnp.inf)
    return t.transpose(0, 1, 3, 2, 4).reshape(h, kr, GRID_W, kr * GRID_W)


def _attention(qkv, bias, n_ctx, heads, w_gu):
    b, l, d3 = qkv.shape
    d = d3 // 3
    dh = d // heads
    rows = (l - n_ctx) // GRID_W
    kr = bias.shape[1]
    n_steps = heads * b
    gu_rows, n2 = w_gu.shape
    assert gu_rows % (8 * n_steps) == 0
    tg = gu_rows // n_steps
    src = np.concatenate([np.arange(0, 2 * LANES, 2), np.arange(1, 2 * LANES, 2)])
    perm = jnp.asarray(np.arange(2 * LANES)[:, None] == src[None, :], BF16)
    kern = functools.partial(_attn_kernel, n_ctx=n_ctx, rows=rows, kr=kr, scale=dh ** -0.5)
    slab = lambda h, i: (h * b + i, 0)
    return pl.pallas_call(
        kern,
        grid=(heads, b),
        in_specs=[
            pl.BlockSpec((1, l, dh), lambda h, i: (i, 0, h)),
            pl.BlockSpec((1, l, dh), lambda h, i: (i, 0, heads + h)),
            pl.BlockSpec((1, l, dh), lambda h, i: (i, 0, 2 * heads + h)),
            pl.BlockSpec((1, kr, GRID_W, kr * GRID_W), lambda h, i: (h, 0, 0, 0)),
            pl.BlockSpec((tg, n2), slab),
            pl.BlockSpec((2 * LANES, 2 * LANES), lambda h, i: (0, 0)),
        ],
        out_specs=[
            pl.BlockSpec((1, l, dh), lambda h, i: (i, 0, h)),
            pl.BlockSpec((tg, n2 // 2), slab),
            pl.BlockSpec((tg, n2 // 2), slab),
        ],
        out_shape=[
            jax.ShapeDtypeStruct((b, l, d), BF16),
            jax.ShapeDtypeStruct((gu_rows, n2 // 2), BF16),
            jax.ShapeDtypeStruct((gu_rows, n2 // 2), BF16),
        ],
        compiler_params=_cparams(("parallel", "parallel")),
        name="na_attention",
    )(qkv, qkv, qkv, bias, w_gu, perm)


def _proj_kernel(*refs, ctx_tiles, tile_off, split_residual):
    if split_residual:
        a_ref, w_ref, b_ref, ctx_ref, x_ref, tab_ref, g_ref, rw_ref, rb_ref, x1_ref, hrow_ref, lg_ref = refs
    else:
        a_ref, w_ref, b_ref, x_ref, tab_ref, g_ref, rw_ref, rb_ref, x1_ref, hrow_ref, lg_ref = refs
    is_ctx = pl.program_id(1) + tile_off < ctx_tiles
    resid = jnp.where(is_ctx, ctx_ref[0], x_ref[0]) if split_residual else x_ref[0]
    y = _dot(a_ref[0], w_ref[...]) + b_ref[...]
    x1 = resid + _tab_row(tab_ref, is_ctx, 2) * y
    x1_ref[0] = x1
    h2 = _norm_mod(x1, g_ref[...], _tab_row(tab_ref, is_ctx, 3), _tab_row(tab_ref, is_ctx, 4))
    lg_ref[...] = _dot(h2.astype(BF16), rw_ref[...]) + rb_ref[...]
    hrow_ref[...] = _to_token_rows(h2)


def _proj(a, w, bias, resid, tab, g2, rw, rb, n_ctx, latent_only):
    split = isinstance(resid, tuple)
    b, l, d = a.shape
    tm = n_ctx
    off = n_ctx // tm if latent_only else 0
    nt = l // tm - off
    s_rows = d // LANES
    t = b * nt * tm
    kern = functools.partial(_proj_kernel, ctx_tiles=n_ctx // tm, tile_off=off, split_residual=split)
    if split:
        assert not latent_only and n_ctx == tm
        resid_specs = [pl.BlockSpec((1, tm, d), lambda i, j: (i, 0, 0)),
                       pl.BlockSpec((1, tm, d), lambda i, j: (i, jnp.maximum(j - 1, 0), 0))]
        resid_args = list(resid)
    else:
        resid_specs = [pl.BlockSpec((1, tm, d), lambda i, j: (i, j + off, 0))]
        resid_args = [resid]
    return pl.pallas_call(
        kern,
        grid=(b, nt),
        in_specs=[
            pl.BlockSpec((1, tm, d), lambda i, j: (i, j + off, 0)),
            pl.BlockSpec((d, d), lambda i, j: (0, 0)),
            pl.BlockSpec((1, d), lambda i, j: (0, 0)),
        ] + resid_specs + [
            pl.BlockSpec((1, TAB_ROWS, d), lambda i, j: (i, 0, 0)),
            pl.BlockSpec((1, d), lambda i, j: (0, 0)),
            pl.BlockSpec((d, LANES), lambda i, j: (0, 0)),
            pl.BlockSpec((1, LANES), lambda i, j: (0, 0)),
        ],
        out_specs=[
            pl.BlockSpec((1, tm, d), lambda i, j: (i, j, 0)),
            pl.BlockSpec((tm * s_rows, LANES), lambda i, j: (i * nt + j, 0)),
            pl.BlockSpec((tm, LANES), lambda i, j: (i * nt + j, 0)),
        ],
        out_shape=[
            jax.ShapeDtypeStruct((b, nt * tm, d), F32),
            jax.ShapeDtypeStruct((t * s_rows, LANES), F32),
            jax.ShapeDtypeStruct((t, LANES), F32),
        ],
        compiler_params=_cparams(("parallel", "parallel")),
        name="mixer_proj",
    )(a, w, bias.reshape(1, d), *resid_args, tab, g2.reshape(1, d), rw, rb)


def _route_kernel(lg_ref, e_ref, g_ref, r_ref, cnt_ref, carry_ref, *, n_exp):
    @pl.when(pl.program_id(0) == 0)
    def _():
        carry_ref[...] = jnp.zeros_like(carry_ref)

    lg = lg_ref[...]
    tm = lg.shape[0]
    lane = lax.broadcasted_iota(jnp.int32, lg.shape, 1).astype(F32)
    cur = jnp.where(lane < n_exp, lg, -jnp.inf)
    multi = jnp.zeros(lg.shape, F32)
    vals, idxs = [], []
    for _ in range(TOP_K):
        m = cur.max(axis=-1, keepdims=True)
        idx = jnp.where(cur == m, lane, float(LANES)).min(axis=-1, keepdims=True)
        sel = lane == idx
        multi = jnp.where(sel, 1.0, multi)
        cur = jnp.where(sel, -jnp.inf, cur)
        vals.append(m)
        idxs.append(idx)
    exps = [jnp.exp(v - vals[0]) for v in vals]
    den = exps[0]
    for e in exps[1:]:
        den = den + e
    tri = (lax.broadcasted_iota(jnp.int32, (tm, tm), 0) > lax.broadcasted_iota(jnp.int32, (tm, tm), 1))
    pref = _dot(jnp.where(tri, 1.0, 0.0).astype(BF16), multi.astype(BF16))
    tot = carry_ref[...] + pref
    e_out = jnp.zeros(lg.shape, F32)
    g_out = jnp.zeros(lg.shape, F32)
    r_out = jnp.zeros(lg.shape, F32)
    for k in range(TOP_K):
        rank_k = jnp.where(lane == idxs[k], tot, 0.0).sum(axis=-1, keepdims=True)
        e_out = jnp.where(lane == k, idxs[k], e_out)
        g_out = jnp.where(lane == k, exps[k] / den, g_out)
        r_out = jnp.where(lane == k, rank_k, r_out)
    e_ref[...] = e_out.astype(jnp.int32)
    g_ref[...] = g_out
    r_ref[...] = r_out.astype(jnp.int32)
    carry_ref[...] = carry_ref[...] + multi.sum(axis=0, keepdims=True)
    cnt_ref[...] = carry_ref[...]


def _route(logits, n_exp):
    t = logits.shape[0]
    tm = _pick(t, 256)
    spec = pl.BlockSpec((tm, LANES), lambda i: (i, 0))
    return pl.pallas_call(
        functools.partial(_route_kernel, n_exp=n_exp),
        grid=(t // tm,),
        in_specs=[spec],
        out_specs=[spec, spec, spec, pl.BlockSpec((1, LANES), lambda i: (0, 0))],
        out_shape=[
            jax.ShapeDtypeStruct((t, LANES), jnp.int32),
            jax.ShapeDtypeStruct((t, LANES), F32),
            jax.ShapeDtypeStruct((t, LANES), jnp.int32),
            jax.ShapeDtypeStruct((1, LANES), F32),
        ],
        scratch_shapes=[pltpu.VMEM((1, LANES), F32)],
        compiler_params=_cparams(("arbitrary",)),
        name="router",
    )(logits)


def _dispatch_kernel(dest_ref, h_ref, xs_ref, sem, *, td, s_rows):
    def issue(t, carry):
        src = h_ref.at[pl.ds(pl.multiple_of(t * s_rows, s_rows), s_rows), :]
        for k in range(TOP_K):
            d = dest_ref[0, 0, t * TOP_K + k]
            dst = xs_ref.at[pl.ds(pl.multiple_of(d * s_rows, s_rows), s_rows), :]
            pltpu.make_async_copy(src, dst, sem).start()
        return carry

    lax.fori_loop(0, td, issue, 0)
    for _ in range(TOP_K):
        pltpu.make_async_copy(h_ref, xs_ref.at[pl.ds(0, td * s_rows), :], sem).wait()


DISPATCH_TOKENS = 1024


def _dispatch(hrow, dest, s_rows):
    t = dest.shape[0] // TOP_K
    td = _pick(t, DISPATCH_TOKENS)
    nt = t // td
    return pl.pallas_call(
        functools.partial(_dispatch_kernel, td=td, s_rows=s_rows),
        grid=(nt,),
        in_specs=[
            pl.BlockSpec((1, 1, td * TOP_K), lambda i: (i, 0, 0), memory_space=pltpu.SMEM),
            pl.BlockSpec((td * s_rows, LANES), lambda i: (i, 0)),
        ],
        out_specs=pl.BlockSpec(memory_space=pl.ANY),
        out_shape=jax.ShapeDtypeStruct((t * TOP_K * s_rows, LANES), F32),
        scratch_shapes=[pltpu.SemaphoreType.DMA(())],
        compiler_params=pltpu.CompilerParams(dimension_semantics=("arbitrary",), vmem_limit_bytes=VMEM_LIMIT,
                                             has_side_effects=True),
        name="moe_dispatch",
    )(dest.reshape(nt, 1, td * TOP_K), hrow)


def _gmm_kernel(it_tile, it_e, it_lo, it_hi, it_first, it_valid,
                xs_ref, wg_ref, wl_ref, bg_ref, bl_ref, wd_ref, bd_ref, ys_ref, xb_ref, *, tg, s_rows, fk):
    m = pl.program_id(0)
    tsub = _pick(tg, 128)
    f = wd_ref.shape[1]

    @pl.when(it_valid[m] == 1)
    def _():
        for t0 in range(0, tg, tsub):
            xt = _from_token_rows(xs_ref[pl.ds(t0 * s_rows, tsub * s_rows), :], s_rows)
            for s in range(s_rows):
                xb_ref[pl.ds(t0, tsub), s * LANES:(s + 1) * LANES] = xt[s].astype(BF16)

        x = xb_ref[...]
        y = None
        for f0 in range(0, f, fk):
            cols = slice(f0, f0 + fk)
            glu = jnp.minimum(_dot(x, wg_ref[0, :, cols]) + bg_ref[0, :, cols], SWIGLU_LIMIT)
            lin = jnp.clip(_dot(x, wl_ref[0, :, cols]) + bl_ref[0, :, cols], -SWIGLU_LIMIT, SWIGLU_LIMIT)
            act = glu * jax.nn.sigmoid(SWIGLU_ALPHA * glu) * (lin + 1.0)
            part = _dot(act.astype(BF16), wd_ref[0, cols, :])
            y = part if y is None else y + part
        y = y + bd_ref[0]

        for t0 in range(0, tg, tsub):
            rows = pl.ds(t0 * s_rows, tsub * s_rows)
            yr = _to_token_rows(y[t0:t0 + tsub])
            row = t0 * s_rows + lax.broadcasted_iota(jnp.int32, (tsub * s_rows, 1), 0)
            mine = (row >= it_lo[m] * s_rows) & (row < it_hi[m] * s_rows)

            @pl.when(it_first[m] == 1)
            def _():
                ys_ref[rows, :] = jnp.where(mine, yr, 0.0)

            @pl.when(it_first[m] == 0)
            def _():
                ys_ref[rows, :] = jnp.where(mine, yr, ys_ref[rows, :])


def _gmm_items(counts, n_tiles, tg, max_items):
    n_exp = counts.shape[0]
    ends = jnp.cumsum(counts)
    starts = ends - counts
    def count_le(sorted_vals, q):
        return jnp.sum(sorted_vals[None, :] <= q[:, None], axis=1, dtype=jnp.int32)

    tile0 = jnp.arange(n_tiles, dtype=jnp.int32) * tg
    e_lo = jnp.minimum(count_le(ends, tile0), n_exp - 1)
    e_hi = jnp.minimum(count_le(ends, tile0 + tg - 1), n_exp - 1)
    n_items = e_hi - e_lo + 1
    item_end = jnp.cumsum(n_items)
    item_start = item_end - n_items
    total = item_end[-1]
    m = jnp.arange(max_items, dtype=jnp.int32)
    valid = m < total
    tile = jnp.minimum(count_le(item_end, m), n_tiles - 1)
    e = jnp.where(valid, e_lo[tile] + (m - item_start[tile]), e_hi[n_tiles - 1]).astype(jnp.int32)
    lo = jnp.clip(starts[e] - tile * tg, 0, tg).astype(jnp.int32)
    hi = jnp.clip(ends[e] - tile * tg, 0, tg).astype(jnp.int32)
    first = (m == item_start[tile]).astype(jnp.int32)
    return tile, e, lo, hi, first, valid.astype(jnp.int32)


GMM_ROW_TILE = 256
GMM_F_CHUNK = 1024


def _gmm(xs, counts, w, layer, s_rows):
    n_exp = counts.shape[0]
    d, f = w["wg"].shape[1:]
    p = xs.shape[0] // s_rows
    tg = _pick(p, GMM_ROW_TILE)
    fk = _pick(f, GMM_F_CHUNK, LANES)
    n_tiles = p // tg
    max_items = n_tiles + n_exp - 1
    tile, e, lo, hi, first, valid = _gmm_items(counts, n_tiles, tg, max_items)
    items = (tile, e + layer * n_exp, lo, hi, first, valid)

    def wspec(block, buffers=1):
        return pl.BlockSpec(block, lambda m, t, e, lo, hi, fi, va: (e[m], 0, 0), pipeline_mode=pl.Buffered(buffers))

    rows_spec = pl.BlockSpec((tg * s_rows, LANES), lambda m, t, e, lo, hi, fi, va: (t[m], 0))
    grid_spec = pltpu.PrefetchScalarGridSpec(
        num_scalar_prefetch=6,
        grid=(max_items,),
        in_specs=[
            rows_spec,
            wspec((1, d, f), 2),
            wspec((1, d, f), 2),
            wspec((1, 1, f), 2),
            wspec((1, 1, f), 2),
            wspec((1, f, d)),
            wspec((1, 1, d), 2),
        ],
        out_specs=rows_spec,
        scratch_shapes=[pltpu.VMEM((tg, d), BF16)],
    )
    return pl.pallas_call(
        functools.partial(_gmm_kernel, tg=tg, s_rows=s_rows, fk=fk),
        grid_spec=grid_spec,
        out_shape=jax.ShapeDtypeStruct(xs.shape, F32),
        compiler_params=_cparams(("arbitrary",)),
        name="moe_experts",
    )(*items, xs, w["wg"], w["wl"], w["bg"], w["bl"], w["wd"], w["bd"])


def _combine_kernel(dest_ref, dest_next_ref, ys_ref, gate_ref, x_ref, tab_ref, tabn_ref, g_ref,
                    x2_ref, h_ref, buf_ref, moe_ref, sem, *, tc, s_rows, ctx_tiles, n_steps, mode):
    step = pl.program_id(0) * pl.num_programs(1) + pl.program_id(1)
    slot = step % 2

    def gather(dref, slot_idx):
        def issue(t, carry):
            for k in range(TOP_K):
                d = dref[0, 0, t * TOP_K + k]
                src = ys_ref.at[pl.ds(pl.multiple_of(d * s_rows, s_rows), s_rows), :]
                dst = buf_ref.at[slot_idx, pl.ds(pl.multiple_of((k * tc + t) * s_rows, s_rows), s_rows), :]
                pltpu.make_async_copy(src, dst, sem.at[slot_idx]).start()
            return carry

        lax.fori_loop(0, tc, issue, 0)

    @pl.when(step == 0)
    def _():
        gather(dest_ref, 0)

    @pl.when(step + 1 < n_steps)
    def _():
        gather(dest_next_ref, 1 - slot)

    pltpu.make_async_copy(buf_ref.at[slot], buf_ref.at[slot], sem.at[slot]).wait()

    gates = gate_ref[...]
    for k in range(TOP_K):
        chunks = _from_token_rows(buf_ref[slot, pl.ds(k * tc * s_rows, tc * s_rows), :], s_rows)
        for s in range(s_rows):
            term = chunks[s] * gates[:, k:k + 1]
            if k == 0:
                moe_ref[:, s * LANES:(s + 1) * LANES] = term
            else:
                moe_ref[:, s * LANES:(s + 1) * LANES] += term

    is_ctx = pl.program_id(1) < ctx_tiles
    x2 = x_ref[0] + _tab_row(tab_ref, is_ctx, 5) * moe_ref[...]
    x2_ref[0] = x2
    if mode == "next":
        h = _norm_mod(x2, g_ref[...], _tab_row(tabn_ref, is_ctx, 0), _tab_row(tabn_ref, is_ctx, 1))
        h_ref[0] = h.astype(h_ref.dtype)
    else:
        y = x2 * lax.rsqrt(jnp.mean(x2 * x2, axis=-1, keepdims=True) + RMS_EPS)
        h_ref[0] = (y * g_ref[...]).astype(h_ref.dtype)


def _combine(ys, dest, gates, x1, tab, tab_next, g_next, n_ctx_rows, s_rows, mode):
    b, l, d = x1.shape
    tc = _pick(min(l, 128) if n_ctx_rows == 0 else n_ctx_rows, 128)
    nt = l // tc
    n_steps = b * nt
    dest3 = dest.reshape(n_steps, 1, tc * TOP_K)
    kern = functools.partial(_combine_kernel, tc=tc, s_rows=s_rows, ctx_tiles=n_ctx_rows // tc,
                             n_steps=n_steps, mode=mode)
    out_dtype = BF16 if mode == "next" else F32
    return pl.pallas_call(
        kern,
        grid=(b, nt),
        in_specs=[
            pl.BlockSpec((1, 1, tc * TOP_K), lambda i, j: (i * nt + j, 0, 0), memory_space=pltpu.SMEM),
            pl.BlockSpec((1, 1, tc * TOP_K), lambda i, j: (jnp.minimum(i * nt + j + 1, n_steps - 1), 0, 0),
                         memory_space=pltpu.SMEM),
            pl.BlockSpec(memory_space=pl.ANY),
            pl.BlockSpec((tc, LANES), lambda i, j: (i * nt + j, 0)),
            pl.BlockSpec((1, tc, d), lambda i, j: (i, j, 0)),
            pl.BlockSpec((1, TAB_ROWS, d), lambda i, j: (i, 0, 0)),
            pl.BlockSpec((1, TAB_ROWS, d), lambda i, j: (i, 0, 0)),
            pl.BlockSpec((1, d), lambda i, j: (0, 0)),
        ],
        out_specs=[
            pl.BlockSpec((1, tc, d), lambda i, j: (i, j, 0)),
            pl.BlockSpec((1, tc, d), lambda i, j: (i, j, 0)),
        ],
        out_shape=[
            jax.ShapeDtypeStruct((b, l, d), F32),
            jax.ShapeDtypeStruct((b, l, d), out_dtype),
        ],
        scratch_shapes=[
            pltpu.VMEM((2, tc * TOP_K * s_rows, LANES), F32),
            pltpu.VMEM((tc, d), F32),
            pltpu.SemaphoreType.DMA((2,)),
        ],
        compiler_params=_cparams(("arbitrary", "arbitrary")),
        name="moe_combine_" + mode,
    )(dest3, dest3, ys, gates, x1, tab, tab_next, g_next.reshape(1, d))


def _moe(hrow, logits, x1, tab, tab_next, g_next, w, layer, n_exp, n_ctx_rows, mode):
    s_rows = x1.shape[-1] // LANES
    e_pad, g_pad, r_pad, cnt = _route(logits, n_exp)
    top_e, rank = e_pad[:, :TOP_K], r_pad[:, :TOP_K]
    counts = cnt[0, :n_exp].astype(jnp.int32)
    starts = jnp.cumsum(counts) - counts
    first_slot = jnp.sum(jnp.where(top_e[..., None] == jnp.arange(n_exp, dtype=jnp.int32), starts, 0), axis=-1)
    dest = (first_slot + rank).reshape(-1)
    xs = _dispatch(hrow, dest, s_rows)
    ys = _gmm(xs, counts, w, layer, s_rows)
    return _combine(ys, dest, g_pad, x1, tab, tab_next, g_next, n_ctx_rows, s_rows, mode)


def _conv_kernel(x_ref, w_ref, b_ref, o_ref, *, n_ctx):
    l = x_ref.shape[1]
    w = w_ref[...]
    for s0, n in ((0, n_ctx), (n_ctx, l - n_ctx)):
        x = x_ref[0, s0:s0 + n, :]
        row = lax.broadcasted_iota(jnp.int32, (n, 1), 0)
        acc = x * w[CONV_LEFT:CONV_LEFT + 1]
        for j in range(w.shape[0]):
            off = j - CONV_LEFT
            if off == 0:
                continue
            shifted = pltpu.roll(x, (-off) % n, 0)
            ok = (row + off >= 0) & (row + off < n)
            acc = acc + jnp.where(ok, shifted, 0.0) * w[j:j + 1]
        o_ref[0, s0:s0 + n, :] = acc + b_ref[...]


def _conv(xb, conv_w, conv_b, n_ctx):
    b, l, d = xb.shape
    dt = _pick(d, 256, LANES)
    return pl.pallas_call(
        functools.partial(_conv_kernel, n_ctx=n_ctx),
        grid=(b, d // dt),
        in_specs=[
            pl.BlockSpec((1, l, dt), lambda i, j: (i, 0, j)),
            pl.BlockSpec((conv_w.shape[0], dt), lambda i, j: (0, j)),
            pl.BlockSpec((1, dt), lambda i, j: (0, j)),
        ],
        out_specs=pl.BlockSpec((1, l, dt), lambda i, j: (i, 0, j)),
        out_shape=jax.ShapeDtypeStruct((b, l, d), F32),
        compiler_params=_cparams(("parallel", "parallel")),
        name="rg_conv",
    )(xb, conv_w, conv_b.reshape(1, d))


SCAN_PAD = 8
SCAN_UNROLL = 8


def _scan_kernel(*refs, nb, tc, nh, reverse):
    n_in = 8 if reverse else 6
    u_ref, wa_ref, ba_ref, wi_ref, bi_ref, lam_ref = refs[:6]
    o_ref = refs[n_in]
    scratch = refs[n_in + 1:]
    a_s, x_s, h_s = scratch[0:nh], scratch[nh:2 * nh], scratch[2 * nh:3 * nh]
    carry = scratch[3 * nh]
    ts = tc + SCAN_PAD

    @pl.when(pl.program_id(1) == 0)
    def _():
        carry[...] = jnp.zeros_like(carry)

    neg = -lam_ref[0]
    softplus = jnp.maximum(neg, 0.0) + jnp.log1p(jnp.exp(-jnp.abs(neg)))
    for bi in range(nb):
        u = u_ref[bi]
        ub = u.astype(BF16)
        r = jax.nn.sigmoid(_dot(ub, wa_ref[0, 0]) + ba_ref[0, 0])
        i = jax.nn.sigmoid(_dot(ub, wi_ref[0, 0]) + bi_ref[0, 0])
        log_a = (-RG_C) * r * softplus
        a = jnp.exp(log_a)
        xin = jnp.sqrt(1.0 - a * a) * (i * u)
        for p in range(nh):
            a_s[p][pl.ds(bi * ts, tc), :] = a[:, p * LANES:(p + 1) * LANES]
            x_s[p][pl.ds(bi * ts, tc), :] = xin[:, p * LANES:(p + 1) * LANES]

    def block(j, hs):
        hs = list(hs)
        for q in range(SCAN_UNROLL):
            t = j * SCAN_UNROLL + q
            if reverse:
                t = tc - 1 - t
            for p in range(nh):
                hs[p] = a_s[p][pl.ds(t, nb, stride=ts), :] * hs[p] + x_s[p][pl.ds(t, nb, stride=ts), :]
                h_s[p][pl.ds(t, nb, stride=ts), :] = hs[p]
        return tuple(hs)

    h0 = tuple(carry[:, p * LANES:(p + 1) * LANES] for p in range(nh))
    hs = lax.fori_loop(0, tc // SCAN_UNROLL, block, h0)
    for p in range(nh):
        carry[:, p * LANES:(p + 1) * LANES] = hs[p]
    for bi in range(nb):
        for p in range(nh):
            h = h_s[p][pl.ds(bi * ts, tc), :]
            cols = slice(p * LANES, (p + 1) * LANES)
            if reverse:
                hf_ref, gy_ref = refs[6], refs[7]
                o_ref[bi, :, cols] = ((hf_ref[bi, :, cols] + h) * gy_ref[bi, :, cols]).astype(o_ref.dtype)
            else:
                o_ref[bi, :, cols] = h


def _scan(u, w_a, b_a, w_i, b_i, lam, dirn, n_ctx, h_fwd=None, gy=None):
    b, l, d = u.shape
    n_blk, w = w_a.shape[1], w_a.shape[2]
    tc = _pick(n_ctx, 128)
    nt, nc = l // tc, n_ctx // tc
    reverse = dirn == 1

    def chunk(j):
        if not reverse:
            return j
        return jnp.where(j < nc, nc - 1 - j, nt - 1 - (j - nc))

    blk = pl.BlockSpec((b, tc, w), lambda g, j: (0, chunk(j), g))
    in_specs = [
        blk,
        pl.BlockSpec((1, 1, w, w), lambda g, j: (dirn, g, 0, 0)),
        pl.BlockSpec((1, 1, 1, w), lambda g, j: (dirn, g, 0, 0)),
        pl.BlockSpec((1, 1, w, w), lambda g, j: (dirn, g, 0, 0)),
        pl.BlockSpec((1, 1, 1, w), lambda g, j: (dirn, g, 0, 0)),
        pl.BlockSpec((1, 1, w), lambda g, j: (dirn, 0, g)),
    ]
    args = [u, w_a, b_a.reshape(2, n_blk, 1, w), w_i, b_i.reshape(2, n_blk, 1, w), lam.reshape(2, 1, d)]
    if reverse:
        in_specs += [blk, blk]
        args += [h_fwd, gy]
    ts = tc + SCAN_PAD
    return pl.pallas_call(
        functools.partial(_scan_kernel, nb=b, tc=tc, nh=w // LANES, reverse=reverse),
        grid=(n_blk, nt),
        in_specs=in_specs,
        out_specs=blk,
        out_shape=jax.ShapeDtypeStruct((b, l, d), BF16 if reverse else F32),
        scratch_shapes=[pltpu.VMEM((b * ts, LANES), F32)] * (3 * (w // LANES)) + [pltpu.VMEM((b, w), F32)],
        compiler_params=_cparams(("parallel", "arbitrary")),
        name="rg_scan_" + ("rev" if reverse else "fwd"),
    )(*args)


def _mod_tables(mod_out, b, d):
    tabs = []
    for i in range(mod_out.shape[0]):
        ml = mod_out[i, :b].reshape(b, MOD_ROWS, d)
        mc = jnp.broadcast_to(mod_out[i, b].reshape(1, MOD_ROWS, d), (b, MOD_ROWS, d))
        pad = jnp.zeros((b, TAB_ROWS - 2 * MOD_ROWS, d), F32)
        tabs.append(jnp.concatenate([mc, ml, pad], axis=1))
    return tabs


def _moe_weights(router_w, router_b, wg, wl, b_gu, wd, b_dn):
    depth, d, n_exp = router_w.shape
    f = wg.shape[-1]
    return {
        "rw": jnp.pad(router_w, ((0, 0), (0, 0), (0, LANES - n_exp))).astype(BF16),
        "rb": jnp.pad(router_b, ((0, 0), (0, LANES - n_exp))).reshape(depth, 1, LANES),
        "wg": wg.reshape(depth * n_exp, d, f),
        "wl": wl.reshape(depth * n_exp, d, f),
        "bg": b_gu[..., 0::2].reshape(depth * n_exp, 1, f),
        "bl": b_gu[..., 1::2].reshape(depth * n_exp, 1, f),
        "wd": wd.reshape(depth * n_exp, f, d),
        "bd": b_dn.reshape(depth * n_exp, 1, d),
    }


def kernel(x, c, ctx, c_ctx, mod_w, mod_b, norm1_g, norm2_g, final_g, na_w_qkv, na_w_o, na_rpb, rg_w_y, rg_b_y,
           rg_w_x, rg_b_x, rg_conv_w, rg_conv_b, rg_w_a, rg_b_a, rg_w_i, rg_b_i, rg_lam, rg_w_out, rg_b_out,
           moe_router_w, moe_router_b, moe_w_gu, moe_b_gu, moe_w_dn, moe_b_dn):
    b, n_lat, d = x.shape
    n_ctx = ctx.shape[1]
    l = n_ctx + n_lat
    heads = na_rpb.shape[1]
    rows = n_lat // GRID_W
    kr = min((na_rpb.shape[2] + 1) // 2, rows)
    assert mod_w.shape[0] == 2 and n_lat % n_ctx == 0 and d % LANES == 0

    mod_rows = -(-(b + 1) // 8) * 8
    cc = jnp.concatenate([c, c_ctx[None, :], jnp.zeros((mod_rows - b - 1, d), F32)], axis=0)
    tab0, tab1 = _mod_tables(_modulation(cc, mod_w, mod_b), b, d)

    h = _prenorm(ctx, x, tab0, norm1_g[0])
    qkv, wd = _matmul_and_cast(h.reshape(b * l, d), na_w_qkv[0].astype(BF16), moe_w_dn.reshape(-1, moe_w_dn.shape[-1]))
    bias = _na_bias_table(na_rpb[0], rows, kr)
    o, wg, wl = _attention(qkv.reshape(b, l, 3 * d), bias, n_ctx, heads, moe_w_gu.reshape(-1, moe_w_gu.shape[-1]))
    n_exp = moe_router_w.shape[2]
    w = _moe_weights(moe_router_w, moe_router_b, wg, wl, moe_b_gu, wd, moe_b_dn)
    x1, hrow, logits = _proj(o, na_w_o[0].astype(BF16), jnp.zeros((d,), F32), (ctx, x), tab0, norm2_g[0],
                             w["rw"][0], w["rb"][0], n_ctx, latent_only=False)
    x2, h = _moe(hrow, logits, x1, tab0, tab1, norm1_g[1], w, 0, n_exp, n_ctx, "next")

    hf = h.reshape(b * l, d)
    xb = _matmul(hf, rg_w_x[0].astype(BF16), rg_b_x[0], F32).reshape(b, l, d)
    gy = _matmul(hf, rg_w_y[0].astype(BF16), rg_b_y[0], F32, act="gelu").reshape(b, l, d)
    u = _conv(xb, rg_conv_w[0], rg_conv_b[0], n_ctx)
    w_a, w_i = rg_w_a[0].astype(BF16), rg_w_i[0].astype(BF16)
    h_fwd = _scan(u, w_a, rg_b_a[0], w_i, rg_b_i[0], rg_lam[0], 0, n_ctx)
    hg = _scan(u, w_a, rg_b_a[0], w_i, rg_b_i[0], rg_lam[0], 1, n_ctx, h_fwd, gy)
    x1, hrow, logits = _proj(hg, rg_w_out[0].astype(BF16), rg_b_out[0], x2, tab1, norm2_g[1],
                             w["rw"][1], w["rb"][1], n_ctx, latent_only=True)
    _, out = _moe(hrow, logits, x1, tab1, tab1, final_g, w, 1, n_exp, 0, "final")
    return out
```

```python
import functools

import jax
import jax.numpy as jnp
import numpy as np
from jax import lax
from jax.experimental import pallas as pl
from jax.experimental.pallas import tpu as pltpu

F32 = jnp.float32
BF16 = jnp.bfloat16

LANES = 128
GRID_W = 64
TOP_K = 4
RG_C = 8.0
CONV_LEFT = 2
SWIGLU_ALPHA = 1.702
SWIGLU_LIMIT = 7.0
RMS_EPS = 1e-6
MOD_ROWS = 6
TAB_ROWS = 16
VMEM_LIMIT = 56 * 1024 * 1024


def _cparams(sem):
    return pltpu.CompilerParams(dimension_semantics=sem, vmem_limit_bytes=VMEM_LIMIT)


def _pick(n, pref, mult=8):
    for t in range(min(pref, n), 0, -1):
        if n % t == 0 and t % mult == 0:
            return t
    return n


def _dot(a, b):
    return jnp.dot(a, b, preferred_element_type=F32)


def _dot_nt(a, b):
    return lax.dot_general(a, b, (((1,), (1,)), ((), ())), preferred_element_type=F32)


def _split_bf16(x):
    hi = x.astype(BF16)
    lo = (x - hi.astype(F32)).astype(BF16)
    return hi, lo


def _dot3(a, w):
    a_hi, a_lo = _split_bf16(a)
    w_hi, w_lo = _split_bf16(w)
    return _dot(a_hi, w_hi) + _dot(a_lo, w_hi) + _dot(a_hi, w_lo)


def _norm_mod(x, g, shift, scale):
    y = x * lax.rsqrt(jnp.mean(x * x, axis=-1, keepdims=True) + RMS_EPS)
    return (y * g) * (1.0 + scale) + shift


def _to_token_rows(v):
    t, d = v.shape
    s_rows = d // LANES
    chunks = jnp.stack([v[:, s * LANES:(s + 1) * LANES] for s in range(s_rows)], axis=0)
    return pltpu.einshape("stl->tsl", chunks).reshape(t * s_rows, LANES)


def _from_token_rows(r, s_rows):
    return pltpu.einshape("tsl->stl", r.reshape(r.shape[0] // s_rows, s_rows, LANES))


def _tab_row(tab_ref, is_ctx, k):
    base = jnp.where(is_ctx, 0, MOD_ROWS)
    return tab_ref[0, pl.ds(base + k, 1), :]


def _mod_kernel(a_ref, w_ref, b_ref, o_ref):
    a = a_ref[...]
    a = a * jax.nn.sigmoid(a)
    o_ref[0] = _dot3(a, w_ref[0]) + b_ref[0]


def _modulation(cc, mod_w, mod_b):
    depth, d, n = mod_w.shape
    r = cc.shape[0]
    tn = _pick(n, 1024, LANES)
    return pl.pallas_call(
        _mod_kernel,
        grid=(depth, n // tn),
        in_specs=[
            pl.BlockSpec((r, d), lambda i, j: (0, 0)),
            pl.BlockSpec((1, d, tn), lambda i, j: (i, 0, j)),
            pl.BlockSpec((1, 1, tn), lambda i, j: (i, 0, j)),
        ],
        out_specs=pl.BlockSpec((1, r, tn), lambda i, j: (i, 0, j)),
        out_shape=jax.ShapeDtypeStruct((depth, r, n), F32),
        compiler_params=_cparams(("parallel", "parallel")),
        name="modulation",
    )(cc, mod_w, mod_b.reshape(depth, 1, n))


def _prenorm_kernel(ctx_ref, x_ref, tab_ref, g_ref, o_ref):
    is_ctx = pl.program_id(1) == 0
    xin = jnp.where(is_ctx, ctx_ref[0], x_ref[0])
    h = _norm_mod(xin, g_ref[...], _tab_row(tab_ref, is_ctx, 0), _tab_row(tab_ref, is_ctx, 1))
    o_ref[0] = h.astype(BF16)


def _prenorm(ctx, x, tab, g):
    b, n_ctx, d = ctx.shape
    tm = n_ctx
    nt = 1 + x.shape[1] // tm
    return pl.pallas_call(
        _prenorm_kernel,
        grid=(b, nt),
        in_specs=[
            pl.BlockSpec((1, tm, d), lambda i, j: (i, 0, 0)),
            pl.BlockSpec((1, tm, d), lambda i, j: (i, jnp.maximum(j - 1, 0), 0)),
            pl.BlockSpec((1, TAB_ROWS, d), lambda i, j: (i, 0, 0)),
            pl.BlockSpec((1, d), lambda i, j: (0, 0)),
        ],
        out_specs=pl.BlockSpec((1, tm, d), lambda i, j: (i, j, 0)),
        out_shape=jax.ShapeDtypeStruct((b, nt * tm, d), BF16),
        compiler_params=_cparams(("parallel", "parallel")),
        name="prenorm",
    )(ctx, x, tab, g.reshape(1, d))


def _gelu_tanh(x):
    return 0.5 * x * (1.0 + jnp.tanh(np.sqrt(2.0 / np.pi) * (x + 0.044715 * (x * x * x))))


def _matmul_kernel(a_ref, w_ref, b_ref, o_ref, *, act):
    y = _dot(a_ref[...], w_ref[...]) + b_ref[...]
    if act == "gelu":
        y = _gelu_tanh(y)
    o_ref[...] = y.astype(o_ref.dtype)


def _matmul(a, w, bias, out_dtype, act=None):
    m, k = a.shape
    n = w.shape[1]
    tm = _pick(m, 1024)
    tn = _pick(n, 512, LANES)
    return pl.pallas_call(
        functools.partial(_matmul_kernel, act=act),
        grid=(m // tm, n // tn),
        in_specs=[
            pl.BlockSpec((tm, k), lambda i, j: (i, 0)),
            pl.BlockSpec((k, tn), lambda i, j: (0, j)),
            pl.BlockSpec((1, tn), lambda i, j: (0, j)),
        ],
        out_specs=pl.BlockSpec((tm, tn), lambda i, j: (i, j)),
        out_shape=jax.ShapeDtypeStruct((m, n), out_dtype),
        compiler_params=_cparams(("parallel", "parallel")),
        name="matmul_" + (act or "linear"),
    )(a, w, bias.reshape(1, n))


QKV_ROW_TILE = 1152
QKV_COL_TILE = 768


def _matmul_cast_kernel(a_ref, w_ref, side_ref, o_ref, side_o_ref):
    o_ref[...] = _dot(a_ref[...], w_ref[...]).astype(o_ref.dtype)
    side_o_ref[...] = side_ref[...].astype(side_o_ref.dtype)


def _matmul_and_cast(a, w, side):
    m, k = a.shape
    n = w.shape[1]
    tm = _pick(m, QKV_ROW_TILE)
    tn = _pick(n, QKV_COL_TILE, LANES)
    nj = n // tn
    steps = (m // tm) * nj
    rows, cols = side.shape
    assert rows % (8 * steps) == 0
    slab = pl.BlockSpec((rows // steps, cols), lambda i, j: (i * nj + j, 0))
    return pl.pallas_call(
        _matmul_cast_kernel,
        grid=(m // tm, nj),
        in_specs=[
            pl.BlockSpec((tm, k), lambda i, j: (i, 0)),
            pl.BlockSpec((k, tn), lambda i, j: (0, j)),
            slab,
        ],
        out_specs=[pl.BlockSpec((tm, tn), lambda i, j: (i, j)), slab],
        out_shape=[jax.ShapeDtypeStruct((m, n), BF16), jax.ShapeDtypeStruct((rows, cols), BF16)],
        compiler_params=_cparams(("parallel", "parallel")),
        name="matmul_and_cast",
    )(a, w, side)


def _split_gu_slab(wgu_ref, perm_ref, wg_ref, wl_ref, rows):
    for j in range(wgu_ref.shape[1] // (2 * LANES)):
        y = _dot(wgu_ref[rows, 2 * LANES * j:2 * LANES * (j + 1)].astype(BF16), perm_ref[...])
        wg_ref[rows, LANES * j:LANES * (j + 1)] = y[:, :LANES].astype(BF16)
        wl_ref[rows, LANES * j:LANES * (j + 1)] = y[:, LANES:].astype(BF16)


def _split_perm():
    src = np.concatenate([np.arange(0, 2 * LANES, 2), np.arange(1, 2 * LANES, 2)])
    return jnp.asarray(np.arange(2 * LANES)[:, None] == src[None, :], BF16)


def _matmul_split_kernel(a_ref, w_ref, b_ref, wgu_ref, perm_ref, o_ref, wg_ref, wl_ref):
    o_ref[...] = (_dot(a_ref[...], w_ref[...]) + b_ref[...]).astype(o_ref.dtype)
    _split_gu_slab(wgu_ref, perm_ref, wg_ref, wl_ref, slice(None))


def _matmul_and_split(a, w, bias, w_gu, row0, n_rows):
    m, k = a.shape
    n = w.shape[1]
    tm = _pick(m, QKV_ROW_TILE)
    tn = _pick(n, 512, LANES)
    nj = n // tn
    steps = (m // tm) * nj
    n2 = w_gu.shape[1]
    assert n_rows % (8 * steps) == 0 and row0 % (n_rows // steps) == 0
    ts = n_rows // steps
    blk0 = row0 // ts
    out_slab = pl.BlockSpec((ts, n2 // 2), lambda i, j: (i * nj + j, 0))
    return pl.pallas_call(
        _matmul_split_kernel,
        grid=(m // tm, nj),
        in_specs=[
            pl.BlockSpec((tm, k), lambda i, j: (i, 0)),
            pl.BlockSpec((k, tn), lambda i, j: (0, j)),
            pl.BlockSpec((1, tn), lambda i, j: (0, j)),
            pl.BlockSpec((ts, n2), lambda i, j: (blk0 + i * nj + j, 0)),
            pl.BlockSpec((2 * LANES, 2 * LANES), lambda i, j: (0, 0)),
        ],
        out_specs=[pl.BlockSpec((tm, tn), lambda i, j: (i, j)), out_slab, out_slab],
        out_shape=[jax.ShapeDtypeStruct((m, n), F32)] + [jax.ShapeDtypeStruct((n_rows, n2 // 2), BF16)] * 2,
        compiler_params=_cparams(("parallel", "parallel")),
        name="matmul_and_split",
    )(a, w, bias.reshape(1, n), w_gu, _split_perm())


def _softmax_parts(parts):
    m = parts[0].max(axis=-1, keepdims=True)
    for s in parts[1:]:
        m = jnp.maximum(m, s.max(axis=-1, keepdims=True))
    ps = [jnp.exp(s - m) for s in parts]
    den = ps[0].sum(axis=-1, keepdims=True)
    for p in ps[1:]:
        den = den + p.sum(axis=-1, keepdims=True)
    return ps, den


ATTN_ROWS_PER_ITER = 16


def _attn_kernel(q_ref, k_ref, v_ref, bias_ref, wgu_ref, perm_ref, o_ref, wg_ref, wl_ref, *, n_ctx, rows, kr, scale):
    kc = k_ref[0, 0:n_ctx, :]
    vc = v_ref[0, 0:n_ctx, :]
    (p,), den = _softmax_parts([_dot_nt(q_ref[0, 0:n_ctx, :], kc) * scale])
    o_ref[0, 0:n_ctx, :] = (_dot(p.astype(BF16), vc) / den).astype(BF16)

    group = next(g for g in (ATTN_ROWS_PER_ITER, 2, 1) if rows % g == 0)
    n_iter = rows // group
    gu_rows = wgu_ref.shape[0]
    spread = gu_rows % (8 * n_iter) == 0

    def convert(i, n_parts):
        r_gu = gu_rows // n_parts
        _split_gu_slab(wgu_ref, perm_ref, wg_ref, wl_ref, pl.ds(pl.multiple_of(i * r_gu, 8), r_gu))

    if not spread:
        convert(0, 1)

    def row_group(i, carry):
        if spread:
            convert(i, n_iter)
        rr = [i * group + j for j in range(group)]
        rs = [jnp.clip(r - kr // 2, 0, rows - kr) for r in rr]
        q0 = [pl.multiple_of(n_ctx + r * GRID_W, GRID_W) for r in rr]
        k0 = [pl.multiple_of(n_ctx + s * GRID_W, GRID_W) for s in rs]
        scores = []
        for j in range(group):
            q = q_ref[0, pl.ds(q0[j], GRID_W), :]
            s_lat = _dot_nt(q, k_ref[0, pl.ds(k0[j], kr * GRID_W), :]) * scale + bias_ref[0, rr[j] - rs[j]]
            scores.append([s_lat, _dot_nt(q, kc) * scale])
        probs = [_softmax_parts(s) for s in scores]
        for j in range(group):
            (p_lat, p_ctx), den = probs[j]
            o = _dot(p_lat.astype(BF16), v_ref[0, pl.ds(k0[j], kr * GRID_W), :]) + _dot(p_ctx.astype(BF16), vc)
            o_ref[0, pl.ds(q0[j], GRID_W), :] = (o / den).astype(BF16)
        return carry

    lax.fori_loop(0, n_iter, row_group, 0)


def _na_bias_table(rpb, rows, kr):
    h, n_dr, n_dc = rpb.shape
    win_rows, win_cols = (n_dr + 1) // 2, (n_dc + 1) // 2
    col = np.arange(GRID_W)
    col_start = np.clip(col - win_cols // 2, 0, GRID_W - win_cols)
    col_mask = (col[None, :] >= col_start[:, None]) & (col[None, :] < col_start[:, None] + win_cols)
    dc_idx = np.clip(col[None, :] - col[:, None], 1 - win_cols, win_cols - 1) + win_cols - 1
    dr_idx = np.arange(kr)[None, :] - np.arange(kr)[:, None] + win_rows - 1
    t = rpb[:, dr_idx][:, :, :, dc_idx]
    t = jnp.where(col_mask[None, None, None], t.astype(F32), -jnp.inf)
    return t.transpose(0, 1, 3, 2, 4).reshape(h, kr, GRID_W, kr * GRID_W)


def _attention(qkv, bias, n_ctx, heads, w_gu, gu_rows):
    b, l, d3 = qkv.shape
    d = d3 // 3
    dh = d // heads
    rows = (l - n_ctx) // GRID_W
    kr = bias.shape[1]
    n_steps = heads * b
    n2 = w_gu.shape[1]
    assert gu_rows % (8 * n_steps) == 0
    tg = gu_rows // n_steps
    perm = _split_perm()
    kern = functools.partial(_attn_kernel, n_ctx=n_ctx, rows=rows, kr=kr, scale=dh ** -0.5)
    slab = lambda h, i: (h * b + i, 0)
    return pl.pallas_call(
        kern,
        grid=(heads, b),
        in_specs=[
            pl.BlockSpec((1, l, dh), lambda h, i: (i, 0, h)),
            pl.BlockSpec((1, l, dh), lambda h, i: (i, 0, heads + h)),
            pl.BlockSpec((1, l, dh), lambda h, i: (i, 0, 2 * heads + h)),
            pl.BlockSpec((1, kr, GRID_W, kr * GRID_W), lambda h, i: (h, 0, 0, 0)),
            pl.BlockSpec((tg, n2), slab),
            pl.BlockSpec((2 * LANES, 2 * LANES), lambda h, i: (0, 0)),
        ],
        out_specs=[
            pl.BlockSpec((1, l, dh), lambda h, i: (i, 0, h)),
            pl.BlockSpec((tg, n2 // 2), slab),
            pl.BlockSpec((tg, n2 // 2), slab),
        ],
        out_shape=[
            jax.ShapeDtypeStruct((b, l, d), BF16),
            jax.ShapeDtypeStruct((gu_rows, n2 // 2), BF16),
            jax.ShapeDtypeStruct((gu_rows, n2 // 2), BF16),
        ],
        compiler_params=_cparams(("parallel", "parallel")),
        name="na_attention",
    )(qkv, qkv, qkv, bias, w_gu, perm)


def _proj_kernel(*refs, ctx_tiles, tile_off, split_residual):
    if split_residual:
        a_ref, w_ref, b_ref, ctx_ref, x_ref, tab_ref, g_ref, rw_ref, rb_ref, x1_ref, hrow_ref, lg_ref = refs
    else:
        a_ref, w_ref, b_ref, x_ref, tab_ref, g_ref, rw_ref, rb_ref, x1_ref, hrow_ref, lg_ref = refs
    is_ctx = pl.program_id(1) + tile_off < ctx_tiles
    resid = jnp.where(is_ctx, ctx_ref[0], x_ref[0]) if split_residual else x_ref[0]
    y = _dot(a_ref[0], w_ref[...]) + b_ref[...]
    x1 = resid + _tab_row(tab_ref, is_ctx, 2) * y
    x1_ref[0] = x1
    h2 = _norm_mod(x1, g_ref[...], _tab_row(tab_ref, is_ctx, 3), _tab_row(tab_ref, is_ctx, 4))
    lg_ref[...] = _dot(h2.astype(BF16), rw_ref[...]) + rb_ref[...]
    hrow_ref[...] = _to_token_rows(h2)


def _proj(a, w, bias, resid, tab, g2, rw, rb, n_ctx, latent_only):
    split = isinstance(resid, tuple)
    b, l, d = a.shape
    tm = n_ctx
    off = n_ctx // tm if latent_only else 0
    nt = l // tm - off
    s_rows = d // LANES
    t = b * nt * tm
    kern = functools.partial(_proj_kernel, ctx_tiles=n_ctx // tm, tile_off=off, split_residual=split)
    if split:
        assert not latent_only and n_ctx == tm
        resid_specs = [pl.BlockSpec((1, tm, d), lambda i, j: (i, 0, 0)),
                       pl.BlockSpec((1, tm, d), lambda i, j: (i, jnp.maximum(j - 1, 0), 0))]
        resid_args = list(resid)
    else:
        resid_specs = [pl.BlockSpec((1, tm, d), lambda i, j: (i, j + off, 0))]
        resid_args = [resid]
    return pl.pallas_call(
        kern,
        grid=(b, nt),
        in_specs=[
            pl.BlockSpec((1, tm, d), lambda i, j: (i, j + off, 0)),
            pl.BlockSpec((d, d), lambda i, j: (0, 0)),
            pl.BlockSpec((1, d), lambda i, j: (0, 0)),
        ] + resid_specs + [
            pl.BlockSpec((1, TAB_ROWS, d), lambda i, j: (i, 0, 0)),
            pl.BlockSpec((1, d), lambda i, j: (0, 0)),
            pl.BlockSpec((d, LANES), lambda i, j: (0, 0)),
            pl.BlockSpec((1, LANES), lambda i, j: (0, 0)),
        ],
        out_specs=[
            pl.BlockSpec((1, tm, d), lambda i, j: (i, j, 0)),
            pl.BlockSpec((tm * s_rows, LANES), lambda i, j: (i * nt + j, 0)),
            pl.BlockSpec((tm, LANES), lambda i, j: (i * nt + j, 0)),
        ],
        out_shape=[
            jax.ShapeDtypeStruct((b, nt * tm, d), F32),
            jax.ShapeDtypeStruct((t * s_rows, LANES), F32),
            jax.ShapeDtypeStruct((t, LANES), F32),
        ],
        compiler_params=_cparams(("parallel", "parallel")),
        name="mixer_proj",
    )(a, w, bias.reshape(1, d), *resid_args, tab, g2.reshape(1, d), rw, rb)


def _route_kernel(lg_ref, e_ref, g_ref, r_ref, cnt_ref, carry_ref, *, n_exp):
    @pl.when(pl.program_id(0) == 0)
    def _():
        carry_ref[...] = jnp.zeros_like(carry_ref)

    lg = lg_ref[...]
    tm = lg.shape[0]
    lane = lax.broadcasted_iota(jnp.int32, lg.shape, 1).astype(F32)
    cur = jnp.where(lane < n_exp, lg, -jnp.inf)
    multi = jnp.zeros(lg.shape, F32)
    vals, idxs = [], []
    for _ in range(TOP_K):
        m = cur.max(axis=-1, keepdims=True)
        idx = jnp.where(cur == m, lane, float(LANES)).min(axis=-1, keepdims=True)
        sel = lane == idx
        multi = jnp.where(sel, 1.0, multi)
        cur = jnp.where(sel, -jnp.inf, cur)
        vals.append(m)
        idxs.append(idx)
    exps = [jnp.exp(v - vals[0]) for v in vals]
    den = exps[0]
    for e in exps[1:]:
        den = den + e
    tri = (lax.broadcasted_iota(jnp.int32, (tm, tm), 0) > lax.broadcasted_iota(jnp.int32, (tm, tm), 1))
    pref = _dot(jnp.where(tri, 1.0, 0.0).astype(BF16), multi.astype(BF16))
    tot = carry_ref[...] + pref
    e_out = jnp.zeros(lg.shape, F32)
    g_out = jnp.zeros(lg.shape, F32)
    r_out = jnp.zeros(lg.shape, F32)
    for k in range(TOP_K):
        rank_k = jnp.where(lane == idxs[k], tot, 0.0).sum(axis=-1, keepdims=True)
        e_out = jnp.where(lane == k, idxs[k], e_out)
        g_out = jnp.where(lane == k, exps[k] / den, g_out)
        r_out = jnp.where(lane == k, rank_k, r_out)
    e_ref[...] = e_out.astype(jnp.int32)
    g_ref[...] = g_out
    r_ref[...] = r_out.astype(jnp.int32)
    carry_ref[...] = carry_ref[...] + multi.sum(axis=0, keepdims=True)
    cnt_ref[...] = carry_ref[...]


def _route(logits, n_exp):
    t = logits.shape[0]
    tm = _pick(t, 256)
    spec = pl.BlockSpec((tm, LANES), lambda i: (i, 0))
    return pl.pallas_call(
        functools.partial(_route_kernel, n_exp=n_exp),
        grid=(t // tm,),
        in_specs=[spec],
        out_specs=[spec, spec, spec, pl.BlockSpec((1, LANES), lambda i: (0, 0))],
        out_shape=[
            jax.ShapeDtypeStruct((t, LANES), jnp.int32),
            jax.ShapeDtypeStruct((t, LANES), F32),
            jax.ShapeDtypeStruct((t, LANES), jnp.int32),
            jax.ShapeDtypeStruct((1, LANES), F32),
        ],
        scratch_shapes=[pltpu.VMEM((1, LANES), F32)],
        compiler_params=_cparams(("arbitrary",)),
        name="router",
    )(logits)


def _dispatch_kernel(dest_ref, h_ref, xs_ref, sem, *, td, s_rows):
    def issue(t, carry):
        src = h_ref.at[pl.ds(pl.multiple_of(t * s_rows, s_rows), s_rows), :]
        for k in range(TOP_K):
            d = dest_ref[0, 0, t * TOP_K + k]
            dst = xs_ref.at[pl.ds(pl.multiple_of(d * s_rows, s_rows), s_rows), :]
            pltpu.make_async_copy(src, dst, sem).start()
        return carry

    lax.fori_loop(0, td, issue, 0)
    for _ in range(TOP_K):
        pltpu.make_async_copy(h_ref, xs_ref.at[pl.ds(0, td * s_rows), :], sem).wait()


DISPATCH_TOKENS = 1024


def _dispatch(hrow, dest, s_rows):
    t = dest.shape[0] // TOP_K
    td = _pick(t, DISPATCH_TOKENS)
    nt = t // td
    return pl.pallas_call(
        functools.partial(_dispatch_kernel, td=td, s_rows=s_rows),
        grid=(nt,),
        in_specs=[
            pl.BlockSpec((1, 1, td * TOP_K), lambda i: (i, 0, 0), memory_space=pltpu.SMEM),
            pl.BlockSpec((td * s_rows, LANES), lambda i: (i, 0)),
        ],
        out_specs=pl.BlockSpec(memory_space=pl.ANY),
        out_shape=jax.ShapeDtypeStruct((t * TOP_K * s_rows, LANES), F32),
        scratch_shapes=[pltpu.SemaphoreType.DMA(())],
        compiler_params=pltpu.CompilerParams(dimension_semantics=("arbitrary",), vmem_limit_bytes=VMEM_LIMIT,
                                             has_side_effects=True),
        name="moe_dispatch",
    )(dest.reshape(nt, 1, td * TOP_K), hrow)


def _gmm_kernel(it_tile, it_e, it_lo, it_hi, it_first, it_valid,
                xs_ref, wg_ref, wl_ref, bg_ref, bl_ref, wd_ref, bd_ref, ys_ref, xb_ref, *, tg, s_rows, fk):
    m = pl.program_id(0)
    tsub = _pick(tg, 128)
    f = wd_ref.shape[1]

    @pl.when(it_valid[m] == 1)
    def _():
        for t0 in range(0, tg, tsub):
            xt = _from_token_rows(xs_ref[pl.ds(t0 * s_rows, tsub * s_rows), :], s_rows)
            for s in range(s_rows):
                xb_ref[pl.ds(t0, tsub), s * LANES:(s + 1) * LANES] = xt[s].astype(BF16)

        x = xb_ref[...]
        y = None
        for f0 in range(0, f, fk):
            cols = slice(f0, f0 + fk)
            glu = jnp.minimum(_dot(x, wg_ref[0, :, cols]) + bg_ref[0, :, cols], SWIGLU_LIMIT)
            lin = jnp.clip(_dot(x, wl_ref[0, :, cols]) + bl_ref[0, :, cols], -SWIGLU_LIMIT, SWIGLU_LIMIT)
            act = glu * jax.nn.sigmoid(SWIGLU_ALPHA * glu) * (lin + 1.0)
            part = _dot(act.astype(BF16), wd_ref[0, cols, :])
            y = part if y is None else y + part
        y = y + bd_ref[0]

        for t0 in range(0, tg, tsub):
            rows = pl.ds(t0 * s_rows, tsub * s_rows)
            yr = _to_token_rows(y[t0:t0 + tsub])
            row = t0 * s_rows + lax.broadcasted_iota(jnp.int32, (tsub * s_rows, 1), 0)
            mine = (row >= it_lo[m] * s_rows) & (row < it_hi[m] * s_rows)

            @pl.when(it_first[m] == 1)
            def _():
                ys_ref[rows, :] = jnp.where(mine, yr, 0.0)

            @pl.when(it_first[m] == 0)
            def _():
                ys_ref[rows, :] = jnp.where(mine, yr, ys_ref[rows, :])


def _gmm_items(counts, n_tiles, tg, max_items):
    n_exp = counts.shape[0]
    ends = jnp.cumsum(counts)
    starts = ends - counts
    def count_le(sorted_vals, q):
        return jnp.sum(sorted_vals[None, :] <= q[:, None], axis=1, dtype=jnp.int32)

    tile0 = jnp.arange(n_tiles, dtype=jnp.int32) * tg
    e_lo = jnp.minimum(count_le(ends, tile0), n_exp - 1)
    e_hi = jnp.minimum(count_le(ends, tile0 + tg - 1), n_exp - 1)
    n_items = e_hi - e_lo + 1
    item_end = jnp.cumsum(n_items)
    item_start = item_end - n_items
    total = item_end[-1]
    m = jnp.arange(max_items, dtype=jnp.int32)
    valid = m < total
    tile = jnp.minimum(count_le(item_end, m), n_tiles - 1)
    e = jnp.where(valid, e_lo[tile] + (m - item_start[tile]), e_hi[n_tiles - 1]).astype(jnp.int32)
    lo = jnp.clip(starts[e] - tile * tg, 0, tg).astype(jnp.int32)
    hi = jnp.clip(ends[e] - tile * tg, 0, tg).astype(jnp.int32)
    first = (m == item_start[tile]).astype(jnp.int32)
    return tile, e, lo, hi, first, valid.astype(jnp.int32)


GMM_ROW_TILE = 256
GMM_F_CHUNK = 1024


def _gmm(xs, counts, w, layer, s_rows):
    n_exp = counts.shape[0]
    d, f = w["wg"].shape[1:]
    p = xs.shape[0] // s_rows
    tg = _pick(p, GMM_ROW_TILE)
    fk = _pick(f, GMM_F_CHUNK, LANES)
    n_tiles = p // tg
    max_items = n_tiles + n_exp - 1
    items = _gmm_items(counts, n_tiles, tg, max_items)
    all_layers = layer * n_exp

    def wspec(block, buffers=1, first=all_layers):
        return pl.BlockSpec(block, lambda m, t, e, lo, hi, fi, va: (e[m] + first, 0, 0),
                            pipeline_mode=pl.Buffered(buffers))

    rows_spec = pl.BlockSpec((tg * s_rows, LANES), lambda m, t, e, lo, hi, fi, va: (t[m], 0))
    grid_spec = pltpu.PrefetchScalarGridSpec(
        num_scalar_prefetch=6,
        grid=(max_items,),
        in_specs=[
            rows_spec,
            wspec((1, d, f), 2, 0),
            wspec((1, d, f), 2, 0),
            wspec((1, 1, f), 2),
            wspec((1, 1, f), 2),
            wspec((1, f, d)),
            wspec((1, 1, d), 2),
        ],
        out_specs=rows_spec,
        scratch_shapes=[pltpu.VMEM((tg, d), BF16)],
    )
    return pl.pallas_call(
        functools.partial(_gmm_kernel, tg=tg, s_rows=s_rows, fk=fk),
        grid_spec=grid_spec,
        out_shape=jax.ShapeDtypeStruct(xs.shape, F32),
        compiler_params=_cparams(("arbitrary",)),
        name="moe_experts",
    )(*items, xs, w["wg"], w["wl"], w["bg"], w["bl"], w["wd"], w["bd"])


def _combine_kernel(dest_ref, dest_next_ref, ys_ref, gate_ref, x_ref, tab_ref, tabn_ref, g_ref,
                    x2_ref, h_ref, buf_ref, moe_ref, sem, *, tc, s_rows, ctx_tiles, n_steps, mode):
    step = pl.program_id(0) * pl.num_programs(1) + pl.program_id(1)
    slot = step % 2

    def gather(dref, slot_idx):
        def issue(t, carry):
            for k in range(TOP_K):
                d = dref[0, 0, t * TOP_K + k]
                src = ys_ref.at[pl.ds(pl.multiple_of(d * s_rows, s_rows), s_rows), :]
                dst = buf_ref.at[slot_idx, pl.ds(pl.multiple_of((k * tc + t) * s_rows, s_rows), s_rows), :]
                pltpu.make_async_copy(src, dst, sem.at[slot_idx]).start()
            return carry

        lax.fori_loop(0, tc, issue, 0)

    @pl.when(step == 0)
    def _():
        gather(dest_ref, 0)

    @pl.when(step + 1 < n_steps)
    def _():
        gather(dest_next_ref, 1 - slot)

    pltpu.make_async_copy(buf_ref.at[slot], buf_ref.at[slot], sem.at[slot]).wait()

    gates = gate_ref[...]
    for k in range(TOP_K):
        chunks = _from_token_rows(buf_ref[slot, pl.ds(k * tc * s_rows, tc * s_rows), :], s_rows)
        for s in range(s_rows):
            term = chunks[s] * gates[:, k:k + 1]
            if k == 0:
                moe_ref[:, s * LANES:(s + 1) * LANES] = term
            else:
                moe_ref[:, s * LANES:(s + 1) * LANES] += term

    is_ctx = pl.program_id(1) < ctx_tiles
    x2 = x_ref[0] + _tab_row(tab_ref, is_ctx, 5) * moe_ref[...]
    x2_ref[0] = x2
    if mode == "next":
        h = _norm_mod(x2, g_ref[...], _tab_row(tabn_ref, is_ctx, 0), _tab_row(tabn_ref, is_ctx, 1))
        h_ref[0] = h.astype(h_ref.dtype)
    else:
        y = x2 * lax.rsqrt(jnp.mean(x2 * x2, axis=-1, keepdims=True) + RMS_EPS)
        h_ref[0] = (y * g_ref[...]).astype(h_ref.dtype)


def _combine(ys, dest, gates, x1, tab, tab_next, g_next, n_ctx_rows, s_rows, mode):
    b, l, d = x1.shape
    tc = _pick(min(l, 128) if n_ctx_rows == 0 else n_ctx_rows, 128)
    nt = l // tc
    n_steps = b * nt
    dest3 = dest.reshape(n_steps, 1, tc * TOP_K)
    kern = functools.partial(_combine_kernel, tc=tc, s_rows=s_rows, ctx_tiles=n_ctx_rows // tc,
                             n_steps=n_steps, mode=mode)
    out_dtype = BF16 if mode == "next" else F32
    return pl.pallas_call(
        kern,
        grid=(b, nt),
        in_specs=[
            pl.BlockSpec((1, 1, tc * TOP_K), lambda i, j: (i * nt + j, 0, 0), memory_space=pltpu.SMEM),
            pl.BlockSpec((1, 1, tc * TOP_K), lambda i, j: (jnp.minimum(i * nt + j + 1, n_steps - 1), 0, 0),
                         memory_space=pltpu.SMEM),
            pl.BlockSpec(memory_space=pl.ANY),
            pl.BlockSpec((tc, LANES), lambda i, j: (i * nt + j, 0)),
            pl.BlockSpec((1, tc, d), lambda i, j: (i, j, 0)),
            pl.BlockSpec((1, TAB_ROWS, d), lambda i, j: (i, 0, 0)),
            pl.BlockSpec((1, TAB_ROWS, d), lambda i, j: (i, 0, 0)),
            pl.BlockSpec((1, d), lambda i, j: (0, 0)),
        ],
        out_specs=[
            pl.BlockSpec((1, tc, d), lambda i, j: (i, j, 0)),
            pl.BlockSpec((1, tc, d), lambda i, j: (i, j, 0)),
        ],
        out_shape=[
            jax.ShapeDtypeStruct((b, l, d), F32),
            jax.ShapeDtypeStruct((b, l, d), out_dtype),
        ],
        scratch_shapes=[
            pltpu.VMEM((2, tc * TOP_K * s_rows, LANES), F32),
            pltpu.VMEM((tc, d), F32),
            pltpu.SemaphoreType.DMA((2,)),
        ],
        compiler_params=_cparams(("arbitrary", "arbitrary")),
        name="moe_combine_" + mode,
    )(dest3, dest3, ys, gates, x1, tab, tab_next, g_next.reshape(1, d))


def _moe(hrow, logits, x1, tab, tab_next, g_next, w, layer, n_exp, n_ctx_rows, mode):
    s_rows = x1.shape[-1] // LANES
    e_pad, g_pad, r_pad, cnt = _route(logits, n_exp)
    top_e, rank = e_pad[:, :TOP_K], r_pad[:, :TOP_K]
    counts = cnt[0, :n_exp].astype(jnp.int32)
    starts = jnp.cumsum(counts) - counts
    first_slot = jnp.sum(jnp.where(top_e[..., None] == jnp.arange(n_exp, dtype=jnp.int32), starts, 0), axis=-1)
    dest = (first_slot + rank).reshape(-1)
    xs = _dispatch(hrow, dest, s_rows)
    ys = _gmm(xs, counts, w, layer, s_rows)
    return _combine(ys, dest, g_pad, x1, tab, tab_next, g_next, n_ctx_rows, s_rows, mode)


def _conv_kernel(x_ref, w_ref, b_ref, o_ref, *, n_ctx):
    l = x_ref.shape[1]
    w = w_ref[...]
    for s0, n in ((0, n_ctx), (n_ctx, l - n_ctx)):
        x = x_ref[0, s0:s0 + n, :]
        row = lax.broadcasted_iota(jnp.int32, (n, 1), 0)
        acc = x * w[CONV_LEFT:CONV_LEFT + 1]
        for j in range(w.shape[0]):
            off = j - CONV_LEFT
            if off == 0:
                continue
            shifted = pltpu.roll(x, (-off) % n, 0)
            ok = (row + off >= 0) & (row + off < n)
            acc = acc + jnp.where(ok, shifted, 0.0) * w[j:j + 1]
        o_ref[0, s0:s0 + n, :] = acc + b_ref[...]


def _conv(xb, conv_w, conv_b, n_ctx):
    b, l, d = xb.shape
    dt = _pick(d, 256, LANES)
    return pl.pallas_call(
        functools.partial(_conv_kernel, n_ctx=n_ctx),
        grid=(b, d // dt),
        in_specs=[
            pl.BlockSpec((1, l, dt), lambda i, j: (i, 0, j)),
            pl.BlockSpec((conv_w.shape[0], dt), lambda i, j: (0, j)),
            pl.BlockSpec((1, dt), lambda i, j: (0, j)),
        ],
        out_specs=pl.BlockSpec((1, l, dt), lambda i, j: (i, 0, j)),
        out_shape=jax.ShapeDtypeStruct((b, l, d), F32),
        compiler_params=_cparams(("parallel", "parallel")),
        name="rg_conv",
    )(xb, conv_w, conv_b.reshape(1, d))


SCAN_PAD = 8
SCAN_UNROLL = 8


def _scan_kernel(*refs, nb, tc, nh, reverse):
    n_in = 8 if reverse else 6
    u_ref, wa_ref, ba_ref, wi_ref, bi_ref, lam_ref = refs[:6]
    o_ref = refs[n_in]
    scratch = refs[n_in + 1:]
    a_s, x_s, h_s = scratch[0:nh], scratch[nh:2 * nh], scratch[2 * nh:3 * nh]
    carry = scratch[3 * nh]
    ts = tc + SCAN_PAD

    @pl.when(pl.program_id(1) == 0)
    def _():
        carry[...] = jnp.zeros_like(carry)

    neg = -lam_ref[0]
    softplus = jnp.maximum(neg, 0.0) + jnp.log1p(jnp.exp(-jnp.abs(neg)))
    for bi in range(nb):
        u = u_ref[bi]
        ub = u.astype(BF16)
        r = jax.nn.sigmoid(_dot(ub, wa_ref[0, 0]) + ba_ref[0, 0])
        i = jax.nn.sigmoid(_dot(ub, wi_ref[0, 0]) + bi_ref[0, 0])
        log_a = (-RG_C) * r * softplus
        a = jnp.exp(log_a)
        xin = jnp.sqrt(1.0 - a * a) * (i * u)
        for p in range(nh):
            a_s[p][pl.ds(bi * ts, tc), :] = a[:, p * LANES:(p + 1) * LANES]
            x_s[p][pl.ds(bi * ts, tc), :] = xin[:, p * LANES:(p + 1) * LANES]

    def block(j, hs):
        hs = list(hs)
        for q in range(SCAN_UNROLL):
            t = j * SCAN_UNROLL + q
            if reverse:
                t = tc - 1 - t
            for p in range(nh):
                hs[p] = a_s[p][pl.ds(t, nb, stride=ts), :] * hs[p] + x_s[p][pl.ds(t, nb, stride=ts), :]
                h_s[p][pl.ds(t, nb, stride=ts), :] = hs[p]
        return tuple(hs)

    h0 = tuple(carry[:, p * LANES:(p + 1) * LANES] for p in range(nh))
    hs = lax.fori_loop(0, tc // SCAN_UNROLL, block, h0)
    for p in range(nh):
        carry[:, p * LANES:(p + 1) * LANES] = hs[p]
    for bi in range(nb):
        for p in range(nh):
            h = h_s[p][pl.ds(bi * ts, tc), :]
            cols = slice(p * LANES, (p + 1) * LANES)
            if reverse:
                hf_ref, gy_ref = refs[6], refs[7]
                o_ref[bi, :, cols] = ((hf_ref[bi, :, cols] + h) * gy_ref[bi, :, cols]).astype(o_ref.dtype)
            else:
                o_ref[bi, :, cols] = h


def _scan(u, w_a, b_a, w_i, b_i, lam, dirn, n_ctx, h_fwd=None, gy=None):
    b, l, d = u.shape
    n_blk, w = w_a.shape[1], w_a.shape[2]
    tc = _pick(n_ctx, 128)
    nt, nc = l // tc, n_ctx // tc
    reverse = dirn == 1

    def chunk(j):
        if not reverse:
            return j
        return jnp.where(j < nc, nc - 1 - j, nt - 1 - (j - nc))

    blk = pl.BlockSpec((b, tc, w), lambda g, j: (0, chunk(j), g))
    in_specs = [
        blk,
        pl.BlockSpec((1, 1, w, w), lambda g, j: (dirn, g, 0, 0)),
        pl.BlockSpec((1, 1, 1, w), lambda g, j: (dirn, g, 0, 0)),
        pl.BlockSpec((1, 1, w, w), lambda g, j: (dirn, g, 0, 0)),
        pl.BlockSpec((1, 1, 1, w), lambda g, j: (dirn, g, 0, 0)),
        pl.BlockSpec((1, 1, w), lambda g, j: (dirn, 0, g)),
    ]
    args = [u, w_a, b_a.reshape(2, n_blk, 1, w), w_i, b_i.reshape(2, n_blk, 1, w), lam.reshape(2, 1, d)]
    if reverse:
        in_specs += [blk, blk]
        args += [h_fwd, gy]
    ts = tc + SCAN_PAD
    return pl.pallas_call(
        functools.partial(_scan_kernel, nb=b, tc=tc, nh=w // LANES, reverse=reverse),
        grid=(n_blk, nt),
        in_specs=in_specs,
        out_specs=blk,
        out_shape=jax.ShapeDtypeStruct((b, l, d), BF16 if reverse else F32),
        scratch_shapes=[pltpu.VMEM((b * ts, LANES), F32)] * (3 * (w // LANES)) + [pltpu.VMEM((b, w), F32)],
        compiler_params=_cparams(("parallel", "arbitrary")),
        name="rg_scan_" + ("rev" if reverse else "fwd"),
    )(*args)


def _mod_tables(mod_out, b, d):
    tabs = []
    for i in range(mod_out.shape[0]):
        ml = mod_out[i, :b].reshape(b, MOD_ROWS, d)
        mc = jnp.broadcast_to(mod_out[i, b].reshape(1, MOD_ROWS, d), (b, MOD_ROWS, d))
        pad = jnp.zeros((b, TAB_ROWS - 2 * MOD_ROWS, d), F32)
        tabs.append(jnp.concatenate([mc, ml, pad], axis=1))
    return tabs


def _moe_weights(router_w, router_b, b_gu, wd, b_dn):
    depth, d, n_exp = router_w.shape
    f = wd.shape[-2]
    return {
        "rw": jnp.pad(router_w, ((0, 0), (0, 0), (0, LANES - n_exp))).astype(BF16),
        "rb": jnp.pad(router_b, ((0, 0), (0, LANES - n_exp))).reshape(depth, 1, LANES),
        "bg": b_gu[..., 0::2].reshape(depth * n_exp, 1, f),
        "bl": b_gu[..., 1::2].reshape(depth * n_exp, 1, f),
        "wd": wd.reshape(depth * n_exp, f, d),
        "bd": b_dn.reshape(depth * n_exp, 1, d),
    }


def kernel(x, c, ctx, c_ctx, mod_w, mod_b, norm1_g, norm2_g, final_g, na_w_qkv, na_w_o, na_rpb, rg_w_y, rg_b_y,
           rg_w_x, rg_b_x, rg_conv_w, rg_conv_b, rg_w_a, rg_b_a, rg_w_i, rg_b_i, rg_lam, rg_w_out, rg_b_out,
           moe_router_w, moe_router_b, moe_w_gu, moe_b_gu, moe_w_dn, moe_b_dn):
    b, n_lat, d = x.shape
    n_ctx = ctx.shape[1]
    l = n_ctx + n_lat
    heads = na_rpb.shape[1]
    rows = n_lat // GRID_W
    kr = min((na_rpb.shape[2] + 1) // 2, rows)
    assert mod_w.shape[0] == 2 and n_lat % n_ctx == 0 and d % LANES == 0

    mod_rows = -(-(b + 1) // 8) * 8
    cc = jnp.concatenate([c, c_ctx[None, :], jnp.zeros((mod_rows - b - 1, d), F32)], axis=0)
    tab0, tab1 = _mod_tables(_modulation(cc, mod_w, mod_b), b, d)

    h = _prenorm(ctx, x, tab0, norm1_g[0])
    qkv, wd = _matmul_and_cast(h.reshape(b * l, d), na_w_qkv[0].astype(BF16), moe_w_dn.reshape(-1, moe_w_dn.shape[-1]))
    bias = _na_bias_table(na_rpb[0], rows, kr)
    n_exp = moe_router_w.shape[2]
    f = moe_w_dn.shape[2]
    w_gu = moe_w_gu.reshape(-1, 2 * f)
    layer_rows = n_exp * d
    o, wg, wl = _attention(qkv.reshape(b, l, 3 * d), bias, n_ctx, heads, w_gu, layer_rows)
    w_all = _moe_weights(moe_router_w, moe_router_b, moe_b_gu, wd.reshape(-1, f, d), moe_b_dn)
    w = dict(w_all, wg=wg.reshape(n_exp, d, f), wl=wl.reshape(n_exp, d, f))
    x1, hrow, logits = _proj(o, na_w_o[0].astype(BF16), jnp.zeros((d,), F32), (ctx, x), tab0, norm2_g[0],
                             w["rw"][0], w["rb"][0], n_ctx, latent_only=False)
    x2, h = _moe(hrow, logits, x1, tab0, tab1, norm1_g[1], w, 0, n_exp, n_ctx, "next")

    hf = h.reshape(b * l, d)
    xb, wg, wl = _matmul_and_split(hf, rg_w_x[0].astype(BF16), rg_b_x[0], w_gu, layer_rows, layer_rows)
    xb = xb.reshape(b, l, d)
    w = dict(w_all, wg=wg.reshape(n_exp, d, f), wl=wl.reshape(n_exp, d, f))
    gy = _matmul(hf, rg_w_y[0].astype(BF16), rg_b_y[0], F32, act="gelu").reshape(b, l, d)
    u = _conv(xb, rg_conv_w[0], rg_conv_b[0], n_ctx)
    w_a, w_i = rg_w_a[0].astype(BF16), rg_w_i[0].astype(BF16)
    h_fwd = _scan(u, w_a, rg_b_a[0], w_i, rg_b_i[0], rg_lam[0], 0, n_ctx)
    hg = _scan(u, w_a, rg_b_a[0], w_i, rg_b_i[0], rg_lam[0], 1, n_ctx, h_fwd, gy)
    x1, hrow, logits = _proj(hg, rg_w_out[0].astype(BF16), rg_b_out[0], x2, tab1, norm2_g[1],
                             w["rw"][1], w["rb"][1], n_ctx, latent_only=True)
    _, out = _moe(hrow, logits, x1, tab1, tab1, final_g, w, 1, n_exp, 0, "final")
    return out
```

```python
import functools

import jax
import jax.numpy as jnp
import numpy as np
from jax import lax
from jax.experimental import pallas as pl
from jax.experimental.pallas import tpu as pltpu

F32 = jnp.float32
BF16 = jnp.bfloat16

LANES = 128
GRID_W = 64
TOP_K = 4
RG_C = 8.0
CONV_LEFT = 2
SWIGLU_ALPHA = 1.702
SWIGLU_LIMIT = 7.0
RMS_EPS = 1e-6
MOD_ROWS = 6
TAB_ROWS = 16
VMEM_LIMIT = 56 * 1024 * 1024


def _cparams(sem):
    return pltpu.CompilerParams(dimension_semantics=sem, vmem_limit_bytes=VMEM_LIMIT)


def _pick(n, pref, mult=8):
    for t in range(min(pref, n), 0, -1):
        if n % t == 0 and t % mult == 0:
            return t
    return n


def _dot(a, b):
    return jnp.dot(a, b, preferred_element_type=F32)


def _dot_nt(a, b):
    return lax.dot_general(a, b, (((1,), (1,)), ((), ())), preferred_element_type=F32)


def _split_bf16(x):
    hi = x.astype(BF16)
    lo = (x - hi.astype(F32)).astype(BF16)
    return hi, lo


def _dot3(a, w):
    a_hi, a_lo = _split_bf16(a)
    w_hi, w_lo = _split_bf16(w)
    return _dot(a_hi, w_hi) + _dot(a_lo, w_hi) + _dot(a_hi, w_lo)


def _norm_mod(x, g, shift, scale):
    y = x * lax.rsqrt(jnp.mean(x * x, axis=-1, keepdims=True) + RMS_EPS)
    return (y * g) * (1.0 + scale) + shift


def _to_token_rows(v):
    t, d = v.shape
    s_rows = d // LANES
    chunks = jnp.stack([v[:, s * LANES:(s + 1) * LANES] for s in range(s_rows)], axis=0)
    return pltpu.einshape("stl->tsl", chunks).reshape(t * s_rows, LANES)


def _from_token_rows(r, s_rows):
    return pltpu.einshape("tsl->stl", r.reshape(r.shape[0] // s_rows, s_rows, LANES))


def _tab_row(tab_ref, is_ctx, k):
    base = jnp.where(is_ctx, 0, MOD_ROWS)
    return tab_ref[0, pl.ds(base + k, 1), :]


def _mod_kernel(a_ref, w_ref, b_ref, o_ref):
    a = a_ref[...]
    a = a * jax.nn.sigmoid(a)
    o_ref[0] = _dot3(a, w_ref[0]) + b_ref[0]


def _modulation(cc, mod_w, mod_b):
    depth, d, n = mod_w.shape
    r = cc.shape[0]
    tn = _pick(n, 1024, LANES)
    return pl.pallas_call(
        _mod_kernel,
        grid=(depth, n // tn),
        in_specs=[
            pl.BlockSpec((r, d), lambda i, j: (0, 0)),
            pl.BlockSpec((1, d, tn), lambda i, j: (i, 0, j)),
            pl.BlockSpec((1, 1, tn), lambda i, j: (i, 0, j)),
        ],
        out_specs=pl.BlockSpec((1, r, tn), lambda i, j: (i, 0, j)),
        out_shape=jax.ShapeDtypeStruct((depth, r, n), F32),
        compiler_params=_cparams(("parallel", "parallel")),
        name="modulation",
    )(cc, mod_w, mod_b.reshape(depth, 1, n))


def _prenorm_kernel(ctx_ref, x_ref, tab_ref, g_ref, o_ref):
    is_ctx = pl.program_id(1) == 0
    xin = jnp.where(is_ctx, ctx_ref[0], x_ref[0])
    h = _norm_mod(xin, g_ref[...], _tab_row(tab_ref, is_ctx, 0), _tab_row(tab_ref, is_ctx, 1))
    o_ref[0] = h.astype(BF16)


def _prenorm(ctx, x, tab, g):
    b, n_ctx, d = ctx.shape
    tm = n_ctx
    nt = 1 + x.shape[1] // tm
    return pl.pallas_call(
        _prenorm_kernel,
        grid=(b, nt),
        in_specs=[
            pl.BlockSpec((1, tm, d), lambda i, j: (i, 0, 0)),
            pl.BlockSpec((1, tm, d), lambda i, j: (i, jnp.maximum(j - 1, 0), 0)),
            pl.BlockSpec((1, TAB_ROWS, d), lambda i, j: (i, 0, 0)),
            pl.BlockSpec((1, d), lambda i, j: (0, 0)),
        ],
        out_specs=pl.BlockSpec((1, tm, d), lambda i, j: (i, j, 0)),
        out_shape=jax.ShapeDtypeStruct((b, nt * tm, d), BF16),
        compiler_params=_cparams(("parallel", "parallel")),
        name="prenorm",
    )(ctx, x, tab, g.reshape(1, d))


def _gelu_tanh(x):
    return 0.5 * x * (1.0 + jnp.tanh(np.sqrt(2.0 / np.pi) * (x + 0.044715 * (x * x * x))))


def _matmul_kernel(a_ref, w_ref, b_ref, o_ref, *, act):
    y = _dot(a_ref[...], w_ref[...]) + b_ref[...]
    if act == "gelu":
        y = _gelu_tanh(y)
    o_ref[...] = y.astype(o_ref.dtype)


def _matmul(a, w, bias, out_dtype, act=None):
    m, k = a.shape
    n = w.shape[1]
    tm = _pick(m, 1024)
    tn = _pick(n, 512, LANES)
    return pl.pallas_call(
        functools.partial(_matmul_kernel, act=act),
        grid=(m // tm, n // tn),
        in_specs=[
            pl.BlockSpec((tm, k), lambda i, j: (i, 0)),
            pl.BlockSpec((k, tn), lambda i, j: (0, j)),
            pl.BlockSpec((1, tn), lambda i, j: (0, j)),
        ],
        out_specs=pl.BlockSpec((tm, tn), lambda i, j: (i, j)),
        out_shape=jax.ShapeDtypeStruct((m, n), out_dtype),
        compiler_params=_cparams(("parallel", "parallel")),
        name="matmul_" + (act or "linear"),
    )(a, w, bias.reshape(1, n))


QKV_ROW_TILE = 1152
QKV_COL_TILE = 768


def _matmul_cast_kernel(a_ref, w_ref, side_ref, o_ref, side_o_ref):
    o_ref[...] = _dot(a_ref[...], w_ref[...]).astype(o_ref.dtype)
    side_o_ref[...] = side_ref[...].astype(side_o_ref.dtype)


def _matmul_and_cast(a, w, side):
    m, k = a.shape
    n = w.shape[1]
    tm = _pick(m, QKV_ROW_TILE)
    tn = _pick(n, QKV_COL_TILE, LANES)
    nj = n // tn
    steps = (m // tm) * nj
    rows, cols = side.shape
    assert rows % (8 * steps) == 0
    slab = pl.BlockSpec((rows // steps, cols), lambda i, j: (i * nj + j, 0))
    return pl.pallas_call(
        _matmul_cast_kernel,
        grid=(m // tm, nj),
        in_specs=[
            pl.BlockSpec((tm, k), lambda i, j: (i, 0)),
            pl.BlockSpec((k, tn), lambda i, j: (0, j)),
            slab,
        ],
        out_specs=[pl.BlockSpec((tm, tn), lambda i, j: (i, j)), slab],
        out_shape=[jax.ShapeDtypeStruct((m, n), BF16), jax.ShapeDtypeStruct((rows, cols), BF16)],
        compiler_params=_cparams(("parallel", "parallel")),
        name="matmul_and_cast",
    )(a, w, side)


def _softmax_parts(parts):
    m = parts[0].max(axis=-1, keepdims=True)
    for s in parts[1:]:
        m = jnp.maximum(m, s.max(axis=-1, keepdims=True))
    ps = [jnp.exp(s - m) for s in parts]
    den = ps[0].sum(axis=-1, keepdims=True)
    for p in ps[1:]:
        den = den + p.sum(axis=-1, keepdims=True)
    return ps, den


ATTN_ROWS_PER_ITER = 16


def _attn_kernel(q_ref, k_ref, v_ref, bias_ref, wgu_ref, perm_ref, o_ref, wg_ref, wl_ref, *, n_ctx, rows, kr, scale):
    kc = k_ref[0, 0:n_ctx, :]
    vc = v_ref[0, 0:n_ctx, :]
    (p,), den = _softmax_parts([_dot_nt(q_ref[0, 0:n_ctx, :], kc) * scale])
    o_ref[0, 0:n_ctx, :] = (_dot(p.astype(BF16), vc) / den).astype(BF16)

    group = next(g for g in (ATTN_ROWS_PER_ITER, 2, 1) if rows % g == 0)
    n_iter = rows // group
    gu_rows = wgu_ref.shape[0]
    spread = gu_rows % (8 * n_iter) == 0

    def convert(i, n_parts):
        r_gu = gu_rows // n_parts
        gu = pl.ds(pl.multiple_of(i * r_gu, 8), r_gu)
        for j in range(wgu_ref.shape[1] // (2 * LANES)):
            y = _dot(wgu_ref[gu, 2 * LANES * j:2 * LANES * (j + 1)].astype(BF16), perm_ref[...])
            wg_ref[gu, LANES * j:LANES * (j + 1)] = y[:, :LANES].astype(BF16)
            wl_ref[gu, LANES * j:LANES * (j + 1)] = y[:, LANES:].astype(BF16)

    if not spread:
        convert(0, 1)

    def row_group(i, carry):
        if spread:
            convert(i, n_iter)
        rr = [i * group + j for j in range(group)]
        rs = [jnp.clip(r - kr // 2, 0, rows - kr) for r in rr]
        q0 = [pl.multiple_of(n_ctx + r * GRID_W, GRID_W) for r in rr]
        k0 = [pl.multiple_of(n_ctx + s * GRID_W, GRID_W) for s in rs]
        scores = []
        for j in range(group):
            q = q_ref[0, pl.ds(q0[j], GRID_W), :]
            s_lat = _dot_nt(q, k_ref[0, pl.ds(k0[j], kr * GRID_W), :]) * scale + bias_ref[0, rr[j] - rs[j]]
            scores.append([s_lat, _dot_nt(q, kc) * scale])
        probs = [_softmax_parts(s) for s in scores]
        for j in range(group):
            (p_lat, p_ctx), den = probs[j]
            o = _dot(p_lat.astype(BF16), v_ref[0, pl.ds(k0[j], kr * GRID_W), :]) + _dot(p_ctx.astype(BF16), vc)
            o_ref[0, pl.ds(q0[j], GRID_W), :] = (o / den).astype(BF16)
        return carry

    lax.fori_loop(0, n_iter, row_group, 0)


def _na_bias_table(rpb, rows, kr):
    h, n_dr, n_dc = rpb.shape
    win_rows, win_cols = (n_dr + 1) // 2, (n_dc + 1) // 2
    col = np.arange(GRID_W)
    col_start = np.clip(col - win_cols // 2, 0, GRID_W - win_cols)
    col_mask = (col[None, :] >= col_start[:, None]) & (col[None, :] < col_start[:, None] + win_cols)
    dc_idx = np.clip(col[None, :] - col[:, None], 1 - win_cols, win_cols - 1) + win_cols - 1
    dr_idx = np.arange(kr)[None, :] - np.arange(kr)[:, None] + win_rows - 1
    t = rpb[:, dr_idx][:, :, :, dc_idx]
    t = jnp.where(col_mask[None, None, None], t.astype(F32), -jnp.inf)
    return t.transpose(0, 1, 3, 2, 4).reshape(h, kr, GRID_W, kr * GRID_W)


def _attention(qkv, bias, n_ctx, heads, w_gu):
    b, l, d3 = qkv.shape
    d = d3 // 3
    dh = d // heads
    rows = (l - n_ctx) // GRID_W
    kr = bias.shape[1]
    n_steps = heads * b
    gu_rows, n2 = w_gu.shape
    assert gu_rows % (8 * n_steps) == 0
    tg = gu_rows // n_steps
    src = np.concatenate([np.arange(0, 2 * LANES, 2), np.arange(1, 2 * LANES, 2)])
    perm = jnp.asarray(np.arange(2 * LANES)[:, None] == src[None, :], BF16)
    kern = functools.partial(_attn_kernel, n_ctx=n_ctx, rows=rows, kr=kr, scale=dh ** -0.5)
    slab = lambda h, i: (h * b + i, 0)
    return pl.pallas_call(
        kern,
        grid=(heads, b),
        in_specs=[
            pl.BlockSpec((1, l, dh), lambda h, i: (i, 0, h)),
            pl.BlockSpec((1, l, dh), lambda h, i: (i, 0, heads + h)),
            pl.BlockSpec((1, l, dh), lambda h, i: (i, 0, 2 * heads + h)),
            pl.BlockSpec((1, kr, GRID_W, kr * GRID_W), lambda h, i: (h, 0, 0, 0)),
            pl.BlockSpec((tg, n2), slab),
            pl.BlockSpec((2 * LANES, 2 * LANES), lambda h, i: (0, 0)),
        ],
        out_specs=[
            pl.BlockSpec((1, l, dh), lambda h, i: (i, 0, h)),
            pl.BlockSpec((tg, n2 // 2), slab),
            pl.BlockSpec((tg, n2 // 2), slab),
        ],
        out_shape=[
            jax.ShapeDtypeStruct((b, l, d), BF16),
            jax.ShapeDtypeStruct((gu_rows, n2 // 2), BF16),
            jax.ShapeDtypeStruct((gu_rows, n2 // 2), BF16),
        ],
        compiler_params=_cparams(("parallel", "parallel")),
        name="na_attention",
    )(qkv, qkv, qkv, bias, w_gu, perm)


def _proj_kernel(*refs, ctx_tiles, tile_off, split_residual):
    if split_residual:
        a_ref, w_ref, b_ref, ctx_ref, x_ref, tab_ref, g_ref, rw_ref, rb_ref, x1_ref, hrow_ref, lg_ref = refs
    else:
        a_ref, w_ref, b_ref, x_ref, tab_ref, g_ref, rw_ref, rb_ref, x1_ref, hrow_ref, lg_ref = refs
    is_ctx = pl.program_id(1) + tile_off < ctx_tiles
    resid = jnp.where(is_ctx, ctx_ref[0], x_ref[0]) if split_residual else x_ref[0]
    y = _dot(a_ref[0], w_ref[...]) + b_ref[...]
    x1 = resid + _tab_row(tab_ref, is_ctx, 2) * y
    x1_ref[0] = x1
    h2 = _norm_mod(x1, g_ref[...], _tab_row(tab_ref, is_ctx, 3), _tab_row(tab_ref, is_ctx, 4))
    lg_ref[...] = _dot(h2.astype(BF16), rw_ref[...]) + rb_ref[...]
    hrow_ref[...] = _to_token_rows(h2)


def _proj(a, w, bias, resid, tab, g2, rw, rb, n_ctx, latent_only):
    split = isinstance(resid, tuple)
    b, l, d = a.shape
    tm = n_ctx
    off = n_ctx // tm if latent_only else 0
    nt = l // tm - off
    s_rows = d // LANES
    t = b * nt * tm
    kern = functools.partial(_proj_kernel, ctx_tiles=n_ctx // tm, tile_off=off, split_residual=split)
    if split:
        assert not latent_only and n_ctx == tm
        resid_specs = [pl.BlockSpec((1, tm, d), lambda i, j: (i, 0, 0)),
                       pl.BlockSpec((1, tm, d), lambda i, j: (i, jnp.maximum(j - 1, 0), 0))]
        resid_args = list(resid)
    else:
        resid_specs = [pl.BlockSpec((1, tm, d), lambda i, j: (i, j + off, 0))]
        resid_args = [resid]
    return pl.pallas_call(
        kern,
        grid=(b, nt),
        in_specs=[
            pl.BlockSpec((1, tm, d), lambda i, j: (i, j + off, 0)),
            pl.BlockSpec((d, d), lambda i, j: (0, 0)),
            pl.BlockSpec((1, d), lambda i, j: (0, 0)),
        ] + resid_specs + [
            pl.BlockSpec((1, TAB_ROWS, d), lambda i, j: (i, 0, 0)),
            pl.BlockSpec((1, d), lambda i, j: (0, 0)),
            pl.BlockSpec((d, LANES), lambda i, j: (0, 0)),
            pl.BlockSpec((1, LANES), lambda i, j: (0, 0)),
        ],
        out_specs=[
            pl.BlockSpec((1, tm, d), lambda i, j: (i, j, 0)),
            pl.BlockSpec((tm * s_rows, LANES), lambda i, j: (i * nt + j, 0)),
            pl.BlockSpec((tm, LANES), lambda i, j: (i * nt + j, 0)),
        ],
        out_shape=[
            jax.ShapeDtypeStruct((b, nt * tm, d), F32),
            jax.ShapeDtypeStruct((t * s_rows, LANES), F32),
            jax.ShapeDtypeStruct((t, LANES), F32),
        ],
        compiler_params=_cparams(("parallel", "parallel")),
        name="mixer_proj",
    )(a, w, bias.reshape(1, d), *resid_args, tab, g2.reshape(1, d), rw, rb)


def _route_kernel(lg_ref, e_ref, g_ref, r_ref, cnt_ref, carry_ref, *, n_exp):
    @pl.when(pl.program_id(0) == 0)
    def _():
        carry_ref[...] = jnp.zeros_like(carry_ref)

    lg = lg_ref[...]
    tm = lg.shape[0]
    lane = lax.broadcasted_iota(jnp.int32, lg.shape, 1).astype(F32)
    cur = jnp.where(lane < n_exp, lg, -jnp.inf)
    multi = jnp.zeros(lg.shape, F32)
    vals, idxs = [], []
    for _ in range(TOP_K):
        m = cur.max(axis=-1, keepdims=True)
        idx = jnp.where(cur == m, lane, float(LANES)).min(axis=-1, keepdims=True)
        sel = lane == idx
        multi = jnp.where(sel, 1.0, multi)
        cur = jnp.where(sel, -jnp.inf, cur)
        vals.append(m)
        idxs.append(idx)
    exps = [jnp.exp(v - vals[0]) for v in vals]
    den = exps[0]
    for e in exps[1:]:
        den = den + e
    tri = (lax.broadcasted_iota(jnp.int32, (tm, tm), 0) > lax.broadcasted_iota(jnp.int32, (tm, tm), 1))
    pref = _dot(jnp.where(tri, 1.0, 0.0).astype(BF16), multi.astype(BF16))
    tot = carry_ref[...] + pref
    e_out = jnp.zeros(lg.shape, F32)
    g_out = jnp.zeros(lg.shape, F32)
    r_out = jnp.zeros(lg.shape, F32)
    for k in range(TOP_K):
        rank_k = jnp.where(lane == idxs[k], tot, 0.0).sum(axis=-1, keepdims=True)
        e_out = jnp.where(lane == k, idxs[k], e_out)
        g_out = jnp.where(lane == k, exps[k] / den, g_out)
        r_out = jnp.where(lane == k, rank_k, r_out)
    e_ref[...] = e_out.astype(jnp.int32)
    g_ref[...] = g_out
    r_ref[...] = r_out.astype(jnp.int32)
    carry_ref[...] = carry_ref[...] + multi.sum(axis=0, keepdims=True)
    cnt_ref[...] = carry_ref[...]


def _route(logits, n_exp):
    t = logits.shape[0]
    tm = _pick(t, 256)
    spec = pl.BlockSpec((tm, LANES), lambda i: (i, 0))
    return pl.pallas_call(
        functools.partial(_route_kernel, n_exp=n_exp),
        grid=(t // tm,),
        in_specs=[spec],
        out_specs=[spec, spec, spec, pl.BlockSpec((1, LANES), lambda i: (0, 0))],
        out_shape=[
            jax.ShapeDtypeStruct((t, LANES), jnp.int32),
            jax.ShapeDtypeStruct((t, LANES), F32),
            jax.ShapeDtypeStruct((t, LANES), jnp.int32),
            jax.ShapeDtypeStruct((1, LANES), F32),
        ],
        scratch_shapes=[pltpu.VMEM((1, LANES), F32)],
        compiler_params=_cparams(("arbitrary",)),
        name="router",
    )(logits)


def _dispatch_kernel(dest_ref, h_ref, xs_ref, sem, *, td, s_rows):
    def issue(t, carry):
        src = h_ref.at[pl.ds(pl.multiple_of(t * s_rows, s_rows), s_rows), :]
        for k in range(TOP_K):
            d = dest_ref[0, 0, t * TOP_K + k]
            dst = xs_ref.at[pl.ds(pl.multiple_of(d * s_rows, s_rows), s_rows), :]
            pltpu.make_async_copy(src, dst, sem).start(priority=k % 2)
        return carry

    lax.fori_loop(0, td, issue, 0)
    for _ in range(TOP_K):
        pltpu.make_async_copy(h_ref, xs_ref.at[pl.ds(0, td * s_rows), :], sem).wait()


DISPATCH_TOKENS = 1024


def _dispatch(hrow, dest, s_rows):
    t = dest.shape[0] // TOP_K
    td = _pick(t, DISPATCH_TOKENS)
    nt = t // td
    return pl.pallas_call(
        functools.partial(_dispatch_kernel, td=td, s_rows=s_rows),
        grid=(nt,),
        in_specs=[
            pl.BlockSpec((1, 1, td * TOP_K), lambda i: (i, 0, 0), memory_space=pltpu.SMEM),
            pl.BlockSpec((td * s_rows, LANES), lambda i: (i, 0)),
        ],
        out_specs=pl.BlockSpec(memory_space=pl.ANY),
        out_shape=jax.ShapeDtypeStruct((t * TOP_K * s_rows, LANES), F32),
        scratch_shapes=[pltpu.SemaphoreType.DMA(())],
        compiler_params=pltpu.CompilerParams(dimension_semantics=("arbitrary",), vmem_limit_bytes=VMEM_LIMIT,
                                             has_side_effects=True),
        name="moe_dispatch",
    )(dest.reshape(nt, 1, td * TOP_K), hrow)


def _gmm_kernel(it_tile, it_e, it_lo, it_hi, it_first, it_valid,
                xs_ref, wg_ref, wl_ref, bg_ref, bl_ref, wd_ref, bd_ref, ys_ref, xb_ref, *, tg, s_rows, fk):
    m = pl.program_id(0)
    tsub = _pick(tg, 128)
    f = wd_ref.shape[1]

    @pl.when(it_valid[m] == 1)
    def _():
        for t0 in range(0, tg, tsub):
            xt = _from_token_rows(xs_ref[pl.ds(t0 * s_rows, tsub * s_rows), :], s_rows)
            for s in range(s_rows):
                xb_ref[pl.ds(t0, tsub), s * LANES:(s + 1) * LANES] = xt[s].astype(BF16)

        x = xb_ref[...]
        y = None
        for f0 in range(0, f, fk):
            cols = slice(f0, f0 + fk)
            glu = jnp.minimum(_dot(x, wg_ref[0, :, cols]) + bg_ref[0, :, cols], SWIGLU_LIMIT)
            lin = jnp.clip(_dot(x, wl_ref[0, :, cols]) + bl_ref[0, :, cols], -SWIGLU_LIMIT, SWIGLU_LIMIT)
            act = glu * jax.nn.sigmoid(SWIGLU_ALPHA * glu) * (lin + 1.0)
            part = _dot(act.astype(BF16), wd_ref[0, cols, :])
            y = part if y is None else y + part
        y = y + bd_ref[0]

        for t0 in range(0, tg, tsub):
            rows = pl.ds(t0 * s_rows, tsub * s_rows)
            yr = _to_token_rows(y[t0:t0 + tsub])
            row = t0 * s_rows + lax.broadcasted_iota(jnp.int32, (tsub * s_rows, 1), 0)
            mine = (row >= it_lo[m] * s_rows) & (row < it_hi[m] * s_rows)

            @pl.when(it_first[m] == 1)
            def _():
                ys_ref[rows, :] = jnp.where(mine, yr, 0.0)

            @pl.when(it_first[m] == 0)
            def _():
                ys_ref[rows, :] = jnp.where(mine, yr, ys_ref[rows, :])


def _gmm_items(counts, n_tiles, tg, max_items):
    n_exp = counts.shape[0]
    ends = jnp.cumsum(counts)
    starts = ends - counts
    def count_le(sorted_vals, q):
        return jnp.sum(sorted_vals[None, :] <= q[:, None], axis=1, dtype=jnp.int32)

    tile0 = jnp.arange(n_tiles, dtype=jnp.int32) * tg
    e_lo = jnp.minimum(count_le(ends, tile0), n_exp - 1)
    e_hi = jnp.minimum(count_le(ends, tile0 + tg - 1), n_exp - 1)
    n_items = e_hi - e_lo + 1
    item_end = jnp.cumsum(n_items)
    item_start = item_end - n_items
    total = item_end[-1]
    m = jnp.arange(max_items, dtype=jnp.int32)
    valid = m < total
    tile = jnp.minimum(count_le(item_end, m), n_tiles - 1)
    e = jnp.where(valid, e_lo[tile] + (m - item_start[tile]), e_hi[n_tiles - 1]).astype(jnp.int32)
    lo = jnp.clip(starts[e] - tile * tg, 0, tg).astype(jnp.int32)
    hi = jnp.clip(ends[e] - tile * tg, 0, tg).astype(jnp.int32)
    first = (m == item_start[tile]).astype(jnp.int32)
    return tile, e, lo, hi, first, valid.astype(jnp.int32)


GMM_ROW_TILE = 256
GMM_F_CHUNK = 1024


def _gmm(xs, counts, w, layer, s_rows):
    n_exp = counts.shape[0]
    d, f = w["wg"].shape[1:]
    p = xs.shape[0] // s_rows
    tg = _pick(p, GMM_ROW_TILE)
    fk = _pick(f, GMM_F_CHUNK, LANES)
    n_tiles = p // tg
    max_items = n_tiles + n_exp - 1
    tile, e, lo, hi, first, valid = _gmm_items(counts, n_tiles, tg, max_items)
    items = (tile, e + layer * n_exp, lo, hi, first, valid)

    def wspec(block, buffers=1):
        return pl.BlockSpec(block, lambda m, t, e, lo, hi, fi, va: (e[m], 0, 0), pipeline_mode=pl.Buffered(buffers))

    rows_spec = pl.BlockSpec((tg * s_rows, LANES), lambda m, t, e, lo, hi, fi, va: (t[m], 0))
    grid_spec = pltpu.PrefetchScalarGridSpec(
        num_scalar_prefetch=6,
        grid=(max_items,),
        in_specs=[
            rows_spec,
            wspec((1, d, f), 2),
            wspec((1, d, f), 2),
            wspec((1, 1, f), 2),
            wspec((1, 1, f), 2),
            wspec((1, f, d)),
            wspec((1, 1, d), 2),
        ],
        out_specs=rows_spec,
        scratch_shapes=[pltpu.VMEM((tg, d), BF16)],
    )
    return pl.pallas_call(
        functools.partial(_gmm_kernel, tg=tg, s_rows=s_rows, fk=fk),
        grid_spec=grid_spec,
        out_shape=jax.ShapeDtypeStruct(xs.shape, F32),
        compiler_params=_cparams(("arbitrary",)),
        name="moe_experts",
    )(*items, xs, w["wg"], w["wl"], w["bg"], w["bl"], w["wd"], w["bd"])


def _combine_kernel(dest_ref, dest_next_ref, ys_ref, gate_ref, x_ref, tab_ref, tabn_ref, g_ref,
                    x2_ref, h_ref, buf_ref, moe_ref, sem, *, tc, s_rows, ctx_tiles, n_steps, mode):
    step = pl.program_id(0) * pl.num_programs(1) + pl.program_id(1)
    slot = step % 2

    def gather(dref, slot_idx):
        def issue(t, carry):
            for k in range(TOP_K):
                d = dref[0, 0, t * TOP_K + k]
                src = ys_ref.at[pl.ds(pl.multiple_of(d * s_rows, s_rows), s_rows), :]
                dst = buf_ref.at[slot_idx, pl.ds(pl.multiple_of((k * tc + t) * s_rows, s_rows), s_rows), :]
                pltpu.make_async_copy(src, dst, sem.at[slot_idx]).start(priority=k % 2)
            return carry

        lax.fori_loop(0, tc, issue, 0)

    @pl.when(step == 0)
    def _():
        gather(dest_ref, 0)

    @pl.when(step + 1 < n_steps)
    def _():
        gather(dest_next_ref, 1 - slot)

    pltpu.make_async_copy(buf_ref.at[slot], buf_ref.at[slot], sem.at[slot]).wait()

    gates = gate_ref[...]
    for k in range(TOP_K):
        chunks = _from_token_rows(buf_ref[slot, pl.ds(k * tc * s_rows, tc * s_rows), :], s_rows)
        for s in range(s_rows):
            term = chunks[s] * gates[:, k:k + 1]
            if k == 0:
                moe_ref[:, s * LANES:(s + 1) * LANES] = term
            else:
                moe_ref[:, s * LANES:(s + 1) * LANES] += term

    is_ctx = pl.program_id(1) < ctx_tiles
    x2 = x_ref[0] + _tab_row(tab_ref, is_ctx, 5) * moe_ref[...]
    x2_ref[0] = x2
    if mode == "next":
        h = _norm_mod(x2, g_ref[...], _tab_row(tabn_ref, is_ctx, 0), _tab_row(tabn_ref, is_ctx, 1))
        h_ref[0] = h.astype(h_ref.dtype)
    else:
        y = x2 * lax.rsqrt(jnp.mean(x2 * x2, axis=-1, keepdims=True) + RMS_EPS)
        h_ref[0] = (y * g_ref[...]).astype(h_ref.dtype)


def _combine(ys, dest, gates, x1, tab, tab_next, g_next, n_ctx_rows, s_rows, mode):
    b, l, d = x1.shape
    tc = _pick(min(l, 128) if n_ctx_rows == 0 else n_ctx_rows, 128)
    nt = l // tc
    n_steps = b * nt
    dest3 = dest.reshape(n_steps, 1, tc * TOP_K)
    kern = functools.partial(_combine_kernel, tc=tc, s_rows=s_rows, ctx_tiles=n_ctx_rows // tc,
                             n_steps=n_steps, mode=mode)
    out_dtype = BF16 if mode == "next" else F32
    return pl.pallas_call(
        kern,
        grid=(b, nt),
        in_specs=[
            pl.BlockSpec((1, 1, tc * TOP_K), lambda i, j: (i * nt + j, 0, 0), memory_space=pltpu.SMEM),
            pl.BlockSpec((1, 1, tc * TOP_K), lambda i, j: (jnp.minimum(i * nt + j + 1, n_steps - 1), 0, 0),
                         memory_space=pltpu.SMEM),
            pl.BlockSpec(memory_space=pl.ANY),
            pl.BlockSpec((tc, LANES), lambda i, j: (i * nt + j, 0)),
            pl.BlockSpec((1, tc, d), lambda i, j: (i, j, 0)),
            pl.BlockSpec((1, TAB_ROWS, d), lambda i, j: (i, 0, 0)),
            pl.BlockSpec((1, TAB_ROWS, d), lambda i, j: (i, 0, 0)),
            pl.BlockSpec((1, d), lambda i, j: (0, 0)),
        ],
        out_specs=[
            pl.BlockSpec((1, tc, d), lambda i, j: (i, j, 0)),
            pl.BlockSpec((1, tc, d), lambda i, j: (i, j, 0)),
        ],
        out_shape=[
            jax.ShapeDtypeStruct((b, l, d), F32),
            jax.ShapeDtypeStruct((b, l, d), out_dtype),
        ],
        scratch_shapes=[
            pltpu.VMEM((2, tc * TOP_K * s_rows, LANES), F32),
            pltpu.VMEM((tc, d), F32),
            pltpu.SemaphoreType.DMA((2,)),
        ],
        compiler_params=_cparams(("arbitrary", "arbitrary")),
        name="moe_combine_" + mode,
    )(dest3, dest3, ys, gates, x1, tab, tab_next, g_next.reshape(1, d))


def _moe(hrow, logits, x1, tab, tab_next, g_next, w, layer, n_exp, n_ctx_rows, mode):
    s_rows = x1.shape[-1] // LANES
    e_pad, g_pad, r_pad, cnt = _route(logits, n_exp)
    top_e, rank = e_pad[:, :TOP_K], r_pad[:, :TOP_K]
    counts = cnt[0, :n_exp].astype(jnp.int32)
    starts = jnp.cumsum(counts) - counts
    first_slot = jnp.sum(jnp.where(top_e[..., None] == jnp.arange(n_exp, dtype=jnp.int32), starts, 0), axis=-1)
    dest = (first_slot + rank).reshape(-1)
    xs = _dispatch(hrow, dest, s_rows)
    ys = _gmm(xs, counts, w, layer, s_rows)
    return _combine(ys, dest, g_pad, x1, tab, tab_next, g_next, n_ctx_rows, s_rows, mode)


def _conv_kernel(x_ref, w_ref, b_ref, o_ref, *, n_ctx):
    l = x_ref.shape[1]
    w = w_ref[...]
    for s0, n in ((0, n_ctx), (n_ctx, l - n_ctx)):
        x = x_ref[0, s0:s0 + n, :]
        row = lax.broadcasted_iota(jnp.int32, (n, 1), 0)
        acc = x * w[CONV_LEFT:CONV_LEFT + 1]
        for j in range(w.shape[0]):
            off = j - CONV_LEFT
            if off == 0:
                continue
            shifted = pltpu.roll(x, (-off) % n, 0)
            ok = (row + off >= 0) & (row + off < n)
            acc = acc + jnp.where(ok, shifted, 0.0) * w[j:j + 1]
        o_ref[0, s0:s0 + n, :] = acc + b_ref[...]


def _conv(xb, conv_w, conv_b, n_ctx):
    b, l, d = xb.shape
    dt = _pick(d, 256, LANES)
    return pl.pallas_call(
        functools.partial(_conv_kernel, n_ctx=n_ctx),
        grid=(b, d // dt),
        in_specs=[
            pl.BlockSpec((1, l, dt), lambda i, j: (i, 0, j)),
            pl.BlockSpec((conv_w.shape[0], dt), lambda i, j: (0, j)),
            pl.BlockSpec((1, dt), lambda i, j: (0, j)),
        ],
        out_specs=pl.BlockSpec((1, l, dt), lambda i, j: (i, 0, j)),
        out_shape=jax.ShapeDtypeStruct((b, l, d), F32),
        compiler_params=_cparams(("parallel", "parallel")),
        name="rg_conv",
    )(xb, conv_w, conv_b.reshape(1, d))


SCAN_PAD = 8
SCAN_UNROLL = 8


def _scan_kernel(*refs, nb, tc, nh, reverse):
    n_in = 8 if reverse else 6
    u_ref, wa_ref, ba_ref, wi_ref, bi_ref, lam_ref = refs[:6]
    o_ref = refs[n_in]
    scratch = refs[n_in + 1:]
    a_s, x_s, h_s = scratch[0:nh], scratch[nh:2 * nh], scratch[2 * nh:3 * nh]
    carry = scratch[3 * nh]
    ts = tc + SCAN_PAD

    @pl.when(pl.program_id(1) == 0)
    def _():
        carry[...] = jnp.zeros_like(carry)

    neg = -lam_ref[0]
    softplus = jnp.maximum(neg, 0.0) + jnp.log1p(jnp.exp(-jnp.abs(neg)))
    for bi in range(nb):
        u = u_ref[bi]
        ub = u.astype(BF16)
        r = jax.nn.sigmoid(_dot(ub, wa_ref[0, 0]) + ba_ref[0, 0])
        i = jax.nn.sigmoid(_dot(ub, wi_ref[0, 0]) + bi_ref[0, 0])
        log_a = (-RG_C) * r * softplus
        a = jnp.exp(log_a)
        xin = jnp.sqrt(1.0 - a * a) * (i * u)
        for p in range(nh):
            a_s[p][pl.ds(bi * ts, tc), :] = a[:, p * LANES:(p + 1) * LANES]
            x_s[p][pl.ds(bi * ts, tc), :] = xin[:, p * LANES:(p + 1) * LANES]

    def block(j, hs):
        hs = list(hs)
        for q in range(SCAN_UNROLL):
            t = j * SCAN_UNROLL + q
            if reverse:
                t = tc - 1 - t
            for p in range(nh):
                hs[p] = a_s[p][pl.ds(t, nb, stride=ts), :] * hs[p] + x_s[p][pl.ds(t, nb, stride=ts), :]
                h_s[p][pl.ds(t, nb, stride=ts), :] = hs[p]
        return tuple(hs)

    h0 = tuple(carry[:, p * LANES:(p + 1) * LANES] for p in range(nh))
    hs = lax.fori_loop(0, tc // SCAN_UNROLL, block, h0)
    for p in range(nh):
        carry[:, p * LANES:(p + 1) * LANES] = hs[p]
    for bi in range(nb):
        for p in range(nh):
            h = h_s[p][pl.ds(bi * ts, tc), :]
            cols = slice(p * LANES, (p + 1) * LANES)
            if reverse:
                hf_ref, gy_ref = refs[6], refs[7]
                o_ref[bi, :, cols] = ((hf_ref[bi, :, cols] + h) * gy_ref[bi, :, cols]).astype(o_ref.dtype)
            else:
                o_ref[bi, :, cols] = h


def _scan(u, w_a, b_a, w_i, b_i, lam, dirn, n_ctx, h_fwd=None, gy=None):
    b, l, d = u.shape
    n_blk, w = w_a.shape[1], w_a.shape[2]
    tc = _pick(n_ctx, 128)
    nt, nc = l // tc, n_ctx // tc
    reverse = dirn == 1

    def chunk(j):
        if not reverse:
            return j
        return jnp.where(j < nc, nc - 1 - j, nt - 1 - (j - nc))

    blk = pl.BlockSpec((b, tc, w), lambda g, j: (0, chunk(j), g))
    in_specs = [
        blk,
        pl.BlockSpec((1, 1, w, w), lambda g, j: (dirn, g, 0, 0)),
        pl.BlockSpec((1, 1, 1, w), lambda g, j: (dirn, g, 0, 0)),
        pl.BlockSpec((1, 1, w, w), lambda g, j: (dirn, g, 0, 0)),
        pl.BlockSpec((1, 1, 1, w), lambda g, j: (dirn, g, 0, 0)),
        pl.BlockSpec((1, 1, w), lambda g, j: (dirn, 0, g)),
    ]
    args = [u, w_a, b_a.reshape(2, n_blk, 1, w), w_i, b_i.reshape(2, n_blk, 1, w), lam.reshape(2, 1, d)]
    if reverse:
        in_specs += [blk, blk]
        args += [h_fwd, gy]
    ts = tc + SCAN_PAD
    return pl.pallas_call(
        functools.partial(_scan_kernel, nb=b, tc=tc, nh=w // LANES, reverse=reverse),
        grid=(n_blk, nt),
        in_specs=in_specs,
        out_specs=blk,
        out_shape=jax.ShapeDtypeStruct((b, l, d), BF16 if reverse else F32),
        scratch_shapes=[pltpu.VMEM((b * ts, LANES), F32)] * (3 * (w // LANES)) + [pltpu.VMEM((b, w), F32)],
        compiler_params=_cparams(("parallel", "arbitrary")),
        name="rg_scan_" + ("rev" if reverse else "fwd"),
    )(*args)


def _mod_tables(mod_out, b, d):
    tabs = []
    for i in range(mod_out.shape[0]):
        ml = mod_out[i, :b].reshape(b, MOD_ROWS, d)
        mc = jnp.broadcast_to(mod_out[i, b].reshape(1, MOD_ROWS, d), (b, MOD_ROWS, d))
        pad = jnp.zeros((b, TAB_ROWS - 2 * MOD_ROWS, d), F32)
        tabs.append(jnp.concatenate([mc, ml, pad], axis=1))
    return tabs


def _moe_weights(router_w, router_b, wg, wl, b_gu, wd, b_dn):
    depth, d, n_exp = router_w.shape
    f = wg.shape[-1]
    return {
        "rw": jnp.pad(router_w, ((0, 0), (0, 0), (0, LANES - n_exp))).astype(BF16),
        "rb": jnp.pad(router_b, ((0, 0), (0, LANES - n_exp))).reshape(depth, 1, LANES),
        "wg": wg.reshape(depth * n_exp, d, f),
        "wl": wl.reshape(depth * n_exp, d, f),
        "bg": b_gu[..., 0::2].reshape(depth * n_exp, 1, f),
        "bl": b_gu[..., 1::2].reshape(depth * n_exp, 1, f),
        "wd": wd.reshape(depth * n_exp, f, d),
        "bd": b_dn.reshape(depth * n_exp, 1, d),
    }


def kernel(x, c, ctx, c_ctx, mod_w, mod_b, norm1_g, norm2_g, final_g, na_w_qkv, na_w_o, na_rpb, rg_w_y, rg_b_y,
           rg_w_x, rg_b_x, rg_conv_w, rg_conv_b, rg_w_a, rg_b_a, rg_w_i, rg_b_i, rg_lam, rg_w_out, rg_b_out,
           moe_router_w, moe_router_b, moe_w_gu, moe_b_gu, moe_w_dn, moe_b_dn):
    b, n_lat, d = x.shape
    n_ctx = ctx.shape[1]
    l = n_ctx + n_lat
    heads = na_rpb.shape[1]
    rows = n_lat // GRID_W
    kr = min((na_rpb.shape[2] + 1) // 2, rows)
    assert mod_w.shape[0] == 2 and n_lat % n_ctx == 0 and d % LANES == 0

    mod_rows = -(-(b + 1) // 8) * 8
    cc = jnp.concatenate([c, c_ctx[None, :], jnp.zeros((mod_rows - b - 1, d), F32)], axis=0)
    tab0, tab1 = _mod_tables(_modulation(cc, mod_w, mod_b), b, d)

    h = _prenorm(ctx, x, tab0, norm1_g[0])
    qkv, wd = _matmul_and_cast(h.reshape(b * l, d), na_w_qkv[0].astype(BF16), moe_w_dn.reshape(-1, moe_w_dn.shape[-1]))
    bias = _na_bias_table(na_rpb[0], rows, kr)
    o, wg, wl = _attention(qkv.reshape(b, l, 3 * d), bias, n_ctx, heads, moe_w_gu.reshape(-1, moe_w_gu.shape[-1]))
    n_exp = moe_router_w.shape[2]
    w = _moe_weights(moe_router_w, moe_router_b, wg, wl, moe_b_gu, wd, moe_b_dn)
    x1, hrow, logits = _proj(o, na_w_o[0].astype(BF16), jnp.zeros((d,), F32), (ctx, x), tab0, norm2_g[0],
                             w["rw"][0], w["rb"][0], n_ctx, latent_only=False)
    x2, h = _moe(hrow, logits, x1, tab0, tab1, norm1_g[1], w, 0, n_exp, n_ctx, "next")

    hf = h.reshape(b * l, d)
    xb = _matmul(hf, rg_w_x[0].astype(BF16), rg_b_x[0], F32).reshape(b, l, d)
    gy = _matmul(hf, rg_w_y[0].astype(BF16), rg_b_y[0], F32, act="gelu").reshape(b, l, d)
    u = _conv(xb, rg_conv_w[0], rg_conv_b[0], n_ctx)
    w_a, w_i = rg_w_a[0].astype(BF16), rg_w_i[0].astype(BF16)
    h_fwd = _scan(u, w_a, rg_b_a[0], w_i, rg_b_i[0], rg_lam[0], 0, n_ctx)
    hg = _scan(u, w_a, rg_b_a[0], w_i, rg_b_i[0], rg_lam[0], 1, n_ctx, h_fwd, gy)
    x1, hrow, logits = _proj(hg, rg_w_out[0].astype(BF16), rg_b_out[0], x2, tab1, norm2_g[1],
                             w["rw"][1], w["rb"][1], n_ctx, latent_only=True)
    _, out = _moe(hrow, logits, x1, tab1, tab1, final_g, w, 1, n_exp, 0, "final")
    return out
```
